```python
import math
import jax, jax.numpy as jnp
from jax import lax
import numpy as np

D_MODEL = 1024
BATCH = 32
SEQ = 2048
DEPTH = 4
DEC_BATCH = 16
DEC_SEQ = 32
PAST_LEN = 1024

CHUNK = 64
N_HEADS = 8
HEAD_DIM = 64
D_ATT = N_HEADS * HEAD_DIM
N_PAST_CHUNKS = 8
REL_MAX = 128
N_REL = CHUNK - 1 + REL_MAX + 1
SSM_GROUP = 16
N_SSM_GROUPS = 16
D_SSM = SSM_GROUP * N_SSM_GROUPS
SSM_STATE = 64
SCONV_HEADS = 4
D_SCONV = 256
CONV_K = 3
D_MIX = D_ATT + D_SSM + D_SCONV
D_IN_PROJ = 3 * D_ATT + D_SSM + 3 * D_SCONV
SPLITS = [D_ATT, 2 * D_ATT, 3 * D_ATT, 3 * D_ATT + D_SSM,
          3 * D_ATT + D_SSM + D_SCONV, 3 * D_ATT + D_SSM + 2 * D_SCONV]
D_FF = 2048
ALPHA = (2 * DEPTH) ** 0.25
BETA = (8 * DEPTH) ** -0.25
LN_EPS = 1e-5

kernel_name = 'hybrid_streaming_encoder_step'


def layer_norm(x, g, b):
    xf = x.astype(jnp.float32)
    mu = jnp.mean(xf, axis=-1, keepdims=True)
    var = jnp.mean(jnp.square(xf - mu), axis=-1, keepdims=True)
    y = (xf - mu) * lax.rsqrt(var + LN_EPS) * g.astype(jnp.float32) + b.astype(jnp.float32)
    return y.astype(x.dtype)


def causal_dwconv(v, buf, w, bias):
    s = v.shape[1]
    vp = jnp.concatenate([buf.astype(v.dtype), v], axis=1)
    y = sum(vp[:, k:k + s] * w[k] for k in range(CONV_K)) + bias
    return y, vp[:, -(CONV_K - 1):]


def rel_index(dist):
    return jnp.clip(dist, -(CHUNK - 1), REL_MAX) + (CHUNK - 1)


def attend_block(qc, kc, vc, bias, valid):
    scores = jnp.einsum('bthd,bshd->bhts', qc, kc).astype(jnp.float32) * (HEAD_DIM ** -0.5)
    scores = scores + bias.astype(jnp.float32)[None]
    if valid is not None:
        scores = jnp.where(valid, scores, -jnp.inf)
    p = jax.nn.softmax(scores, axis=-1)
    return jnp.einsum('bhts,bshd->bthd', p.astype(vc.dtype), vc)


def chunk_attention_prompt(q, k, v, rel_bias):
    bsz, s = q.shape[:2]
    n_chunks = s // CHUNK
    pad = N_PAST_CHUNKS * CHUNK
    band = pad + CHUNK
    kp = jnp.pad(k, ((0, 0), (pad, 0), (0, 0), (0, 0)))
    vp = jnp.pad(v, ((0, 0), (pad, 0), (0, 0), (0, 0)))
    t_idx = jnp.arange(CHUNK)[:, None]
    s_idx = jnp.arange(band)[None, :]
    bias = rel_bias[:, rel_index(t_idx + pad - s_idx)]

    def one_chunk(c):
        start = c * CHUNK
        qc = lax.dynamic_slice_in_dim(q, start, CHUNK, axis=1)
        kc = lax.dynamic_slice_in_dim(kp, start, band, axis=1)
        vc = lax.dynamic_slice_in_dim(vp, start, band, axis=1)
        valid = s_idx >= pad - start
        return attend_block(qc, kc, vc, bias, valid)

    out = lax.map(one_chunk, jnp.arange(n_chunks))
    return out.transpose(1, 0, 2, 3, 4).reshape(bsz, s, D_ATT)


def chunk_attention_sample(q, k, v, k_past, v_past, rel_bias):
    bsz, t = q.shape[:2]
    n_past = k_past.shape[1]
    kc = jnp.concatenate([k_past.astype(k.dtype), k], axis=1)
    vc = jnp.concatenate([v_past.astype(v.dtype), v], axis=1)
    t_idx = jnp.arange(t)[:, None]
    s_idx = jnp.arange(n_past + t)[None, :]
    bias = rel_bias[:, rel_index(t_idx + n_past - s_idx)]
    return attend_block(q, kc, vc, bias, None).reshape(bsz, t, D_ATT)


def _scan_combine(e1, e2):
    a1, b1 = e1
    a2, b2 = e2
    return a2 * a1, a2 * b1 + b2


def s5_ssm(u, h0_re, h0_im, lam_re, lam_im, log_dt, b_re, b_im, c_re, c_im, d_skip, block):
    f32 = jnp.float32
    bsz, s, _ = u.shape
    lam = lax.complex(lam_re.astype(f32), lam_im.astype(f32))
    dt = jnp.exp(log_dt.astype(f32))[:, None]
    lbar = jnp.exp(lam * dt)
    bbar = ((lbar - 1.0) / lam)[:, :, None] * lax.complex(b_re.astype(f32), b_im.astype(f32))
    cmat = lax.complex(c_re.astype(f32), c_im.astype(f32))
    uf = u.astype(f32)
    ub = uf.reshape(bsz, s // block, block, N_SSM_GROUPS, SSM_GROUP).transpose(1, 0, 2, 3, 4)

    def step(h, u_blk):
        bu = jnp.einsum('btgc,gpc->btgp', u_blk.astype(jnp.complex64), bbar)
        bu = bu.at[:, 0].add(lbar * h)
        a = jnp.broadcast_to(lbar, bu.shape)
        _, hs = lax.associative_scan(_scan_combine, (a, bu), axis=1)
        y = jnp.einsum('btgp,gcp->btgc', hs, cmat).real
        return hs[:, -1], y

    h0 = lax.complex(h0_re.astype(f32), h0_im.astype(f32))
    h_final, ys = lax.scan(step, h0, ub)
    y = ys.transpose(1, 0, 2, 3, 4).reshape(bsz, s, D_SSM) + d_skip.astype(f32) * uf
    return y, h_final.real, h_final.imag


def trunk_layer(x, lp, past, is_prompt):
    bsz, s, _ = x.shape
    proj = jnp.einsum('bsd,de->bse', x, lp['w_in'])
    q, k, v, u, gate_b, gate_c, xv = jnp.split(proj, SPLITS, axis=-1)
    q = q.reshape(bsz, s, N_HEADS, HEAD_DIM)
    k = k.reshape(bsz, s, N_HEADS, HEAD_DIM)
    v = v.reshape(bsz, s, N_HEADS, HEAD_DIM)

    if is_prompt:
        att = chunk_attention_prompt(q, k, v, lp['rel_bias'])
        h_re0 = jnp.zeros((bsz, N_SSM_GROUPS, SSM_STATE), jnp.float32)
        h_im0 = jnp.zeros((bsz, N_SSM_GROUPS, SSM_STATE), jnp.float32)
        sconv_buf = jnp.zeros((bsz, CONV_K - 1, D_SCONV), x.dtype)
        ffn_buf = jnp.zeros((bsz, CONV_K - 1, 2 * D_FF), x.dtype)
        block = CHUNK
        keep = min(N_PAST_CHUNKS * CHUNK, s)
        new_k, new_v = k[:, s - keep:], v[:, s - keep:]
    else:
        k_past, v_past, h_re0, h_im0, sconv_buf, ffn_buf = past
        att = chunk_attention_sample(q, k, v, k_past, v_past, lp['rel_bias'])
        block = s
        new_k, new_v = k, v

    y_ssm, h_re, h_im = s5_ssm(u, h_re0, h_im0, lp['lam_re'], lp['lam_im'], lp['log_dt'],
                               lp['b_re'], lp['b_im'], lp['c_re'], lp['c_im'], lp['d'], block)
    z = jax.nn.gelu(y_ssm)
    ssm_out = (z * jax.nn.sigmoid(z @ lp['w_glu'].astype(jnp.float32))).astype(x.dtype)

    conv_out, new_sconv = causal_dwconv(gate_c * xv, sconv_buf, lp['sconv_w'], lp['sconv_b'])
    sconv_out = gate_b * conv_out

    mix = jnp.einsum('bse,ed->bsd', jnp.concatenate([att, ssm_out, sconv_out], axis=-1), lp['w_out'])
    x = layer_norm(ALPHA * x + mix, lp['ln1_g'], lp['ln1_b'])

    up = jnp.einsum('bsd,df->bsf', x, lp['w_ff_in'])
    up_c, new_ffn = causal_dwconv(up, ffn_buf, lp['ffn_conv_w'], lp['ffn_conv_b'])
    g, val = jnp.split(up_c, 2, axis=-1)
    ffn = jnp.einsum('bsf,fd->bsd', jax.nn.silu(g) * val, lp['w_ff_out'])
    x = layer_norm(ALPHA * x + ffn, lp['ln2_g'], lp['ln2_b'])
    return x, (new_k, new_v, h_re, h_im, new_sconv, new_ffn)


def setup_inputs(seed: int = 0) -> dict:
    key = jax.random.key(seed)
    keys = jax.random.split(key, 40)
    counter = [0]

    def nrm(shape, scale):
        kk = keys[counter[0]]
        counter[0] += 1
        return jax.random.normal(kk, shape, jnp.float32) * scale

    att_rows = min(N_PAST_CHUNKS * CHUNK, PAST_LEN)
    n_idx = jnp.arange(SSM_STATE, dtype=jnp.float32)
    gp = (DEPTH, N_SSM_GROUPS, SSM_STATE)
    inputs = {}
    inputs['x_prompt'] = nrm((BATCH, SEQ, D_MODEL), 1.0)
    inputs['x_sample'] = nrm((DEC_BATCH, DEC_SEQ, D_MODEL), 1.0)
    inputs['cache_k'] = nrm((DEPTH, DEC_BATCH, att_rows, N_HEADS, HEAD_DIM), 1.0)
    inputs['cache_v'] = nrm((DEPTH, DEC_BATCH, att_rows, N_HEADS, HEAD_DIM), 1.0)
    inputs['state_ssm_re'] = nrm((DEPTH, DEC_BATCH, N_SSM_GROUPS, SSM_STATE), 0.5)
    inputs['state_ssm_im'] = nrm((DEPTH, DEC_BATCH, N_SSM_GROUPS, SSM_STATE), 0.5)
    inputs['cache_sconv'] = nrm((DEPTH, DEC_BATCH, CONV_K - 1, D_SCONV), 1.0)
    inputs['cache_ffn_conv'] = nrm((DEPTH, DEC_BATCH, CONV_K - 1, 2 * D_FF), 1.0)
    inputs['ln_in_g'] = 1.0 + nrm((D_MODEL,), 0.01)
    inputs['ln_in_b'] = nrm((D_MODEL,), 0.01)
    inputs['w_in'] = nrm((DEPTH, D_MODEL, D_IN_PROJ), D_MODEL ** -0.5)
    inputs['rel_bias'] = nrm((DEPTH, N_HEADS, N_REL), 0.1)
    inputs['ssm_lam_re'] = -0.5 + nrm(gp, 0.01)
    inputs['ssm_lam_im'] = jnp.pi * n_idx + nrm(gp, 0.01)
    inputs['ssm_log_dt'] = jax.random.uniform(keys[counter[0]], (DEPTH, N_SSM_GROUPS), jnp.float32,
                                              math.log(0.001), math.log(0.1))
    counter[0] += 1
    inputs['ssm_b_re'] = nrm((DEPTH, N_SSM_GROUPS, SSM_STATE, SSM_GROUP), (2 * SSM_GROUP) ** -0.5)
    inputs['ssm_b_im'] = nrm((DEPTH, N_SSM_GROUPS, SSM_STATE, SSM_GROUP), (2 * SSM_GROUP) ** -0.5)
    inputs['ssm_c_re'] = nrm((DEPTH, N_SSM_GROUPS, SSM_GROUP, SSM_STATE), (2 * SSM_STATE) ** -0.5)
    inputs['ssm_c_im'] = nrm((DEPTH, N_SSM_GROUPS, SSM_GROUP, SSM_STATE), (2 * SSM_STATE) ** -0.5)
    inputs['ssm_d'] = nrm((DEPTH, D_SSM), 0.5)
    inputs['w_glu'] = nrm((DEPTH, D_SSM, D_SSM), D_SSM ** -0.5)
    inputs['sconv_w'] = nrm((DEPTH, CONV_K, D_SCONV), CONV_K ** -0.5)
    inputs['sconv_b'] = nrm((DEPTH, D_SCONV), 0.01)
    inputs['w_out'] = nrm((DEPTH, D_MIX, D_MODEL), BETA * D_MIX ** -0.5)
    inputs['ln1_g'] = 1.0 + nrm((DEPTH, D_MODEL), 0.01)
    inputs['ln1_b'] = nrm((DEPTH, D_MODEL), 0.01)
    inputs['w_ff_in'] = nrm((DEPTH, D_MODEL, 2 * D_FF), D_MODEL ** -0.5)
    inputs['ffn_conv_w'] = nrm((DEPTH, CONV_K, 2 * D_FF), CONV_K ** -0.5)
    inputs['ffn_conv_b'] = nrm((DEPTH, 2 * D_FF), 0.01)
    inputs['w_ff_out'] = nrm((DEPTH, D_FF, D_MODEL), BETA * D_FF ** -0.5)
    inputs['ln2_g'] = 1.0 + nrm((DEPTH, D_MODEL), 0.01)
    inputs['ln2_b'] = nrm((DEPTH, D_MODEL), 0.01)
    return inputs


def reference(x_prompt, x_sample, cache_k, cache_v, state_ssm_re, state_ssm_im, cache_sconv,
              cache_ffn_conv, ln_in_g, ln_in_b, w_in, rel_bias, ssm_lam_re, ssm_lam_im, ssm_log_dt,
              ssm_b_re, ssm_b_im, ssm_c_re, ssm_c_im, ssm_d, w_glu, sconv_w, sconv_b, w_out,
              ln1_g, ln1_b, w_ff_in, ffn_conv_w, ffn_conv_b, w_ff_out, ln2_g, ln2_b):
    xp = layer_norm(x_prompt, ln_in_g, ln_in_b)
    xs = layer_norm(x_sample, ln_in_g, ln_in_b)
    st_p = [[] for _ in range(6)]
    st_s = [[] for _ in range(6)]
    for l in range(DEPTH):
        lp = {'w_in': w_in[l], 'rel_bias': rel_bias[l], 'lam_re': ssm_lam_re[l], 'lam_im': ssm_lam_im[l],
              'log_dt': ssm_log_dt[l], 'b_re': ssm_b_re[l], 'b_im': ssm_b_im[l], 'c_re': ssm_c_re[l],
              'c_im': ssm_c_im[l], 'd': ssm_d[l], 'w_glu': w_glu[l], 'sconv_w': sconv_w[l],
              'sconv_b': sconv_b[l], 'w_out': w_out[l], 'ln1_g': ln1_g[l], 'ln1_b': ln1_b[l],
              'w_ff_in': w_ff_in[l], 'ffn_conv_w': ffn_conv_w[l], 'ffn_conv_b': ffn_conv_b[l],
              'w_ff_out': w_ff_out[l], 'ln2_g': ln2_g[l], 'ln2_b': ln2_b[l]}
        xp, new_p = trunk_layer(xp, lp, None, True)
        past = (cache_k[l], cache_v[l], state_ssm_re[l], state_ssm_im[l], cache_sconv[l], cache_ffn_conv[l])
        xs, new_s = trunk_layer(xs, lp, past, False)
        for i in range(6):
            st_p[i].append(new_p[i])
            st_s[i].append(new_s[i])
    k_p, v_p, hre_p, him_p, sc_p, ff_p = [jnp.stack(a, axis=0) for a in st_p]
    k_s, v_s, hre_s, him_s, sc_s, ff_s = [jnp.stack(a, axis=0) for a in st_s]
    return (xp, xs, k_p, v_p, k_s, v_s, hre_p, him_p, hre_s, him_s, sc_p, sc_s, ff_p, ff_s)
```

```python
import functools
import math

import jax
import jax.numpy as jnp
from jax import lax
from jax.experimental import pallas as pl
from jax.experimental.pallas import tpu as pltpu

D_MODEL = 1024
DEPTH = 4
CHUNK = 64
N_HEADS = 8
HEAD_DIM = 64
D_ATT = N_HEADS * HEAD_DIM
N_PAST_CHUNKS = 8
PAST = N_PAST_CHUNKS * CHUNK
REL_MAX = 128
SSM_GROUP = 16
N_SSM_GROUPS = 16
D_SSM = SSM_GROUP * N_SSM_GROUPS
SSM_STATE = 64
D_STATE = N_SSM_GROUPS * SSM_STATE
D_SCONV = 256
CONV_K = 3
D_FF = 2048
D_QKV = 3 * D_ATT
D_REST = D_SSM + 3 * D_SCONV
ALPHA = (2 * DEPTH) ** 0.25
LN_EPS = 1e-5

SUBLANES = 8
LANES = 128
VMEM_LIMIT = 56 * 1024 * 1024

F32 = jnp.float32
BF16 = jnp.bfloat16


def _cparams(sem):
    return pltpu.CompilerParams(dimension_semantics=sem, vmem_limit_bytes=VMEM_LIMIT)


def _layer_norm(x, g, b):
    mu = jnp.mean(x, axis=-1, keepdims=True)
    xc = x - mu
    var = jnp.mean(xc * xc, axis=-1, keepdims=True)
    return xc * lax.rsqrt(var + LN_EPS) * g + b


def _dot(a, b):
    return jnp.dot(a, b, preferred_element_type=F32)


def _causal_conv3(ext_ref, cur, w_ref, b_ref, tm):
    ext_ref[pl.ds(SUBLANES, tm), :] = cur
    m1 = ext_ref[pl.ds(SUBLANES - 1, tm), :]
    m2 = ext_ref[pl.ds(SUBLANES - 2, tm), :]
    return w_ref[0:1, :] * m2 + w_ref[1:2, :] * m1 + w_ref[2:3, :] * cur + b_ref[...]


def _ln_kernel(x_ref, g_ref, b_ref, o_ref):
    o_ref[...] = _layer_norm(x_ref[...], g_ref[...], b_ref[...])


def _ln_call(x2d, g, b, tm):
    rows = x2d.shape[0]
    return pl.pallas_call(
        _ln_kernel,
        grid=(rows // tm,),
        in_specs=[pl.BlockSpec((tm, D_MODEL), lambda i: (i, 0)),
                  pl.BlockSpec((1, D_MODEL), lambda i: (0, 0)),
                  pl.BlockSpec((1, D_MODEL), lambda i: (0, 0))],
        out_specs=pl.BlockSpec((tm, D_MODEL), lambda i: (i, 0)),
        out_shape=jax.ShapeDtypeStruct((rows, D_MODEL), F32),
        compiler_params=_cparams(("parallel",)),
        name="ln_in",
    )(x2d, g, b)


def _inproj_kernel(x_ref, w_ref, scw_ref, scb_ref, sc0_ref,
                   qkv_ref, kt_ref, vt_ref, u_ref, sc_ref, scn_ref,
                   ext_ref, *, tm, tail_first):
    ti = pl.program_id(1)

    @pl.when(ti == 0)
    def _():
        ext_ref[0:SUBLANES, :] = sc0_ref[...]

    xb = x_ref[...].astype(BF16)
    qkv = _dot(xb, w_ref[:, 0:D_QKV])
    qkv_ref[:, 0:D_ATT] = (qkv[:, 0:D_ATT] * (HEAD_DIM ** -0.5)).astype(BF16)
    qkv_ref[:, D_ATT:D_QKV] = qkv[:, D_ATT:D_QKV].astype(BF16)

    @pl.when(ti >= tail_first)
    def _():
        kt_ref[...] = qkv[:, D_ATT:2 * D_ATT]
        vt_ref[...] = qkv[:, 2 * D_ATT:3 * D_ATT]

    rest = _dot(xb, w_ref[:, D_QKV:D_QKV + D_REST])
    u_ref[...] = rest[:, 0:D_SSM]
    gate_b = rest[:, D_SSM:D_SSM + D_SCONV]
    gate_c = rest[:, D_SSM + D_SCONV:D_SSM + 2 * D_SCONV]
    xv = rest[:, D_SSM + 2 * D_SCONV:D_SSM + 3 * D_SCONV]
    conv = _causal_conv3(ext_ref, gate_c * xv, scw_ref, scb_ref, tm)
    sc_ref[...] = (gate_b * conv).astype(BF16)
    last = ext_ref[pl.ds(tm, SUBLANES), :]
    ext_ref[0:SUBLANES, :] = last
    scn_ref[...] = last


def _inproj_call(x, w_in, scw, scb, sc0, tm, keep):
    bsz, s, _ = x.shape
    nt = s // tm
    ntail = keep // tm
    tail_first = nt - ntail
    row = lambda b, t: (b, t, 0)
    const2 = lambda b, t: (0, 0)
    tail = lambda b, t: (b, jnp.maximum(t - tail_first, 0), 0)
    bonly = lambda b, t: (b, 0, 0)
    return pl.pallas_call(
        functools.partial(_inproj_kernel, tm=tm, tail_first=tail_first),
        grid=(bsz, nt),
        in_specs=[pl.BlockSpec((None, tm, D_MODEL), row),
                  pl.BlockSpec((D_MODEL, D_QKV + D_REST), const2),
                  pl.BlockSpec((CONV_K, D_SCONV), const2),
                  pl.BlockSpec((1, D_SCONV), const2),
                  pl.BlockSpec((None, SUBLANES, D_SCONV), bonly)],
        out_specs=[pl.BlockSpec((None, tm, D_QKV), row),
                   pl.BlockSpec((None, tm, D_ATT), tail),
                   pl.BlockSpec((None, tm, D_ATT), tail),
                   pl.BlockSpec((None, tm, D_SSM), row),
                   pl.BlockSpec((None, tm, D_SCONV), row),
                   pl.BlockSpec((None, SUBLANES, D_SCONV), bonly)],
        out_shape=[jax.ShapeDtypeStruct((bsz, s, D_QKV), BF16),
                   jax.ShapeDtypeStruct((bsz, keep, D_ATT), F32),
                   jax.ShapeDtypeStruct((bsz, keep, D_ATT), F32),
                   jax.ShapeDtypeStruct((bsz, s, D_SSM), F32),
                   jax.ShapeDtypeStruct((bsz, s, D_SCONV), BF16),
                   jax.ShapeDtypeStruct((bsz, SUBLANES, D_SCONV), F32)],
        scratch_shapes=[pltpu.VMEM((tm + SUBLANES, D_SCONV), F32)],
        compiler_params=_cparams(("parallel", "arbitrary")),
        name="in_proj",
    )(x, w_in, scw, scb, sc0)


def _attn_kernel(qkv_ref, pk_ref, pv_ref, bias_ref, o_ref, kpad_ref, vpad_ref,
                 *, s, tq, mask_past):
    band = PAST + tq
    kpad_ref[0:PAST, :] = pk_ref[...]
    vpad_ref[0:PAST, :] = pv_ref[...]
    kpad_ref[PAST:PAST + s, :] = qkv_ref[:, D_ATT:2 * D_ATT]
    vpad_ref[PAST:PAST + s, :] = qkv_ref[:, 2 * D_ATT:3 * D_ATT]

    lane = lax.broadcasted_iota(jnp.int32, (tq, 2 * HEAD_DIM), 1)
    low = lane < HEAD_DIM

    def chunk(c, masked):
        r0 = c * tq if isinstance(c, int) else pl.multiple_of(c * tq, tq)
        if masked:
            col = lax.broadcasted_iota(jnp.int32, (tq, band), 1)
            valid = col >= PAST - c * tq
        for hp in range(N_HEADS // 2):
            cols = slice(hp * 2 * HEAD_DIM, (hp + 1) * 2 * HEAD_DIM)
            qp = qkv_ref[pl.ds(r0, tq), cols]
            kp = kpad_ref[pl.ds(r0, band), cols]
            vp = vpad_ref[pl.ds(r0, band), cols]
            outs = []
            for j in range(2):
                qm = jnp.where(low if j == 0 else jnp.logical_not(low), qp, jnp.zeros_like(qp))
                sc = lax.dot_general(qm, kp, (((1,), (1,)), ((), ())), preferred_element_type=F32)
                sc = sc + bias_ref[2 * hp + j]
                if masked:
                    sc = jnp.where(valid, sc, -jnp.inf)
                m = jnp.max(sc, axis=-1, keepdims=True)
                e = jnp.exp(sc - m)
                l = jnp.sum(e, axis=-1, keepdims=True)
                outs.append(_dot(e.astype(BF16), vp) / l)
            o_ref[pl.ds(r0, tq), cols] = jnp.where(low, outs[0], outs[1]).astype(BF16)

    nc = s // tq
    if mask_past:
        nm = min(N_PAST_CHUNKS, nc)
        lax.fori_loop(0, nm, lambda c, _: (chunk(c, True), 0)[1], 0)
        if nc > nm:
            lax.fori_loop(nm, nc, lambda c, _: (chunk(c, False), 0)[1], 0)
    else:
        for c in range(nc):
            chunk(c, False)


def _attn_call(qkv, pk, pv, bias, tq, mask_past):
    bsz, s, _ = qkv.shape
    band = PAST + tq
    bsel = lambda b: (b, 0, 0)
    return pl.pallas_call(
        functools.partial(_attn_kernel, s=s, tq=tq, mask_past=mask_past),
        grid=(bsz,),
        in_specs=[pl.BlockSpec((None, s, D_QKV), bsel),
                  pl.BlockSpec((None, PAST, D_ATT), bsel),
                  pl.BlockSpec((None, PAST, D_ATT), bsel),
                  pl.BlockSpec((N_HEADS, tq, band), lambda b: (0, 0, 0))],
        out_specs=pl.BlockSpec((None, s, D_ATT), bsel),
        out_shape=jax.ShapeDtypeStruct((bsz, s, D_ATT), BF16),
        scratch_shapes=[pltpu.VMEM((PAST + s, D_ATT), BF16),
                        pltpu.VMEM((PAST + s, D_ATT), BF16)],
        compiler_params=_cparams(("parallel",)),
        name="attention",
    )(qkv, pk, pv, bias)


def _gelu_tanh(x):
    c = math.sqrt(2.0 / math.pi)
    return 0.5 * x * (1.0 + jnp.tanh(c * (x + 0.044715 * (x * x * x))))


def _ssm_kernel(u_ref, h0_ref, lbar_ref, bblk_ref, cblk_ref, d_ref, wglu_ref,
                o_ref, hn_ref, st_ref, h_ref, *, tt):
    ti = pl.program_id(1)
    rows = tt * SUBLANES

    @pl.when(ti == 0)
    def _():
        h_ref[...] = h0_ref[...]

    ut = jnp.swapaxes(u_ref[...], 0, 1).reshape(rows, D_SSM)
    st_ref[...] = _dot(ut.astype(BF16), bblk_ref[...])

    lr = jnp.broadcast_to(lbar_ref[0:1, :], (SUBLANES, D_STATE))
    li = jnp.broadcast_to(lbar_ref[1:2, :], (SUBLANES, D_STATE))

    def step(t, carry):
        hr, hi = carry
        r = pl.ds(pl.multiple_of(t * SUBLANES, SUBLANES), SUBLANES)
        nr = lr * hr - li * hi + st_ref[r, 0:D_STATE]
        ni = lr * hi + li * hr + st_ref[r, D_STATE:2 * D_STATE]
        st_ref[r, 0:D_STATE] = nr
        st_ref[r, D_STATE:2 * D_STATE] = ni
        return nr, ni

    hr, hi = lax.fori_loop(0, tt, step, (h_ref[:, 0:D_STATE], h_ref[:, D_STATE:2 * D_STATE]))
    h_ref[:, 0:D_STATE] = hr
    h_ref[:, D_STATE:2 * D_STATE] = hi
    hn_ref[...] = h_ref[...]

    y = _dot(st_ref[...].astype(BF16), cblk_ref[...]) + d_ref[...] * ut
    z = _gelu_tanh(y)
    gate = jax.nn.sigmoid(_dot(z.astype(BF16), wglu_ref[...]))
    out = (z * gate).reshape(tt, SUBLANES, D_SSM)
    o_ref[...] = jnp.swapaxes(out, 0, 1).astype(BF16)


def _ssm_call(u, h0, lbar, bblk, cblk, d, wglu, tt):
    bsz, s, _ = u.shape
    nb = bsz // SUBLANES
    blk = lambda b, t: (b, t, 0)
    const2 = lambda b, t: (0, 0)
    hsel = lambda b, t: (b, 0)
    return pl.pallas_call(
        functools.partial(_ssm_kernel, tt=tt),
        grid=(nb, s // tt),
        in_specs=[pl.BlockSpec((SUBLANES, tt, D_SSM), blk),
                  pl.BlockSpec((SUBLANES, 2 * D_STATE), hsel),
                  pl.BlockSpec((2, D_STATE), const2),
                  pl.BlockSpec((D_SSM, 2 * D_STATE), const2),
                  pl.BlockSpec((2 * D_STATE, D_SSM), const2),
                  pl.BlockSpec((1, D_SSM), const2),
                  pl.BlockSpec((D_SSM, D_SSM), const2)],
        out_specs=[pl.BlockSpec((SUBLANES, tt, D_SSM), blk),
                   pl.BlockSpec((SUBLANES, 2 * D_STATE), hsel)],
        out_shape=[jax.ShapeDtypeStruct((bsz, s, D_SSM), BF16),
                   jax.ShapeDtypeStruct((bsz, 2 * D_STATE), F32)],
        scratch_shapes=[pltpu.VMEM((tt * SUBLANES, 2 * D_STATE), F32),
                        pltpu.VMEM((SUBLANES, 2 * D_STATE), F32)],
        compiler_params=_cparams(("parallel", "arbitrary")),
        name="ssm",
    )(u, h0, lbar, bblk, cblk, d, wglu)


def _outproj_kernel(att_ref, ssm_ref, sc_ref, x_ref, w_ref, g_ref, b_ref, o_ref):
    mix = _dot(att_ref[...], w_ref[0:D_ATT, :])
    mix = mix + _dot(ssm_ref[...], w_ref[D_ATT:D_ATT + D_SSM, :])
    mix = mix + _dot(sc_ref[...], w_ref[D_ATT + D_SSM:D_MODEL, :])
    o_ref[...] = _layer_norm(ALPHA * x_ref[...] + mix, g_ref[...], b_ref[...])


def _outproj_call(att, ssm, sc, x, w_out, g, b, tm):
    rows = x.shape[0]
    row = lambda i: (i, 0)
    const = lambda i: (0, 0)
    return pl.pallas_call(
        _outproj_kernel,
        grid=(rows // tm,),
        in_specs=[pl.BlockSpec((tm, D_ATT), row),
                  pl.BlockSpec((tm, D_SSM), row),
                  pl.BlockSpec((tm, D_SCONV), row),
                  pl.BlockSpec((tm, D_MODEL), row),
                  pl.BlockSpec((D_MODEL, D_MODEL), const),
                  pl.BlockSpec((1, D_MODEL), const),
                  pl.BlockSpec((1, D_MODEL), const)],
        out_specs=pl.BlockSpec((tm, D_MODEL), row),
        out_shape=jax.ShapeDtypeStruct((rows, D_MODEL), F32),
        compiler_params=_cparams(("parallel",)),
        name="out_proj",
    )(att, ssm, sc, x, w_out, g, b)


FF_CHUNK = 512


def _ffn_kernel(x_ref, w1_ref, cw_ref, cb_ref, c0_ref, w2_ref, g_ref, b_ref,
                o_ref, cn_ref, ext_ref, carry_ref, *, tm):
    ti = pl.program_id(1)

    @pl.when(ti == 0)
    def _():
        carry_ref[...] = c0_ref[...]

    x = x_ref[...]
    xb = x.astype(BF16)
    acc = jnp.zeros((tm, D_MODEL), F32)
    for c in range(D_FF // FF_CHUNK):
        halves = []
        for base in (0, D_FF):
            cols = slice(base + c * FF_CHUNK, base + (c + 1) * FF_CHUNK)
            ext_ref[0:SUBLANES, :] = carry_ref[:, cols]
            up = _dot(xb, w1_ref[:, cols])
            halves.append(_causal_conv3(ext_ref, up, cw_ref.at[:, cols], cb_ref.at[:, cols], tm))
            carry_ref[:, cols] = ext_ref[pl.ds(tm, SUBLANES), :]
        gate, val = halves
        h = (gate * jax.nn.sigmoid(gate) * val).astype(BF16)
        acc = acc + _dot(h, w2_ref[c * FF_CHUNK:(c + 1) * FF_CHUNK, :])
    cn_ref[...] = carry_ref[...]
    o_ref[...] = _layer_norm(ALPHA * x + acc, g_ref[...], b_ref[...])


def _ffn_call(x, w1, cw, cb, c0, w2, g, b, tm):
    bsz, s, _ = x.shape
    row = lambda bi, t: (bi, t, 0)
    const2 = lambda bi, t: (0, 0)
    bonly = lambda bi, t: (bi, 0, 0)
    return pl.pallas_call(
        functools.partial(_ffn_kernel, tm=tm),
        grid=(bsz, s // tm),
        in_specs=[pl.BlockSpec((None, tm, D_MODEL), row),
                  pl.BlockSpec((D_MODEL, 2 * D_FF), const2),
                  pl.BlockSpec((CONV_K, 2 * D_FF), const2),
                  pl.BlockSpec((1, 2 * D_FF), const2),
                  pl.BlockSpec((None, SUBLANES, 2 * D_FF), bonly),
                  pl.BlockSpec((D_FF, D_MODEL), const2),
                  pl.BlockSpec((1, D_MODEL), const2),
                  pl.BlockSpec((1, D_MODEL), const2)],
        out_specs=[pl.BlockSpec((None, tm, D_MODEL), row),
                   pl.BlockSpec((None, SUBLANES, 2 * D_FF), bonly)],
        out_shape=[jax.ShapeDtypeStruct((bsz, s, D_MODEL), F32),
                   jax.ShapeDtypeStruct((bsz, SUBLANES, 2 * D_FF), F32)],
        scratch_shapes=[pltpu.VMEM((tm + SUBLANES, FF_CHUNK), F32),
                        pltpu.VMEM((SUBLANES, 2 * D_FF), F32)],
        compiler_params=_cparams(("parallel", "arbitrary")),
        name="conv_ffn",
    )(x, w1, cw, cb, c0, w2, g, b)


def _rel_bias_table(rel_bias, tq):
    t_idx = jnp.arange(tq)[:, None]
    s_idx = jnp.arange(PAST + tq)[None, :]
    idx = jnp.clip(t_idx + PAST - s_idx, -(CHUNK - 1), REL_MAX) + (CHUNK - 1)
    return rel_bias[:, idx].astype(F32)


def _ssm_params(lam_re, lam_im, log_dt, b_re, b_im, c_re, c_im):
    lam = lax.complex(lam_re.astype(F32), lam_im.astype(F32))
    dt = jnp.exp(log_dt.astype(F32))[:, None]
    lbar = jnp.exp(lam * dt)
    bbar = ((lbar - 1.0) / lam)[:, :, None] * lax.complex(b_re.astype(F32), b_im.astype(F32))
    eye = jnp.eye(N_SSM_GROUPS, dtype=F32)

    def in_map(m):
        return jnp.einsum('gpc,gh->gchp', m, eye).reshape(D_SSM, D_STATE)

    def out_map(m):
        return jnp.einsum('gcp,gh->gphc', m, eye).reshape(D_STATE, D_SSM)

    bblk = jnp.concatenate([in_map(bbar.real), in_map(bbar.imag)], axis=1).astype(BF16)
    cblk = jnp.concatenate([out_map(c_re.astype(F32)), out_map(-c_im.astype(F32))], axis=0).astype(BF16)
    lbar2 = jnp.stack([lbar.real.reshape(D_STATE), lbar.imag.reshape(D_STATE)], axis=0)
    return lbar2, bblk, cblk


def _pad_rows(buf):
    return jnp.pad(buf.astype(F32), ((0, 0), (SUBLANES - (CONV_K - 1), 0), (0, 0)))


def _trunk_layer(x, lp, past, tq, tm, tt):
    bsz, s, _ = x.shape
    if past is None:
        keep = min(PAST, s)
        pk = jnp.zeros((bsz, PAST, D_ATT), BF16)
        pv = jnp.zeros((bsz, PAST, D_ATT), BF16)
        h0 = jnp.zeros((bsz, 2 * D_STATE), F32)
        sc0 = jnp.zeros((bsz, SUBLANES, D_SCONV), F32)
        ff0 = jnp.zeros((bsz, SUBLANES, 2 * D_FF), F32)
    else:
        k_past, v_past, h_re0, h_im0, sconv_buf, ffn_buf = past
        keep = s
        pk = k_past.reshape(bsz, PAST, D_ATT).astype(BF16)
        pv = v_past.reshape(bsz, PAST, D_ATT).astype(BF16)
        h0 = jnp.concatenate([h_re0.reshape(bsz, D_STATE), h_im0.reshape(bsz, D_STATE)], axis=1).astype(F32)
        sc0 = _pad_rows(sconv_buf)
        ff0 = _pad_rows(ffn_buf)

    qkv, k_new, v_new, u, sconv_out, sc_new = _inproj_call(
        x, lp['w_in'], lp['sconv_w'], lp['sconv_b'], sc0, tm, keep)
    att = _attn_call(qkv, pk, pv, _rel_bias_table(lp['rel_bias'], tq), tq, past is None)
    ssm_out, h_new = _ssm_call(u, h0, lp['lbar'], lp['bblk'], lp['cblk'], lp['d'], lp['w_glu'], tt)
    rows = bsz * s
    x1 = _outproj_call(att.reshape(rows, D_ATT), ssm_out.reshape(rows, D_SSM),
                       sconv_out.reshape(rows, D_SCONV), x.reshape(rows, D_MODEL),
                       lp['w_out'], lp['ln1_g'], lp['ln1_b'], tm)
    x2, ff_new = _ffn_call(x1.reshape(bsz, s, D_MODEL), lp['w_ff_in'], lp['ffn_conv_w'], lp['ffn_conv_b'],
                           ff0, lp['w_ff_out'], lp['ln2_g'], lp['ln2_b'], tm)
    new = (k_new.reshape(bsz, keep, N_HEADS, HEAD_DIM), v_new.reshape(bsz, keep, N_HEADS, HEAD_DIM),
           h_new[:, 0:D_STATE].reshape(bsz, N_SSM_GROUPS, SSM_STATE),
           h_new[:, D_STATE:].reshape(bsz, N_SSM_GROUPS, SSM_STATE),
           sc_new[:, SUBLANES - (CONV_K - 1):], ff_new[:, SUBLANES - (CONV_K - 1):])
    return x2, new


def _tiles(s):
    tm = min(512, s)
    tt = min(128, s)
    tq = min(CHUNK, s)
    return tq, tm, tt


def kernel(x_prompt, x_sample, cache_k, cache_v, state_ssm_re, state_ssm_im, cache_sconv, cache_ffn_conv, ln_in_g, ln_in_b, w_in, rel_bias, ssm_lam_re, ssm_lam_im, ssm_log_dt, ssm_b_re, ssm_b_im, ssm_c_re, ssm_c_im, ssm_d, w_glu, sconv_w, sconv_b, w_out, ln1_g, ln1_b, w_ff_in, ffn_conv_w, ffn_conv_b, w_ff_out, ln2_g, ln2_b):
    bp, sp, _ = x_prompt.shape
    bs, ss, _ = x_sample.shape
    g_in = ln_in_g.reshape(1, D_MODEL)
    b_in = ln_in_b.reshape(1, D_MODEL)
    xp = _ln_call(x_prompt.reshape(bp * sp, D_MODEL), g_in, b_in, min(512, bp * sp)).reshape(bp, sp, D_MODEL)
    xs = _ln_call(x_sample.reshape(bs * ss, D_MODEL), g_in, b_in, min(512, bs * ss)).reshape(bs, ss, D_MODEL)
    st_p = [[] for _ in range(6)]
    st_s = [[] for _ in range(6)]
    for l in range(DEPTH):
        lbar, bblk, cblk = _ssm_params(ssm_lam_re[l], ssm_lam_im[l], ssm_log_dt[l], ssm_b_re[l], ssm_b_im[l],
                                       ssm_c_re[l], ssm_c_im[l])
        lp = {'w_in': w_in[l].astype(BF16), 'rel_bias': rel_bias[l],
              'lbar': lbar, 'bblk': bblk, 'cblk': cblk,
              'd': ssm_d[l].reshape(1, D_SSM), 'w_glu': w_glu[l].astype(BF16),
              'sconv_w': sconv_w[l], 'sconv_b': sconv_b[l].reshape(1, D_SCONV),
              'w_out': w_out[l].astype(BF16),
              'ln1_g': ln1_g[l].reshape(1, D_MODEL), 'ln1_b': ln1_b[l].reshape(1, D_MODEL),
              'w_ff_in': w_ff_in[l].astype(BF16), 'ffn_conv_w': ffn_conv_w[l],
              'ffn_conv_b': ffn_conv_b[l].reshape(1, 2 * D_FF),
              'w_ff_out': w_ff_out[l].astype(BF16),
              'ln2_g': ln2_g[l].reshape(1, D_MODEL), 'ln2_b': ln2_b[l].reshape(1, D_MODEL)}
        xp, new_p = _trunk_layer(xp, lp, None, *_tiles(sp))
        past = (cache_k[l], cache_v[l], state_ssm_re[l], state_ssm_im[l], cache_sconv[l], cache_ffn_conv[l])
        xs, new_s = _trunk_layer(xs, lp, past, *_tiles(ss))
        for i in range(6):
            st_p[i].append(new_p[i])
            st_s[i].append(new_s[i])
    k_p, v_p, hre_p, him_p, sc_p, ff_p = [jnp.stack(a, axis=0) for a in st_p]
    k_s, v_s, hre_s, him_s, sc_s, ff_s = [jnp.stack(a, axis=0) for a in st_s]
    return (xp, xs, k_p, v_p, k_s, v_s, hre_p, him_p, hre_s, him_s, sc_p, sc_s, ff_p, ff_s)
```

```python
import functools
import math

import jax
import jax.numpy as jnp
from jax import lax
from jax.experimental import pallas as pl
from jax.experimental.pallas import tpu as pltpu

D_MODEL = 1024
DEPTH = 4
CHUNK = 64
N_HEADS = 8
HEAD_DIM = 64
D_ATT = N_HEADS * HEAD_DIM
N_PAST_CHUNKS = 8
PAST = N_PAST_CHUNKS * CHUNK
REL_MAX = 128
SSM_GROUP = 16
N_SSM_GROUPS = 16
D_SSM = SSM_GROUP * N_SSM_GROUPS
SSM_STATE = 64
D_STATE = N_SSM_GROUPS * SSM_STATE
D_SCONV = 256
CONV_K = 3
D_FF = 2048
D_QKV = 3 * D_ATT
D_REST = D_SSM + 3 * D_SCONV
ALPHA = (2 * DEPTH) ** 0.25
LN_EPS = 1e-5

SUBLANES = 8
LANES = 128
VMEM_LIMIT = 56 * 1024 * 1024

F32 = jnp.float32
BF16 = jnp.bfloat16


def _cparams(sem):
    return pltpu.CompilerParams(dimension_semantics=sem, vmem_limit_bytes=VMEM_LIMIT)


def _layer_norm(x, g, b):
    mu = jnp.mean(x, axis=-1, keepdims=True)
    xc = x - mu
    var = jnp.mean(xc * xc, axis=-1, keepdims=True)
    return xc * lax.rsqrt(var + LN_EPS) * g + b


def _dot(a, b):
    return jnp.dot(a, b, preferred_element_type=F32)


def _causal_conv3(ext_ref, cur, w_ref, b_ref, tm):
    ext_ref[pl.ds(SUBLANES, tm), :] = cur
    m1 = ext_ref[pl.ds(SUBLANES - 1, tm), :]
    m2 = ext_ref[pl.ds(SUBLANES - 2, tm), :]
    return w_ref[0:1, :] * m2 + w_ref[1:2, :] * m1 + w_ref[2:3, :] * cur + b_ref[...]


def _ln_kernel(x_ref, g_ref, b_ref, o_ref):
    o_ref[...] = _layer_norm(x_ref[...], g_ref[...], b_ref[...])


def _ln_call(x2d, g, b, tm):
    rows = x2d.shape[0]
    return pl.pallas_call(
        _ln_kernel,
        grid=(rows // tm,),
        in_specs=[pl.BlockSpec((tm, D_MODEL), lambda i: (i, 0)),
                  pl.BlockSpec((1, D_MODEL), lambda i: (0, 0)),
                  pl.BlockSpec((1, D_MODEL), lambda i: (0, 0))],
        out_specs=pl.BlockSpec((tm, D_MODEL), lambda i: (i, 0)),
        out_shape=jax.ShapeDtypeStruct((rows, D_MODEL), F32),
        compiler_params=_cparams(("parallel",)),
        name="ln_in",
    )(x2d, g, b)


def _inproj_kernel(x_ref, w_ref, scw_ref, scb_ref, sc0_ref,
                   qkv_ref, kt_ref, vt_ref, u_ref, sc_ref, scn_ref,
                   ext_ref, *, tm, tail_first):
    ti = pl.program_id(1)

    @pl.when(ti == 0)
    def _():
        ext_ref[0:SUBLANES, :] = sc0_ref[...]

    xb = x_ref[...].astype(BF16)
    qkv = _dot(xb, w_ref[:, 0:D_QKV])
    qkv_ref[:, 0:D_ATT] = (qkv[:, 0:D_ATT] * (HEAD_DIM ** -0.5)).astype(BF16)
    qkv_ref[:, D_ATT:D_QKV] = qkv[:, D_ATT:D_QKV].astype(BF16)

    @pl.when(ti >= tail_first)
    def _():
        kt_ref[...] = qkv[:, D_ATT:2 * D_ATT]
        vt_ref[...] = qkv[:, 2 * D_ATT:3 * D_ATT]

    rest = _dot(xb, w_ref[:, D_QKV:D_QKV + D_REST])
    u_ref[...] = rest[:, 0:D_SSM]
    gate_b = rest[:, D_SSM:D_SSM + D_SCONV]
    gate_c = rest[:, D_SSM + D_SCONV:D_SSM + 2 * D_SCONV]
    xv = rest[:, D_SSM + 2 * D_SCONV:D_SSM + 3 * D_SCONV]
    conv = _causal_conv3(ext_ref, gate_c * xv, scw_ref, scb_ref, tm)
    sc_ref[...] = (gate_b * conv).astype(BF16)
    last = ext_ref[pl.ds(tm, SUBLANES), :]
    ext_ref[0:SUBLANES, :] = last
    scn_ref[...] = last


def _inproj_call(x, w_in, scw, scb, sc0, tm, keep):
    bsz, s, _ = x.shape
    nt = s // tm
    ntail = keep // tm
    tail_first = nt - ntail
    row = lambda b, t: (b, t, 0)
    const2 = lambda b, t: (0, 0)
    tail = lambda b, t: (b, jnp.maximum(t - tail_first, 0), 0)
    bonly = lambda b, t: (b, 0, 0)
    return pl.pallas_call(
        functools.partial(_inproj_kernel, tm=tm, tail_first=tail_first),
        grid=(bsz, nt),
        in_specs=[pl.BlockSpec((None, tm, D_MODEL), row),
                  pl.BlockSpec((D_MODEL, D_QKV + D_REST), const2),
                  pl.BlockSpec((CONV_K, D_SCONV), const2),
                  pl.BlockSpec((1, D_SCONV), const2),
                  pl.BlockSpec((None, SUBLANES, D_SCONV), bonly)],
        out_specs=[pl.BlockSpec((None, tm, D_QKV), row),
                   pl.BlockSpec((None, tm, D_ATT), tail),
                   pl.BlockSpec((None, tm, D_ATT), tail),
                   pl.BlockSpec((None, tm, D_SSM), row),
                   pl.BlockSpec((None, tm, D_SCONV), row),
                   pl.BlockSpec((None, SUBLANES, D_SCONV), bonly)],
        out_shape=[jax.ShapeDtypeStruct((bsz, s, D_QKV), BF16),
                   jax.ShapeDtypeStruct((bsz, keep, D_ATT), F32),
                   jax.ShapeDtypeStruct((bsz, keep, D_ATT), F32),
                   jax.ShapeDtypeStruct((bsz, s, D_SSM), F32),
                   jax.ShapeDtypeStruct((bsz, s, D_SCONV), BF16),
                   jax.ShapeDtypeStruct((bsz, SUBLANES, D_SCONV), F32)],
        scratch_shapes=[pltpu.VMEM((tm + SUBLANES, D_SCONV), F32)],
        compiler_params=_cparams(("parallel", "arbitrary")),
        name="in_proj",
    )(x, w_in, scw, scb, sc0)


HEADS_PER_GROUP = 4
GROUP_W = HEADS_PER_GROUP * HEAD_DIM
N_GROUPS = N_HEADS // HEADS_PER_GROUP
Q_STEP = 4 * CHUNK
K_WIN = PAST + Q_STEP


def _softmax_unnormalised(sc):
    m = jnp.max(sc, axis=-1, keepdims=True)
    e = jnp.exp(sc - m)
    return e.astype(BF16), jnp.sum(e, axis=-1, keepdims=True)


def _group_head_of_lane(shape):
    lane = lax.broadcasted_iota(jnp.int32, shape, len(shape) - 1)
    return (lane // HEAD_DIM) % HEADS_PER_GROUP


def _attn_group(q4, k4, v_of_head, bias_of_head):
    head = _group_head_of_lane(q4.shape)
    acc = None
    scale = None
    for h in range(HEADS_PER_GROUP):
        qm = jnp.where(head == h, q4, jnp.zeros_like(q4))
        sc = lax.dot_general(qm, k4, (((1,), (1,)), ((), ())), preferred_element_type=F32)
        e, l = _softmax_unnormalised(sc + bias_of_head(h))
        pv = _dot(e, v_of_head(h))
        acc = pv if acc is None else acc + pv
        inv = 1.0 / l
        scale = inv if scale is None else jnp.where(head == h, inv, scale)
    return acc * scale


def _attn_prompt_kernel(qkv_ref, bias_ref, o_ref, vmask_ref, *, s):
    v = qkv_ref[:, 2 * D_ATT:3 * D_ATT]
    head = _group_head_of_lane(v.shape)
    for h in range(HEADS_PER_GROUP):
        vmask_ref[h] = jnp.where(head == h, v, jnp.zeros_like(v))

    def step(r0, k0, nk):
        for g in range(N_GROUPS):
            cols = slice(g * GROUP_W, (g + 1) * GROUP_W)
            kcols = slice(D_ATT + g * GROUP_W, D_ATT + (g + 1) * GROUP_W)
            out = _attn_group(
                qkv_ref[pl.ds(r0, Q_STEP), cols],
                qkv_ref[pl.ds(k0, nk), kcols],
                lambda h: vmask_ref[h, pl.ds(k0, nk), cols],
                lambda h: bias_ref[g * HEADS_PER_GROUP + h, :, K_WIN - nk:K_WIN])
            o_ref[pl.ds(r0, Q_STEP), cols] = out.astype(BF16)

    n_steps = s // Q_STEP
    n_head_steps = min(PAST // Q_STEP, n_steps)
    for j in range(n_head_steps):
        step(j * Q_STEP, 0, (j + 1) * Q_STEP)

    def body(j, carry):
        r0 = pl.multiple_of(j * Q_STEP, Q_STEP)
        step(r0, pl.multiple_of(r0 - PAST, Q_STEP), K_WIN)
        return carry

    if n_steps > n_head_steps:
        lax.fori_loop(n_head_steps, n_steps, body, 0)


def _attn_prompt_call(qkv, bias):
    bsz, s, _ = qkv.shape
    assert s % Q_STEP == 0
    bsel = lambda b: (b, 0, 0)
    return pl.pallas_call(
        functools.partial(_attn_prompt_kernel, s=s),
        grid=(bsz,),
        in_specs=[pl.BlockSpec((None, s, D_QKV), bsel),
                  pl.BlockSpec((N_HEADS, Q_STEP, K_WIN), lambda b: (0, 0, 0))],
        out_specs=pl.BlockSpec((None, s, D_ATT), bsel),
        out_shape=jax.ShapeDtypeStruct((bsz, s, D_ATT), BF16),
        scratch_shapes=[pltpu.VMEM((HEADS_PER_GROUP, s, D_ATT), BF16)],
        compiler_params=_cparams(("parallel",)),
        name="attention_prompt",
    )(qkv, bias)


def _attn_sample_kernel(qkv_ref, pk_ref, pv_ref, bias_ref, o_ref, k_ref, vmask_ref, *, s):
    k_ref[0:PAST, :] = pk_ref[...]
    k_ref[PAST:PAST + s, :] = qkv_ref[:, D_ATT:2 * D_ATT]
    head = _group_head_of_lane((PAST, D_ATT))
    head_new = _group_head_of_lane((s, D_ATT))
    for h in range(HEADS_PER_GROUP):
        vmask_ref[h, 0:PAST, :] = jnp.where(head == h, pv_ref[...], jnp.zeros_like(pv_ref[...]))
        v = qkv_ref[:, 2 * D_ATT:3 * D_ATT]
        vmask_ref[h, PAST:PAST + s, :] = jnp.where(head_new == h, v, jnp.zeros_like(v))
    for g in range(N_GROUPS):
        cols = slice(g * GROUP_W, (g + 1) * GROUP_W)
        out = _attn_group(qkv_ref[:, cols], k_ref[:, cols],
                          lambda h: vmask_ref[h, :, cols],
                          lambda h: bias_ref[g * HEADS_PER_GROUP + h])
        o_ref[:, cols] = out.astype(BF16)


def _attn_sample_call(qkv, pk, pv, bias):
    bsz, s, _ = qkv.shape
    band = PAST + s
    bsel = lambda b: (b, 0, 0)
    return pl.pallas_call(
        functools.partial(_attn_sample_kernel, s=s),
        grid=(bsz,),
        in_specs=[pl.BlockSpec((None, s, D_QKV), bsel),
                  pl.BlockSpec((None, PAST, D_ATT), bsel),
                  pl.BlockSpec((None, PAST, D_ATT), bsel),
                  pl.BlockSpec((N_HEADS, s, band), lambda b: (0, 0, 0))],
        out_specs=pl.BlockSpec((None, s, D_ATT), bsel),
        out_shape=jax.ShapeDtypeStruct((bsz, s, D_ATT), BF16),
        scratch_shapes=[pltpu.VMEM((band, D_ATT), BF16),
                        pltpu.VMEM((HEADS_PER_GROUP, band, D_ATT), BF16)],
        compiler_params=_cparams(("parallel",)),
        name="attention_sample",
    )(qkv, pk, pv, bias)


def _gelu_tanh(x):
    c = math.sqrt(2.0 / math.pi)
    return 0.5 * x * (1.0 + jnp.tanh(c * (x + 0.044715 * (x * x * x))))


def _ssm_kernel(u_ref, h0_ref, lbar_ref, bblk_ref, cblk_ref, d_ref, wglu_ref,
                o_ref, hn_ref, st_ref, h_ref, *, tt):
    ti = pl.program_id(1)
    rows = tt * SUBLANES

    @pl.when(ti == 0)
    def _():
        h_ref[...] = h0_ref[...]

    ut = jnp.swapaxes(u_ref[...], 0, 1).reshape(rows, D_SSM)
    st_ref[...] = _dot(ut.astype(BF16), bblk_ref[...])

    lr = jnp.broadcast_to(lbar_ref[0:1, :], (SUBLANES, D_STATE))
    li = jnp.broadcast_to(lbar_ref[1:2, :], (SUBLANES, D_STATE))

    def step(t, carry):
        hr, hi = carry
        r = pl.ds(pl.multiple_of(t * SUBLANES, SUBLANES), SUBLANES)
        nr = lr * hr - li * hi + st_ref[r, 0:D_STATE]
        ni = lr * hi + li * hr + st_ref[r, D_STATE:2 * D_STATE]
        st_ref[r, 0:D_STATE] = nr
        st_ref[r, D_STATE:2 * D_STATE] = ni
        return nr, ni

    hr, hi = lax.fori_loop(0, tt, step, (h_ref[:, 0:D_STATE], h_ref[:, D_STATE:2 * D_STATE]))
    h_ref[:, 0:D_STATE] = hr
    h_ref[:, D_STATE:2 * D_STATE] = hi
    hn_ref[...] = h_ref[...]

    y = _dot(st_ref[...].astype(BF16), cblk_ref[...]) + d_ref[...] * ut
    z = _gelu_tanh(y)
    gate = jax.nn.sigmoid(_dot(z.astype(BF16), wglu_ref[...]))
    out = (z * gate).reshape(tt, SUBLANES, D_SSM)
    o_ref[...] = jnp.swapaxes(out, 0, 1).astype(BF16)


def _ssm_call(u, h0, lbar, bblk, cblk, d, wglu, tt):
    bsz, s, _ = u.shape
    nb = bsz // SUBLANES
    blk = lambda b, t: (b, t, 0)
    const2 = lambda b, t: (0, 0)
    hsel = lambda b, t: (b, 0)
    return pl.pallas_call(
        functools.partial(_ssm_kernel, tt=tt),
        grid=(nb, s // tt),
        in_specs=[pl.BlockSpec((SUBLANES, tt, D_SSM), blk),
                  pl.BlockSpec((SUBLANES, 2 * D_STATE), hsel),
                  pl.BlockSpec((2, D_STATE), const2),
                  pl.BlockSpec((D_SSM, 2 * D_STATE), const2),
                  pl.BlockSpec((2 * D_STATE, D_SSM), const2),
                  pl.BlockSpec((1, D_SSM), const2),
                  pl.BlockSpec((D_SSM, D_SSM), const2)],
        out_specs=[pl.BlockSpec((SUBLANES, tt, D_SSM), blk),
                   pl.BlockSpec((SUBLANES, 2 * D_STATE), hsel)],
        out_shape=[jax.ShapeDtypeStruct((bsz, s, D_SSM), BF16),
                   jax.ShapeDtypeStruct((bsz, 2 * D_STATE), F32)],
        scratch_shapes=[pltpu.VMEM((tt * SUBLANES, 2 * D_STATE), F32),
                        pltpu.VMEM((SUBLANES, 2 * D_STATE), F32)],
        compiler_params=_cparams(("parallel", "arbitrary")),
        name="ssm",
    )(u, h0, lbar, bblk, cblk, d, wglu)


def _outproj_kernel(att_ref, ssm_ref, sc_ref, x_ref, w_ref, g_ref, b_ref, o_ref):
    mix = _dot(att_ref[...], w_ref[0:D_ATT, :])
    mix = mix + _dot(ssm_ref[...], w_ref[D_ATT:D_ATT + D_SSM, :])
    mix = mix + _dot(sc_ref[...], w_ref[D_ATT + D_SSM:D_MODEL, :])
    o_ref[...] = _layer_norm(ALPHA * x_ref[...] + mix, g_ref[...], b_ref[...])


def _outproj_call(att, ssm, sc, x, w_out, g, b, tm):
    rows = x.shape[0]
    row = lambda i: (i, 0)
    const = lambda i: (0, 0)
    return pl.pallas_call(
        _outproj_kernel,
        grid=(rows // tm,),
        in_specs=[pl.BlockSpec((tm, D_ATT), row),
                  pl.BlockSpec((tm, D_SSM), row),
                  pl.BlockSpec((tm, D_SCONV), row),
                  pl.BlockSpec((tm, D_MODEL), row),
                  pl.BlockSpec((D_MODEL, D_MODEL), const),
                  pl.BlockSpec((1, D_MODEL), const),
                  pl.BlockSpec((1, D_MODEL), const)],
        out_specs=pl.BlockSpec((tm, D_MODEL), row),
        out_shape=jax.ShapeDtypeStruct((rows, D_MODEL), F32),
        compiler_params=_cparams(("parallel",)),
        name="out_proj",
    )(att, ssm, sc, x, w_out, g, b)


FF_CHUNK = 512


def _ffn_kernel(x_ref, w1_ref, cw_ref, cb_ref, c0_ref, w2_ref, g_ref, b_ref,
                o_ref, cn_ref, ext_ref, carry_ref, *, tm):
    ti = pl.program_id(1)

    @pl.when(ti == 0)
    def _():
        carry_ref[...] = c0_ref[...]

    x = x_ref[...]
    xb = x.astype(BF16)
    acc = jnp.zeros((tm, D_MODEL), F32)
    for c in range(D_FF // FF_CHUNK):
        halves = []
        for base in (0, D_FF):
            cols = slice(base + c * FF_CHUNK, base + (c + 1) * FF_CHUNK)
            ext_ref[0:SUBLANES, :] = carry_ref[:, cols]
            up = _dot(xb, w1_ref[:, cols])
            halves.append(_causal_conv3(ext_ref, up, cw_ref.at[:, cols], cb_ref.at[:, cols], tm))
            carry_ref[:, cols] = ext_ref[pl.ds(tm, SUBLANES), :]
        gate, val = halves
        h = (gate * jax.nn.sigmoid(gate) * val).astype(BF16)
        acc = acc + _dot(h, w2_ref[c * FF_CHUNK:(c + 1) * FF_CHUNK, :])
    cn_ref[...] = carry_ref[...]
    o_ref[...] = _layer_norm(ALPHA * x + acc, g_ref[...], b_ref[...])


def _ffn_call(x, w1, cw, cb, c0, w2, g, b, tm):
    bsz, s, _ = x.shape
    row = lambda bi, t: (bi, t, 0)
    const2 = lambda bi, t: (0, 0)
    bonly = lambda bi, t: (bi, 0, 0)
    return pl.pallas_call(
        functools.partial(_ffn_kernel, tm=tm),
        grid=(bsz, s // tm),
        in_specs=[pl.BlockSpec((None, tm, D_MODEL), row),
                  pl.BlockSpec((D_MODEL, 2 * D_FF), const2),
                  pl.BlockSpec((CONV_K, 2 * D_FF), const2),
                  pl.BlockSpec((1, 2 * D_FF), const2),
                  pl.BlockSpec((None, SUBLANES, 2 * D_FF), bonly),
                  pl.BlockSpec((D_FF, D_MODEL), const2),
                  pl.BlockSpec((1, D_MODEL), const2),
                  pl.BlockSpec((1, D_MODEL), const2)],
        out_specs=[pl.BlockSpec((None, tm, D_MODEL), row),
                   pl.BlockSpec((None, SUBLANES, 2 * D_FF), bonly)],
        out_shape=[jax.ShapeDtypeStruct((bsz, s, D_MODEL), F32),
                   jax.ShapeDtypeStruct((bsz, SUBLANES, 2 * D_FF), F32)],
        scratch_shapes=[pltpu.VMEM((tm + SUBLANES, FF_CHUNK), F32),
                        pltpu.VMEM((SUBLANES, 2 * D_FF), F32)],
        compiler_params=_cparams(("parallel", "arbitrary")),
        name="conv_ffn",
    )(x, w1, cw, cb, c0, w2, g, b)


def _rel_bias_table(rel_bias, tq, chunk):
    t_idx = jnp.arange(tq)[:, None]
    s_idx = jnp.arange(PAST + tq)[None, :]
    idx = jnp.clip(t_idx + PAST - s_idx, -(CHUNK - 1), REL_MAX) + (CHUNK - 1)
    bias = rel_bias[:, idx].astype(F32)
    if tq == chunk:
        return bias
    key_chunk = s_idx // chunk - t_idx // chunk
    in_band = (key_chunk >= 0) & (key_chunk <= N_PAST_CHUNKS)
    return jnp.where(in_band[None], bias, -jnp.inf)


def _ssm_params(lam_re, lam_im, log_dt, b_re, b_im, c_re, c_im):
    lam = lax.complex(lam_re.astype(F32), lam_im.astype(F32))
    dt = jnp.exp(log_dt.astype(F32))[:, None]
    lbar = jnp.exp(lam * dt)
    bbar = ((lbar - 1.0) / lam)[:, :, None] * lax.complex(b_re.astype(F32), b_im.astype(F32))
    eye = jnp.eye(N_SSM_GROUPS, dtype=F32)

    def in_map(m):
        return jnp.einsum('gpc,gh->gchp', m, eye).reshape(D_SSM, D_STATE)

    def out_map(m):
        return jnp.einsum('gcp,gh->gphc', m, eye).reshape(D_STATE, D_SSM)

    bblk = jnp.concatenate([in_map(bbar.real), in_map(bbar.imag)], axis=1).astype(BF16)
    cblk = jnp.concatenate([out_map(c_re.astype(F32)), out_map(-c_im.astype(F32))], axis=0).astype(BF16)
    lbar2 = jnp.stack([lbar.real.reshape(D_STATE), lbar.imag.reshape(D_STATE)], axis=0)
    return lbar2, bblk, cblk


def _pad_rows(buf):
    return jnp.pad(buf.astype(F32), ((0, 0), (SUBLANES - (CONV_K - 1), 0), (0, 0)))


def _trunk_layer(x, lp, past, tm, tt):
    bsz, s, _ = x.shape
    if past is None:
        keep = min(PAST, s)
        h0 = jnp.zeros((bsz, 2 * D_STATE), F32)
        sc0 = jnp.zeros((bsz, SUBLANES, D_SCONV), F32)
        ff0 = jnp.zeros((bsz, SUBLANES, 2 * D_FF), F32)
    else:
        k_past, v_past, h_re0, h_im0, sconv_buf, ffn_buf = past
        keep = s
        pk = k_past.reshape(bsz, PAST, D_ATT).astype(BF16)
        pv = v_past.reshape(bsz, PAST, D_ATT).astype(BF16)
        h0 = jnp.concatenate([h_re0.reshape(bsz, D_STATE), h_im0.reshape(bsz, D_STATE)], axis=1).astype(F32)
        sc0 = _pad_rows(sconv_buf)
        ff0 = _pad_rows(ffn_buf)

    qkv, k_new, v_new, u, sconv_out, sc_new = _inproj_call(
        x, lp['w_in'], lp['sconv_w'], lp['sconv_b'], sc0, tm, keep)
    if past is None:
        att = _attn_prompt_call(qkv, _rel_bias_table(lp['rel_bias'], Q_STEP, CHUNK))
    else:
        att = _attn_sample_call(qkv, pk, pv, _rel_bias_table(lp['rel_bias'], s, s))
    ssm_out, h_new = _ssm_call(u, h0, lp['lbar'], lp['bblk'], lp['cblk'], lp['d'], lp['w_glu'], tt)
    rows = bsz * s
    x1 = _outproj_call(att.reshape(rows, D_ATT), ssm_out.reshape(rows, D_SSM),
                       sconv_out.reshape(rows, D_SCONV), x.reshape(rows, D_MODEL),
                       lp['w_out'], lp['ln1_g'], lp['ln1_b'], tm)
    x2, ff_new = _ffn_call(x1.reshape(bsz, s, D_MODEL), lp['w_ff_in'], lp['ffn_conv_w'], lp['ffn_conv_b'],
                           ff0, lp['w_ff_out'], lp['ln2_g'], lp['ln2_b'], tm)
    new = (k_new.reshape(bsz, keep, N_HEADS, HEAD_DIM), v_new.reshape(bsz, keep, N_HEADS, HEAD_DIM),
           h_new[:, 0:D_STATE].reshape(bsz, N_SSM_GROUPS, SSM_STATE),
           h_new[:, D_STATE:].reshape(bsz, N_SSM_GROUPS, SSM_STATE),
           sc_new[:, SUBLANES - (CONV_K - 1):], ff_new[:, SUBLANES - (CONV_K - 1):])
    return x2, new


def _tiles(s):
    tm = min(512, s)
    tt = min(128, s)
    return tm, tt


def kernel(x_prompt, x_sample, cache_k, cache_v, state_ssm_re, state_ssm_im, cache_sconv, cache_ffn_conv, ln_in_g, ln_in_b, w_in, rel_bias, ssm_lam_re, ssm_lam_im, ssm_log_dt, ssm_b_re, ssm_b_im, ssm_c_re, ssm_c_im, ssm_d, w_glu, sconv_w, sconv_b, w_out, ln1_g, ln1_b, w_ff_in, ffn_conv_w, ffn_conv_b, w_ff_out, ln2_g, ln2_b):
    bp, sp, _ = x_prompt.shape
    bs, ss, _ = x_sample.shape
    g_in = ln_in_g.reshape(1, D_MODEL)
    b_in = ln_in_b.reshape(1, D_MODEL)
    xp = _ln_call(x_prompt.reshape(bp * sp, D_MODEL), g_in, b_in, min(512, bp * sp)).reshape(bp, sp, D_MODEL)
    xs = _ln_call(x_sample.reshape(bs * ss, D_MODEL), g_in, b_in, min(512, bs * ss)).reshape(bs, ss, D_MODEL)
    st_p = [[] for _ in range(6)]
    st_s = [[] for _ in range(6)]
    for l in range(DEPTH):
        lbar, bblk, cblk = _ssm_params(ssm_lam_re[l], ssm_lam_im[l], ssm_log_dt[l], ssm_b_re[l], ssm_b_im[l],
                                       ssm_c_re[l], ssm_c_im[l])
        lp = {'w_in': w_in[l].astype(BF16), 'rel_bias': rel_bias[l],
              'lbar': lbar, 'bblk': bblk, 'cblk': cblk,
              'd': ssm_d[l].reshape(1, D_SSM), 'w_glu': w_glu[l].astype(BF16),
              'sconv_w': sconv_w[l], 'sconv_b': sconv_b[l].reshape(1, D_SCONV),
              'w_out': w_out[l].astype(BF16),
              'ln1_g': ln1_g[l].reshape(1, D_MODEL), 'ln1_b': ln1_b[l].reshape(1, D_MODEL),
              'w_ff_in': w_ff_in[l].astype(BF16), 'ffn_conv_w': ffn_conv_w[l],
              'ffn_conv_b': ffn_conv_b[l].reshape(1, 2 * D_FF),
              'w_ff_out': w_ff_out[l].astype(BF16),
              'ln2_g': ln2_g[l].reshape(1, D_MODEL), 'ln2_b': ln2_b[l].reshape(1, D_MODEL)}
        xp, new_p = _trunk_layer(xp, lp, None, *_tiles(sp))
        past = (cache_k[l], cache_v[l], state_ssm_re[l], state_ssm_im[l], cache_sconv[l], cache_ffn_conv[l])
        xs, new_s = _trunk_layer(xs, lp, past, *_tiles(ss))
        for i in range(6):
            st_p[i].append(new_p[i])
            st_s[i].append(new_s[i])
    k_p, v_p, hre_p, him_p, sc_p, ff_p = [jnp.stack(a, axis=0) for a in st_p]
    k_s, v_s, hre_s, him_s, sc_s, ff_s = [jnp.stack(a, axis=0) for a in st_s]
    return (xp, xs, k_p, v_p, k_s, v_s, hre_p, him_p, hre_s, him_s, sc_p, sc_s, ff_p, ff_s)
```

```python
import functools
import math

import jax
import jax.numpy as jnp
import numpy as np
from jax import lax
from jax.experimental import pallas as pl
from jax.experimental.pallas import tpu as pltpu

D_MODEL = 1024
DEPTH = 4
CHUNK = 64
N_HEADS = 8
HEAD_DIM = 64
D_ATT = N_HEADS * HEAD_DIM
N_PAST_CHUNKS = 8
PAST = N_PAST_CHUNKS * CHUNK
REL_MAX = 128
SSM_GROUP = 16
N_SSM_GROUPS = 16
D_SSM = SSM_GROUP * N_SSM_GROUPS
SSM_STATE = 64
D_STATE = N_SSM_GROUPS * SSM_STATE
D_SCONV = 256
CONV_K = 3
D_FF = 2048
D_QKV = 3 * D_ATT
D_REST = D_SSM + 3 * D_SCONV
ALPHA = (2 * DEPTH) ** 0.25
LN_EPS = 1e-5

SUBLANES = 8
LANES = 128
VMEM_LIMIT = 56 * 1024 * 1024

F32 = jnp.float32
BF16 = jnp.bfloat16


def _cparams(sem):
    return pltpu.CompilerParams(dimension_semantics=sem, vmem_limit_bytes=VMEM_LIMIT)


def _layer_norm(x, g, b):
    mu = jnp.mean(x, axis=-1, keepdims=True)
    xc = x - mu
    var = jnp.mean(xc * xc, axis=-1, keepdims=True)
    return xc * lax.rsqrt(var + LN_EPS) * g + b


def _dot(a, b):
    return jnp.dot(a, b, preferred_element_type=F32)


def _causal_conv3(ext_ref, cur, w_ref, b_ref, tm):
    ext_ref[pl.ds(SUBLANES, tm), :] = cur
    m1 = ext_ref[pl.ds(SUBLANES - 1, tm), :]
    m2 = ext_ref[pl.ds(SUBLANES - 2, tm), :]
    return w_ref[0:1, :] * m2 + w_ref[1:2, :] * m1 + w_ref[2:3, :] * cur + b_ref[...]


def _ln_kernel(x_ref, g_ref, b_ref, o_ref):
    o_ref[...] = _layer_norm(x_ref[...], g_ref[...], b_ref[...])


def _ln_call(x2d, g, b, tm):
    rows = x2d.shape[0]
    return pl.pallas_call(
        _ln_kernel,
        grid=(rows // tm,),
        in_specs=[pl.BlockSpec((tm, D_MODEL), lambda i: (i, 0)),
                  pl.BlockSpec((1, D_MODEL), lambda i: (0, 0)),
                  pl.BlockSpec((1, D_MODEL), lambda i: (0, 0))],
        out_specs=pl.BlockSpec((tm, D_MODEL), lambda i: (i, 0)),
        out_shape=jax.ShapeDtypeStruct((rows, D_MODEL), F32),
        compiler_params=_cparams(("parallel",)),
        name="ln_in",
    )(x2d, g, b)


def _inproj_kernel(x_ref, w_ref, scw_ref, scb_ref, sc0_ref,
                   qkv_ref, kt_ref, vt_ref, u_ref, sc_ref, scn_ref,
                   ext_ref, *, tm, tail_first):
    ti = pl.program_id(1)

    @pl.when(ti == 0)
    def _():
        ext_ref[0:SUBLANES, :] = sc0_ref[...]

    xb = x_ref[...].astype(BF16)
    qkv = _dot(xb, w_ref[:, 0:D_QKV])
    qkv_ref[:, 0:D_ATT] = (qkv[:, 0:D_ATT] * (HEAD_DIM ** -0.5)).astype(BF16)
    qkv_ref[:, D_ATT:D_QKV] = qkv[:, D_ATT:D_QKV].astype(BF16)

    @pl.when(ti >= tail_first)
    def _():
        kt_ref[...] = qkv[:, D_ATT:2 * D_ATT]
        vt_ref[...] = qkv[:, 2 * D_ATT:3 * D_ATT]

    rest = _dot(xb, w_ref[:, D_QKV:D_QKV + D_REST])
    u_ref[...] = rest[:, 0:D_SSM]
    gate_b = rest[:, D_SSM:D_SSM + D_SCONV]
    gate_c = rest[:, D_SSM + D_SCONV:D_SSM + 2 * D_SCONV]
    xv = rest[:, D_SSM + 2 * D_SCONV:D_SSM + 3 * D_SCONV]
    conv = _causal_conv3(ext_ref, gate_c * xv, scw_ref, scb_ref, tm)
    sc_ref[...] = (gate_b * conv).astype(BF16)
    last = ext_ref[pl.ds(tm, SUBLANES), :]
    ext_ref[0:SUBLANES, :] = last
    scn_ref[...] = last


def _inproj_call(x, w_in, scw, scb, sc0, tm, keep):
    bsz, s, _ = x.shape
    nt = s // tm
    ntail = keep // tm
    tail_first = nt - ntail
    row = lambda b, t: (b, t, 0)
    const2 = lambda b, t: (0, 0)
    tail = lambda b, t: (b, jnp.maximum(t - tail_first, 0), 0)
    bonly = lambda b, t: (b, 0, 0)
    return pl.pallas_call(
        functools.partial(_inproj_kernel, tm=tm, tail_first=tail_first),
        grid=(bsz, nt),
        in_specs=[pl.BlockSpec((None, tm, D_MODEL), row),
                  pl.BlockSpec((D_MODEL, D_QKV + D_REST), const2),
                  pl.BlockSpec((CONV_K, D_SCONV), const2),
                  pl.BlockSpec((1, D_SCONV), const2),
                  pl.BlockSpec((None, SUBLANES, D_SCONV), bonly)],
        out_specs=[pl.BlockSpec((None, tm, D_QKV), row),
                   pl.BlockSpec((None, tm, D_ATT), tail),
                   pl.BlockSpec((None, tm, D_ATT), tail),
                   pl.BlockSpec((None, tm, D_SSM), row),
                   pl.BlockSpec((None, tm, D_SCONV), row),
                   pl.BlockSpec((None, SUBLANES, D_SCONV), bonly)],
        out_shape=[jax.ShapeDtypeStruct((bsz, s, D_QKV), BF16),
                   jax.ShapeDtypeStruct((bsz, keep, D_ATT), F32),
                   jax.ShapeDtypeStruct((bsz, keep, D_ATT), F32),
                   jax.ShapeDtypeStruct((bsz, s, D_SSM), F32),
                   jax.ShapeDtypeStruct((bsz, s, D_SCONV), BF16),
                   jax.ShapeDtypeStruct((bsz, SUBLANES, D_SCONV), F32)],
        scratch_shapes=[pltpu.VMEM((tm + SUBLANES, D_SCONV), F32)],
        compiler_params=_cparams(("parallel", "arbitrary")),
        name="in_proj",
    )(x, w_in, scw, scb, sc0)


HEADS_PER_GROUP = 4
GROUP_W = HEADS_PER_GROUP * HEAD_DIM
N_GROUPS = N_HEADS // HEADS_PER_GROUP
Q_STEP = 4 * CHUNK
K_WIN = PAST + Q_STEP


def _softmax_unnormalised(sc):
    m = jnp.max(sc, axis=-1, keepdims=True)
    e = jnp.exp(sc - m)
    return e.astype(BF16), jnp.sum(e, axis=-1, keepdims=True)


def _group_head_of_lane(shape):
    lane = lax.broadcasted_iota(jnp.int32, shape, len(shape) - 1)
    return (lane // HEAD_DIM) % HEADS_PER_GROUP


def _attn_group(q4, k4, v_of_head, bias_of_head):
    head = _group_head_of_lane(q4.shape)
    acc = None
    scale = None
    for h in range(HEADS_PER_GROUP):
        qm = jnp.where(head == h, q4, jnp.zeros_like(q4))
        sc = lax.dot_general(qm, k4, (((1,), (1,)), ((), ())), preferred_element_type=F32)
        e, l = _softmax_unnormalised(sc + bias_of_head(h))
        pv = _dot(e, v_of_head(h))
        acc = pv if acc is None else acc + pv
        inv = 1.0 / l
        scale = inv if scale is None else jnp.where(head == h, inv, scale)
    return acc * scale


def _attn_prompt_kernel(qkv_ref, bias_ref, o_ref, vmask_ref, *, s):
    v = qkv_ref[:, 2 * D_ATT:3 * D_ATT]
    head = _group_head_of_lane(v.shape)
    for h in range(HEADS_PER_GROUP):
        vmask_ref[h] = jnp.where(head == h, v, jnp.zeros_like(v))

    def step(r0, k0, nk):
        for g in range(N_GROUPS):
            cols = slice(g * GROUP_W, (g + 1) * GROUP_W)
            kcols = slice(D_ATT + g * GROUP_W, D_ATT + (g + 1) * GROUP_W)
            out = _attn_group(
                qkv_ref[pl.ds(r0, Q_STEP), cols],
                qkv_ref[pl.ds(k0, nk), kcols],
                lambda h: vmask_ref[h, pl.ds(k0, nk), cols],
                lambda h: bias_ref[g * HEADS_PER_GROUP + h, :, K_WIN - nk:K_WIN])
            o_ref[pl.ds(r0, Q_STEP), cols] = out.astype(BF16)

    n_steps = s // Q_STEP
    n_head_steps = min(PAST // Q_STEP, n_steps)
    for j in range(n_head_steps):
        step(j * Q_STEP, 0, (j + 1) * Q_STEP)

    def body(j, carry):
        r0 = pl.multiple_of(j * Q_STEP, Q_STEP)
        step(r0, pl.multiple_of(r0 - PAST, Q_STEP), K_WIN)
        return carry

    if n_steps > n_head_steps:
        lax.fori_loop(n_head_steps, n_steps, body, 0)


def _attn_prompt_call(qkv, bias):
    bsz, s, _ = qkv.shape
    assert s % Q_STEP == 0
    bsel = lambda b: (b, 0, 0)
    return pl.pallas_call(
        functools.partial(_attn_prompt_kernel, s=s),
        grid=(bsz,),
        in_specs=[pl.BlockSpec((None, s, D_QKV), bsel),
                  pl.BlockSpec((N_HEADS, Q_STEP, K_WIN), lambda b: (0, 0, 0))],
        out_specs=pl.BlockSpec((None, s, D_ATT), bsel),
        out_shape=jax.ShapeDtypeStruct((bsz, s, D_ATT), BF16),
        scratch_shapes=[pltpu.VMEM((HEADS_PER_GROUP, s, D_ATT), BF16)],
        compiler_params=_cparams(("parallel",)),
        name="attention_prompt",
    )(qkv, bias)


def _attn_sample_kernel(qkv_ref, pk_ref, pv_ref, bias_ref, o_ref, k_ref, vmask_ref, *, s):
    k_ref[0:PAST, :] = pk_ref[...]
    k_ref[PAST:PAST + s, :] = qkv_ref[:, D_ATT:2 * D_ATT]
    head = _group_head_of_lane((PAST, D_ATT))
    head_new = _group_head_of_lane((s, D_ATT))
    for h in range(HEADS_PER_GROUP):
        vmask_ref[h, 0:PAST, :] = jnp.where(head == h, pv_ref[...], jnp.zeros_like(pv_ref[...]))
        v = qkv_ref[:, 2 * D_ATT:3 * D_ATT]
        vmask_ref[h, PAST:PAST + s, :] = jnp.where(head_new == h, v, jnp.zeros_like(v))
    for g in range(N_GROUPS):
        cols = slice(g * GROUP_W, (g + 1) * GROUP_W)
        out = _attn_group(qkv_ref[:, cols], k_ref[:, cols],
                          lambda h: vmask_ref[h, :, cols],
                          lambda h: bias_ref[g * HEADS_PER_GROUP + h])
        o_ref[:, cols] = out.astype(BF16)


def _attn_sample_call(qkv, pk, pv, bias):
    bsz, s, _ = qkv.shape
    band = PAST + s
    bsel = lambda b: (b, 0, 0)
    return pl.pallas_call(
        functools.partial(_attn_sample_kernel, s=s),
        grid=(bsz,),
        in_specs=[pl.BlockSpec((None, s, D_QKV), bsel),
                  pl.BlockSpec((None, PAST, D_ATT), bsel),
                  pl.BlockSpec((None, PAST, D_ATT), bsel),
                  pl.BlockSpec((N_HEADS, s, band), lambda b: (0, 0, 0))],
        out_specs=pl.BlockSpec((None, s, D_ATT), bsel),
        out_shape=jax.ShapeDtypeStruct((bsz, s, D_ATT), BF16),
        scratch_shapes=[pltpu.VMEM((band, D_ATT), BF16),
                        pltpu.VMEM((HEADS_PER_GROUP, band, D_ATT), BF16)],
        compiler_params=_cparams(("parallel",)),
        name="attention_sample",
    )(qkv, pk, pv, bias)


SSM_SUB_STEPS = 32


def _gelu_tanh(x):
    c = math.sqrt(2.0 / math.pi)
    return 0.5 * x * (1.0 + jnp.tanh(c * (x + 0.044715 * (x * x * x))))


def _ssm_kernel(u_ref, h0_ref, lbar_ref, bblk_ref, cblk_ref, d_ref, wglu_ref,
                o_ref, hn_ref, st_ref, h_ref, *, tt):
    ti = pl.program_id(1)
    rows = tt * SUBLANES

    @pl.when(ti == 0)
    def _():
        h_ref[...] = h0_ref[...]

    ut = jnp.swapaxes(u_ref[...], 0, 1).reshape(rows, D_SSM)
    ub = ut.astype(BF16)

    lr = jnp.broadcast_to(lbar_ref[0:1, :], (SUBLANES, D_STATE))
    li = jnp.broadcast_to(lbar_ref[1:2, :], (SUBLANES, D_STATE))
    hr = h_ref[:, 0:D_STATE]
    hi = h_ref[:, D_STATE:2 * D_STATE]
    sub = min(SSM_SUB_STEPS, tt)
    outs = []
    for k in range(tt // sub):
        blk = slice(k * sub * SUBLANES, (k + 1) * sub * SUBLANES)
        bu = _dot(ub[blk], bblk_ref[...])
        for t in range(sub):
            r = slice(t * SUBLANES, (t + 1) * SUBLANES)
            g = slice(blk.start + r.start, blk.start + r.stop)
            hr, hi = (lr * hr - li * hi + bu[r, 0:D_STATE],
                      lr * hi + li * hr + bu[r, D_STATE:2 * D_STATE])
            st_ref[g, 0:D_STATE] = hr
            st_ref[g, D_STATE:2 * D_STATE] = hi
        y = _dot(st_ref[blk, :].astype(BF16), cblk_ref[...]) + d_ref[...] * ut[blk]
        z = _gelu_tanh(y)
        gate = jax.nn.sigmoid(_dot(z.astype(BF16), wglu_ref[...]))
        outs.append(z * gate)
    h_ref[:, 0:D_STATE] = hr
    h_ref[:, D_STATE:2 * D_STATE] = hi
    hn_ref[:, 0:D_STATE] = hr
    hn_ref[:, D_STATE:2 * D_STATE] = hi
    out = jnp.concatenate(outs, axis=0).reshape(tt, SUBLANES, D_SSM)
    o_ref[...] = jnp.swapaxes(out, 0, 1).astype(BF16)


def _ssm_call(u, h0, lbar, bblk, cblk, d, wglu, tt):
    bsz, s, _ = u.shape
    nb = bsz // SUBLANES
    blk = lambda b, t: (b, t, 0)
    const2 = lambda b, t: (0, 0)
    hsel = lambda b, t: (b, 0)
    return pl.pallas_call(
        functools.partial(_ssm_kernel, tt=tt),
        grid=(nb, s // tt),
        in_specs=[pl.BlockSpec((SUBLANES, tt, D_SSM), blk),
                  pl.BlockSpec((SUBLANES, 2 * D_STATE), hsel),
                  pl.BlockSpec((2, D_STATE), const2),
                  pl.BlockSpec((D_SSM, 2 * D_STATE), const2),
                  pl.BlockSpec((2 * D_STATE, D_SSM), const2),
                  pl.BlockSpec((1, D_SSM), const2),
                  pl.BlockSpec((D_SSM, D_SSM), const2)],
        out_specs=[pl.BlockSpec((SUBLANES, tt, D_SSM), blk),
                   pl.BlockSpec((SUBLANES, 2 * D_STATE), hsel)],
        out_shape=[jax.ShapeDtypeStruct((bsz, s, D_SSM), BF16),
                   jax.ShapeDtypeStruct((bsz, 2 * D_STATE), F32)],
        scratch_shapes=[pltpu.VMEM((tt * SUBLANES, 2 * D_STATE), F32),
                        pltpu.VMEM((SUBLANES, 2 * D_STATE), F32)],
        compiler_params=_cparams(("parallel", "arbitrary")),
        name="ssm",
    )(u, h0, lbar, bblk, cblk, d, wglu)


FFN_ROW_BLOCK = 512


def _mix_ffn_kernel(att_ref, ssm_ref, sc_ref, x_ref, wo_ref, g1_ref, b1_ref,
                    w1_ref, cw_ref, cb_ref, c0_ref, w2_ref, g2_ref, b2_ref,
                    o_ref, cn_ref, carry_ref, *, tm):
    ti = pl.program_id(1)

    @pl.when(ti == 0)
    def _():
        carry_ref[...] = c0_ref[...]

    rb = min(FFN_ROW_BLOCK, tm)
    for i in range(tm // rb):
        rows = slice(i * rb, (i + 1) * rb)
        mix = _dot(att_ref[rows, :], wo_ref[0:D_ATT, :])
        mix = mix + _dot(ssm_ref[rows, :], wo_ref[D_ATT:D_ATT + D_SSM, :])
        mix = mix + _dot(sc_ref[rows, :], wo_ref[D_ATT + D_SSM:D_MODEL, :])
        x1 = _layer_norm(ALPHA * x_ref[rows, :] + mix, g1_ref[...], b1_ref[...])

        up = _dot(x1.astype(BF16), w1_ref[...])
        ext = jnp.concatenate([carry_ref[...], up], axis=0)
        m1 = pltpu.roll(ext, 1, 0)[SUBLANES:]
        m2 = pltpu.roll(ext, 2, 0)[SUBLANES:]
        carry_ref[...] = up[rb - SUBLANES:rb]
        conv = cw_ref[0:1, :] * m2 + cw_ref[1:2, :] * m1 + cw_ref[2:3, :] * up + cb_ref[...]
        gate = conv[:, 0:D_FF]
        val = conv[:, D_FF:2 * D_FF]
        h = (gate * jax.nn.sigmoid(gate) * val).astype(BF16)
        y = _dot(h, w2_ref[...])
        o_ref[rows, :] = _layer_norm(ALPHA * x1 + y, g2_ref[...], b2_ref[...])
    cn_ref[...] = carry_ref[...]


def _mix_ffn_call(att, ssm, sc, x, w_out, g1, b1, w1, cw, cb, c0, w2, g2, b2, tm):
    bsz, s, _ = x.shape
    row = lambda bi, t: (bi, t, 0)
    const2 = lambda bi, t: (0, 0)
    bonly = lambda bi, t: (bi, 0, 0)
    vec = pl.BlockSpec((1, D_MODEL), const2)
    return pl.pallas_call(
        functools.partial(_mix_ffn_kernel, tm=tm),
        grid=(bsz, s // tm),
        in_specs=[pl.BlockSpec((None, tm, D_ATT), row),
                  pl.BlockSpec((None, tm, D_SSM), row),
                  pl.BlockSpec((None, tm, D_SCONV), row),
                  pl.BlockSpec((None, tm, D_MODEL), row),
                  pl.BlockSpec((D_MODEL, D_MODEL), const2), vec, vec,
                  pl.BlockSpec((D_MODEL, 2 * D_FF), const2),
                  pl.BlockSpec((CONV_K, 2 * D_FF), const2),
                  pl.BlockSpec((1, 2 * D_FF), const2),
                  pl.BlockSpec((None, SUBLANES, 2 * D_FF), bonly),
                  pl.BlockSpec((D_FF, D_MODEL), const2), vec, vec],
        out_specs=[pl.BlockSpec((None, tm, D_MODEL), row),
                   pl.BlockSpec((None, SUBLANES, 2 * D_FF), bonly)],
        out_shape=[jax.ShapeDtypeStruct((bsz, s, D_MODEL), F32),
                   jax.ShapeDtypeStruct((bsz, SUBLANES, 2 * D_FF), F32)],
        scratch_shapes=[pltpu.VMEM((SUBLANES, 2 * D_FF), F32)],
        compiler_params=_cparams(("parallel", "arbitrary")),
        name="mix_ffn",
    )(att, ssm, sc, x, w_out, g1, b1, w1, cw, cb, c0, w2, g2, b2)


def _rel_bias_table(rel_bias, tq, chunk):
    nk = PAST + tq
    period = nk + tq
    dist = (nk - 1) - jnp.arange(period)
    by_lag = rel_bias[:, jnp.clip(dist, -(CHUNK - 1), REL_MAX) + (CHUNK - 1)].astype(F32)
    rolled = jnp.roll(by_lag, -(tq - 1), axis=-1)
    n_heads = rel_bias.shape[0]
    bias = jnp.tile(rolled, (1, tq))[:, :tq * (period - 1)].reshape(n_heads, tq, period - 1)[:, :, :nk]
    if tq == chunk:
        return bias
    t_idx = np.arange(tq)[:, None]
    s_idx = np.arange(nk)[None, :]
    key_chunk = s_idx // chunk - t_idx // chunk
    in_band = (key_chunk >= 0) & (key_chunk <= N_PAST_CHUNKS)
    return jnp.where(in_band[None], bias, -jnp.inf)


def _ssm_params(lam_re, lam_im, log_dt, b_re, b_im, c_re, c_im):
    lam = lax.complex(lam_re.astype(F32), lam_im.astype(F32))
    dt = jnp.exp(log_dt.astype(F32))[:, None]
    lbar = jnp.exp(lam * dt)
    bbar = ((lbar - 1.0) / lam)[:, :, None] * lax.complex(b_re.astype(F32), b_im.astype(F32))
    eye = jnp.eye(N_SSM_GROUPS, dtype=F32)

    def in_map(m):
        return jnp.einsum('gpc,gh->gchp', m, eye).reshape(D_SSM, D_STATE)

    def out_map(m):
        return jnp.einsum('gcp,gh->gphc', m, eye).reshape(D_STATE, D_SSM)

    bblk = jnp.concatenate([in_map(bbar.real), in_map(bbar.imag)], axis=1).astype(BF16)
    cblk = jnp.concatenate([out_map(c_re.astype(F32)), out_map(-c_im.astype(F32))], axis=0).astype(BF16)
    lbar2 = jnp.stack([lbar.real.reshape(D_STATE), lbar.imag.reshape(D_STATE)], axis=0)
    return lbar2, bblk, cblk


def _pad_rows(buf):
    return jnp.pad(buf.astype(F32), ((0, 0), (SUBLANES - (CONV_K - 1), 0), (0, 0)))


def _trunk_layer(x, lp, past, tm, tt):
    bsz, s, _ = x.shape
    if past is None:
        keep = min(PAST, s)
        h0 = jnp.zeros((bsz, 2 * D_STATE), F32)
        sc0 = jnp.zeros((bsz, SUBLANES, D_SCONV), F32)
        ff0 = jnp.zeros((bsz, SUBLANES, 2 * D_FF), F32)
    else:
        k_past, v_past, h_re0, h_im0, sconv_buf, ffn_buf = past
        keep = s
        pk = k_past.reshape(bsz, PAST, D_ATT).astype(BF16)
        pv = v_past.reshape(bsz, PAST, D_ATT).astype(BF16)
        h0 = jnp.concatenate([h_re0.reshape(bsz, D_STATE), h_im0.reshape(bsz, D_STATE)], axis=1).astype(F32)
        sc0 = _pad_rows(sconv_buf)
        ff0 = _pad_rows(ffn_buf)

    qkv, k_new, v_new, u, sconv_out, sc_new = _inproj_call(
        x, lp['w_in'], lp['sconv_w'], lp['sconv_b'], sc0, tm, keep)
    if past is None:
        att = _attn_prompt_call(qkv, _rel_bias_table(lp['rel_bias'], Q_STEP, CHUNK))
    else:
        att = _attn_sample_call(qkv, pk, pv, _rel_bias_table(lp['rel_bias'], s, s))
    ssm_out, h_new = _ssm_call(u, h0, lp['lbar'], lp['bblk'], lp['cblk'], lp['d'], lp['w_glu'], tt)
    x2, ff_new = _mix_ffn_call(att, ssm_out, sconv_out, x, lp['w_out'], lp['ln1_g'], lp['ln1_b'],
                               lp['w_ff_in'], lp['ffn_conv_w'], lp['ffn_conv_b'], ff0,
                               lp['w_ff_out'], lp['ln2_g'], lp['ln2_b'], tm)
    new = (k_new.reshape(bsz, keep, N_HEADS, HEAD_DIM), v_new.reshape(bsz, keep, N_HEADS, HEAD_DIM),
           h_new[:, 0:D_STATE].reshape(bsz, N_SSM_GROUPS, SSM_STATE),
           h_new[:, D_STATE:].reshape(bsz, N_SSM_GROUPS, SSM_STATE),
           sc_new[:, SUBLANES - (CONV_K - 1):], ff_new[:, SUBLANES - (CONV_K - 1):])
    return x2, new


def _tiles(s):
    tm = min(512, s)
    tt = min(128, s)
    return tm, tt


def kernel(x_prompt, x_sample, cache_k, cache_v, state_ssm_re, state_ssm_im, cache_sconv, cache_ffn_conv, ln_in_g, ln_in_b, w_in, rel_bias, ssm_lam_re, ssm_lam_im, ssm_log_dt, ssm_b_re, ssm_b_im, ssm_c_re, ssm_c_im, ssm_d, w_glu, sconv_w, sconv_b, w_out, ln1_g, ln1_b, w_ff_in, ffn_conv_w, ffn_conv_b, w_ff_out, ln2_g, ln2_b):
    bp, sp, _ = x_prompt.shape
    bs, ss, _ = x_sample.shape
    g_in = ln_in_g.reshape(1, D_MODEL)
    b_in = ln_in_b.reshape(1, D_MODEL)
    xp = _ln_call(x_prompt.reshape(bp * sp, D_MODEL), g_in, b_in, min(512, bp * sp)).reshape(bp, sp, D_MODEL)
    xs = _ln_call(x_sample.reshape(bs * ss, D_MODEL), g_in, b_in, min(512, bs * ss)).reshape(bs, ss, D_MODEL)
    st_p = [[] for _ in range(6)]
    st_s = [[] for _ in range(6)]
    for l in range(DEPTH):
        lbar, bblk, cblk = _ssm_params(ssm_lam_re[l], ssm_lam_im[l], ssm_log_dt[l], ssm_b_re[l], ssm_b_im[l],
                                       ssm_c_re[l], ssm_c_im[l])
        lp = {'w_in': w_in[l].astype(BF16), 'rel_bias': rel_bias[l],
              'lbar': lbar, 'bblk': bblk, 'cblk': cblk,
              'd': ssm_d[l].reshape(1, D_SSM), 'w_glu': w_glu[l].astype(BF16),
              'sconv_w': sconv_w[l], 'sconv_b': sconv_b[l].reshape(1, D_SCONV),
              'w_out': w_out[l].astype(BF16),
              'ln1_g': ln1_g[l].reshape(1, D_MODEL), 'ln1_b': ln1_b[l].reshape(1, D_MODEL),
              'w_ff_in': w_ff_in[l].astype(BF16), 'ffn_conv_w': ffn_conv_w[l],
              'ffn_conv_b': ffn_conv_b[l].reshape(1, 2 * D_FF),
              'w_ff_out': w_ff_out[l].astype(BF16),
              'ln2_g': ln2_g[l].reshape(1, D_MODEL), 'ln2_b': ln2_b[l].reshape(1, D_MODEL)}
        xp, new_p = _trunk_layer(xp, lp, None, *_tiles(sp))
        past = (cache_k[l], cache_v[l], state_ssm_re[l], state_ssm_im[l], cache_sconv[l], cache_ffn_conv[l])
        xs, new_s = _trunk_layer(xs, lp, past, *_tiles(ss))
        for i in range(6):
            st_p[i].append(new_p[i])
            st_s[i].append(new_s[i])
    k_p, v_p, hre_p, him_p, sc_p, ff_p = [jnp.stack(a, axis=0) for a in st_p]
    k_s, v_s, hre_s, him_s, sc_s, ff_s = [jnp.stack(a, axis=0) for a in st_s]
    return (xp, xs, k_p, v_p, k_s, v_s, hre_p, him_p, hre_s, him_s, sc_p, sc_s, ff_p, ff_s)
```

```python
import functools
import math

import jax
import jax.numpy as jnp
import numpy as np
from jax import lax
from jax.experimental import pallas as pl
from jax.experimental.pallas import tpu as pltpu

D_MODEL = 1024
DEPTH = 4
CHUNK = 64
N_HEADS = 8
HEAD_DIM = 64
D_ATT = N_HEADS * HEAD_DIM
N_PAST_CHUNKS = 8
PAST = N_PAST_CHUNKS * CHUNK
REL_MAX = 128
SSM_GROUP = 16
N_SSM_GROUPS = 16
D_SSM = SSM_GROUP * N_SSM_GROUPS
SSM_STATE = 64
D_STATE = N_SSM_GROUPS * SSM_STATE
D_SCONV = 256
CONV_K = 3
D_FF = 2048
D_QKV = 3 * D_ATT
D_REST = D_SSM + 3 * D_SCONV
ALPHA = (2 * DEPTH) ** 0.25
LN_EPS = 1e-5

SUBLANES = 8
VMEM_LIMIT = 56 * 1024 * 1024

F32 = jnp.float32
BF16 = jnp.bfloat16


def _cparams(sem):
    return pltpu.CompilerParams(dimension_semantics=sem, vmem_limit_bytes=VMEM_LIMIT)


def _layer_norm(x, g, b):
    mu = jnp.mean(x, axis=-1, keepdims=True)
    xc = x - mu
    var = jnp.mean(xc * xc, axis=-1, keepdims=True)
    return xc * lax.rsqrt(var + LN_EPS) * g + b


def _dot(a, b):
    return jnp.dot(a, b, preferred_element_type=F32)


def _causal_conv3(cur, hist_ref, w_ref, b_ref, nb, tt):
    outs = []
    for s in range(nb):
        v = cur[s * tt:(s + 1) * tt]
        ext = jnp.concatenate([hist_ref[s], v], axis=0)
        m1 = pltpu.roll(ext, 1, 0)[SUBLANES:]
        m2 = pltpu.roll(ext, 2, 0)[SUBLANES:]
        hist_ref[s] = v[tt - SUBLANES:tt]
        outs.append(w_ref[0:1, :] * m2 + w_ref[1:2, :] * m1 + w_ref[2:3, :] * v + b_ref[...])
    return outs[0] if nb == 1 else jnp.concatenate(outs, axis=0)


def _ln_kernel(x_ref, g_ref, b_ref, o_ref):
    o_ref[...] = _layer_norm(x_ref[...], g_ref[...], b_ref[...])


def _ln_call(x2d, g, b, tm):
    rows = x2d.shape[0]
    return pl.pallas_call(
        _ln_kernel,
        grid=(rows // tm,),
        in_specs=[pl.BlockSpec((tm, D_MODEL), lambda i: (i, 0)),
                  pl.BlockSpec((1, D_MODEL), lambda i: (0, 0)),
                  pl.BlockSpec((1, D_MODEL), lambda i: (0, 0))],
        out_specs=pl.BlockSpec((tm, D_MODEL), lambda i: (i, 0)),
        out_shape=jax.ShapeDtypeStruct((rows, D_MODEL), F32),
        compiler_params=_cparams(("parallel",)),
        name="ln_in",
    )(x2d, g, b)


SSM_SUB_STEPS = 16


def _gelu_tanh(x):
    c = math.sqrt(2.0 / math.pi)
    return 0.5 * x * (1.0 + jnp.tanh(c * (x + 0.044715 * (x * x * x))))


def _in_ssm_kernel(x_ref, w_ref, scw_ref, scb_ref, sc0_ref, h0_ref,
                   lbar_ref, bblk_ref, cblk_ref, d_ref, wglu_ref,
                   qkv_ref, kt_ref, vt_ref, ssm_ref, sc_ref, scn_ref, hn_ref,
                   hist_ref, h_ref, st_ref, *, tt):
    ti = pl.program_id(1)
    nb = SUBLANES
    rows = nb * tt

    @pl.when(ti == 0)
    def _():
        hist_ref[...] = sc0_ref[...]
        h_ref[...] = h0_ref[...]

    xb = x_ref[...].reshape(rows, D_MODEL).astype(BF16)
    u = _dot(xb, w_ref[:, D_QKV:D_QKV + D_SSM])

    def project_q():
        q = _dot(xb, w_ref[:, 0:D_ATT]) * (HEAD_DIM ** -0.5)
        qkv_ref[:, :, 0:D_ATT] = q.astype(BF16).reshape(nb, tt, D_ATT)

    def project_kv(tail_ref, lo):
        kv = _dot(xb, w_ref[:, lo:lo + D_ATT])
        qkv_ref[:, :, lo:lo + D_ATT] = kv.astype(BF16).reshape(nb, tt, D_ATT)
        tail_ref[...] = kv.reshape(nb, tt, D_ATT)

    def project_sconv():
        g = _dot(xb, w_ref[:, D_QKV + D_SSM:D_QKV + D_REST])
        gate_b = g[:, 0:D_SCONV]
        gate_c = g[:, D_SCONV:2 * D_SCONV]
        xv = g[:, 2 * D_SCONV:3 * D_SCONV]
        conv = _causal_conv3(gate_c * xv, hist_ref, scw_ref, scb_ref, nb, tt)
        sc_ref[...] = (gate_b * conv).astype(BF16).reshape(nb, tt, D_SCONV)
        scn_ref[...] = hist_ref[...]

    side_work = [project_q, functools.partial(project_kv, kt_ref, D_ATT),
                 functools.partial(project_kv, vt_ref, 2 * D_ATT), project_sconv]

    ut = jnp.swapaxes(u.reshape(nb, tt, D_SSM), 0, 1).reshape(rows, D_SSM)
    ub = ut.astype(BF16)
    lr = jnp.broadcast_to(lbar_ref[0:1, :], (SUBLANES, D_STATE))
    li = jnp.broadcast_to(lbar_ref[1:2, :], (SUBLANES, D_STATE))
    hr = h_ref[:, 0:D_STATE]
    hi = h_ref[:, D_STATE:2 * D_STATE]
    sub = min(SSM_SUB_STEPS, tt)
    n_sub = tt // sub
    outs = []
    for k in range(n_sub):
        blk = slice(k * sub * SUBLANES, (k + 1) * sub * SUBLANES)
        bu = _dot(ub[blk], bblk_ref[...])
        for work in side_work[k * len(side_work) // n_sub:(k + 1) * len(side_work) // n_sub]:
            work()
        for t in range(sub):
            r = slice(t * SUBLANES, (t + 1) * SUBLANES)
            g = slice(blk.start + r.start, blk.start + r.stop)
            hr, hi = (lr * hr - li * hi + bu[r, 0:D_STATE],
                      lr * hi + li * hr + bu[r, D_STATE:2 * D_STATE])
            st_ref[g, 0:D_STATE] = hr
            st_ref[g, D_STATE:2 * D_STATE] = hi
        y = _dot(st_ref[blk, :].astype(BF16), cblk_ref[...]) + d_ref[...] * ut[blk]
        z = _gelu_tanh(y)
        gate = jax.nn.sigmoid(_dot(z.astype(BF16), wglu_ref[...]))
        outs.append(z * gate)
    h_ref[:, 0:D_STATE] = hr
    h_ref[:, D_STATE:2 * D_STATE] = hi
    hn_ref[:, 0:D_STATE] = hr
    hn_ref[:, D_STATE:2 * D_STATE] = hi
    out = jnp.concatenate(outs, axis=0).reshape(tt, nb, D_SSM)
    ssm_ref[...] = jnp.swapaxes(out, 0, 1).astype(BF16)


def _in_ssm_call(x, w_in, scw, scb, sc0, h0, lbar, bblk, cblk, d, wglu, tt, keep):
    bsz, s, _ = x.shape
    nb = SUBLANES
    nt = s // tt
    tail_first = nt - keep // tt
    blk = lambda b, t: (b, t, 0)
    tail = lambda b, t: (b, jnp.maximum(t - tail_first, 0), 0)
    bonly3 = lambda b, t: (b, 0, 0)
    bonly2 = lambda b, t: (b, 0)
    const2 = lambda b, t: (0, 0)
    return pl.pallas_call(
        functools.partial(_in_ssm_kernel, tt=tt),
        grid=(bsz // nb, nt),
        in_specs=[pl.BlockSpec((nb, tt, D_MODEL), blk),
                  pl.BlockSpec((D_MODEL, D_QKV + D_REST), const2),
                  pl.BlockSpec((CONV_K, D_SCONV), const2),
                  pl.BlockSpec((1, D_SCONV), const2),
                  pl.BlockSpec((nb, SUBLANES, D_SCONV), bonly3),
                  pl.BlockSpec((nb, 2 * D_STATE), bonly2),
                  pl.BlockSpec((2, D_STATE), const2),
                  pl.BlockSpec((D_SSM, 2 * D_STATE), const2),
                  pl.BlockSpec((2 * D_STATE, D_SSM), const2),
                  pl.BlockSpec((1, D_SSM), const2),
                  pl.BlockSpec((D_SSM, D_SSM), const2)],
        out_specs=[pl.BlockSpec((nb, tt, D_QKV), blk),
                   pl.BlockSpec((nb, tt, D_ATT), tail),
                   pl.BlockSpec((nb, tt, D_ATT), tail),
                   pl.BlockSpec((nb, tt, D_SSM), blk),
                   pl.BlockSpec((nb, tt, D_SCONV), blk),
                   pl.BlockSpec((nb, SUBLANES, D_SCONV), bonly3),
                   pl.BlockSpec((nb, 2 * D_STATE), bonly2)],
        out_shape=[jax.ShapeDtypeStruct((bsz, s, D_QKV), BF16),
                   jax.ShapeDtypeStruct((bsz, keep, D_ATT), F32),
                   jax.ShapeDtypeStruct((bsz, keep, D_ATT), F32),
                   jax.ShapeDtypeStruct((bsz, s, D_SSM), BF16),
                   jax.ShapeDtypeStruct((bsz, s, D_SCONV), BF16),
                   jax.ShapeDtypeStruct((bsz, SUBLANES, D_SCONV), F32),
                   jax.ShapeDtypeStruct((bsz, 2 * D_STATE), F32)],
        scratch_shapes=[pltpu.VMEM((nb, SUBLANES, D_SCONV), F32),
                        pltpu.VMEM((nb, 2 * D_STATE), F32),
                        pltpu.VMEM((nb * tt, 2 * D_STATE), F32)],
        compiler_params=_cparams(("parallel", "arbitrary")),
        name="in_ssm",
    )(x, w_in, scw, scb, sc0, h0, lbar, bblk, cblk, d, wglu)


HEADS_PER_GROUP = 4
GROUP_W = HEADS_PER_GROUP * HEAD_DIM
N_GROUPS = N_HEADS // HEADS_PER_GROUP
Q_STEP = 4 * CHUNK
K_WIN = PAST + Q_STEP


def _softmax_unnormalised(sc):
    m = jnp.max(sc, axis=-1, keepdims=True)
    e = jnp.exp(sc - m)
    return e.astype(BF16), jnp.sum(e, axis=-1, keepdims=True)


def _group_head_of_lane(shape):
    lane = lax.broadcasted_iota(jnp.int32, shape, len(shape) - 1)
    return (lane // HEAD_DIM) % HEADS_PER_GROUP


def _attn_group(q4, k4, v_of_head, bias_of_head):
    head = _group_head_of_lane(q4.shape)
    acc = None
    scale = None
    for h in range(HEADS_PER_GROUP):
        qm = jnp.where(head == h, q4, jnp.zeros_like(q4))
        sc = lax.dot_general(qm, k4, (((1,), (1,)), ((), ())), preferred_element_type=F32)
        e, l = _softmax_unnormalised(sc + bias_of_head(h))
        pv = _dot(e, v_of_head(h))
        acc = pv if acc is None else acc + pv
        inv = 1.0 / l
        scale = inv if scale is None else jnp.where(head == h, inv, scale)
    return acc * scale


def _attn_prompt_kernel(qkv_ref, bias_ref, o_ref, vmask_ref, *, s):
    v = qkv_ref[:, 2 * D_ATT:3 * D_ATT]
    head = _group_head_of_lane(v.shape)
    for h in range(HEADS_PER_GROUP):
        vmask_ref[h] = jnp.where(head == h, v, jnp.zeros_like(v))

    def step(r0, k0, nk):
        for g in range(N_GROUPS):
            cols = slice(g * GROUP_W, (g + 1) * GROUP_W)
            kcols = slice(D_ATT + g * GROUP_W, D_ATT + (g + 1) * GROUP_W)
            out = _attn_group(
                qkv_ref[pl.ds(r0, Q_STEP), cols],
                qkv_ref[pl.ds(k0, nk), kcols],
                lambda h: vmask_ref[h, pl.ds(k0, nk), cols],
                lambda h: bias_ref[g * HEADS_PER_GROUP + h, :, K_WIN - nk:K_WIN])
            o_ref[pl.ds(r0, Q_STEP), cols] = out.astype(BF16)

    n_steps = s // Q_STEP
    n_head_steps = min(PAST // Q_STEP, n_steps)
    for j in range(n_head_steps):
        step(j * Q_STEP, 0, (j + 1) * Q_STEP)

    def body(j, carry):
        r0 = pl.multiple_of(j * Q_STEP, Q_STEP)
        step(r0, pl.multiple_of(r0 - PAST, Q_STEP), K_WIN)
        return carry

    if n_steps > n_head_steps:
        lax.fori_loop(n_head_steps, n_steps, body, 0)


def _attn_prompt_call(qkv, bias):
    bsz, s, _ = qkv.shape
    assert s % Q_STEP == 0
    bsel = lambda b: (b, 0, 0)
    return pl.pallas_call(
        functools.partial(_attn_prompt_kernel, s=s),
        grid=(bsz,),
        in_specs=[pl.BlockSpec((None, s, D_QKV), bsel),
                  pl.BlockSpec((N_HEADS, Q_STEP, K_WIN), lambda b: (0, 0, 0))],
        out_specs=pl.BlockSpec((None, s, D_ATT), bsel),
        out_shape=jax.ShapeDtypeStruct((bsz, s, D_ATT), BF16),
        scratch_shapes=[pltpu.VMEM((HEADS_PER_GROUP, s, D_ATT), BF16)],
        compiler_params=_cparams(("parallel",)),
        name="attention_prompt",
    )(qkv, bias)


def _attn_sample_kernel(qkv_ref, pk_ref, pv_ref, bias_ref, o_ref, k_ref, vmask_ref, *, s):
    k_ref[0:PAST, :] = pk_ref[...]
    k_ref[PAST:PAST + s, :] = qkv_ref[:, D_ATT:2 * D_ATT]
    head = _group_head_of_lane((PAST, D_ATT))
    head_new = _group_head_of_lane((s, D_ATT))
    for h in range(HEADS_PER_GROUP):
        vmask_ref[h, 0:PAST, :] = jnp.where(head == h, pv_ref[...], jnp.zeros_like(pv_ref[...]))
        v = qkv_ref[:, 2 * D_ATT:3 * D_ATT]
        vmask_ref[h, PAST:PAST + s, :] = jnp.where(head_new == h, v, jnp.zeros_like(v))
    for g in range(N_GROUPS):
        cols = slice(g * GROUP_W, (g + 1) * GROUP_W)
        out = _attn_group(qkv_ref[:, cols], k_ref[:, cols],
                          lambda h: vmask_ref[h, :, cols],
                          lambda h: bias_ref[g * HEADS_PER_GROUP + h])
        o_ref[:, cols] = out.astype(BF16)


def _attn_sample_call(qkv, pk, pv, bias):
    bsz, s, _ = qkv.shape
    band = PAST + s
    bsel = lambda b: (b, 0, 0)
    return pl.pallas_call(
        functools.partial(_attn_sample_kernel, s=s),
        grid=(bsz,),
        in_specs=[pl.BlockSpec((None, s, D_QKV), bsel),
                  pl.BlockSpec((None, PAST, D_ATT), bsel),
                  pl.BlockSpec((None, PAST, D_ATT), bsel),
                  pl.BlockSpec((N_HEADS, s, band), lambda b: (0, 0, 0))],
        out_specs=pl.BlockSpec((None, s, D_ATT), bsel),
        out_shape=jax.ShapeDtypeStruct((bsz, s, D_ATT), BF16),
        scratch_shapes=[pltpu.VMEM((band, D_ATT), BF16),
                        pltpu.VMEM((HEADS_PER_GROUP, band, D_ATT), BF16)],
        compiler_params=_cparams(("parallel",)),
        name="attention_sample",
    )(qkv, pk, pv, bias)


def _mix_ffn_kernel(att_ref, ssm_ref, sc_ref, x_ref, wo_ref, g1_ref, b1_ref,
                    w1_ref, cw_ref, cb_ref, c0_ref, w2_ref, g2_ref, b2_ref,
                    o_ref, cn_ref, hist_ref, *, nb, tt):
    ti = pl.program_id(1)
    rows = nb * tt

    @pl.when(ti == 0)
    def _():
        hist_ref[...] = c0_ref[...]

    mix = _dot(att_ref[...].reshape(rows, D_ATT), wo_ref[0:D_ATT, :])
    mix = mix + _dot(ssm_ref[...].reshape(rows, D_SSM), wo_ref[D_ATT:D_ATT + D_SSM, :])
    mix = mix + _dot(sc_ref[...].reshape(rows, D_SCONV), wo_ref[D_ATT + D_SSM:D_MODEL, :])
    x1 = _layer_norm(ALPHA * x_ref[...].reshape(rows, D_MODEL) + mix, g1_ref[...], b1_ref[...])

    up = _dot(x1.astype(BF16), w1_ref[...])
    conv = _causal_conv3(up, hist_ref, cw_ref, cb_ref, nb, tt)
    cn_ref[...] = hist_ref[...]
    gate = conv[:, 0:D_FF]
    val = conv[:, D_FF:2 * D_FF]
    h = (gate * jax.nn.sigmoid(gate) * val).astype(BF16)
    y = _dot(h, w2_ref[...])
    o_ref[...] = _layer_norm(ALPHA * x1 + y, g2_ref[...], b2_ref[...]).reshape(nb, tt, D_MODEL)


def _mix_ffn_call(att, ssm, sc, x, w_out, g1, b1, w1, cw, cb, c0, w2, g2, b2, nb, tt):
    bsz, s, _ = x.shape
    blk = lambda b, t: (b, t, 0)
    bonly = lambda b, t: (b, 0, 0)
    const2 = lambda b, t: (0, 0)
    vec = pl.BlockSpec((1, D_MODEL), const2)
    return pl.pallas_call(
        functools.partial(_mix_ffn_kernel, nb=nb, tt=tt),
        grid=(bsz // nb, s // tt),
        in_specs=[pl.BlockSpec((nb, tt, D_ATT), blk),
                  pl.BlockSpec((nb, tt, D_SSM), blk),
                  pl.BlockSpec((nb, tt, D_SCONV), blk),
                  pl.BlockSpec((nb, tt, D_MODEL), blk),
                  pl.BlockSpec((D_MODEL, D_MODEL), const2), vec, vec,
                  pl.BlockSpec((D_MODEL, 2 * D_FF), const2),
                  pl.BlockSpec((CONV_K, 2 * D_FF), const2),
                  pl.BlockSpec((1, 2 * D_FF), const2),
                  pl.BlockSpec((nb, SUBLANES, 2 * D_FF), bonly),
                  pl.BlockSpec((D_FF, D_MODEL), const2), vec, vec],
        out_specs=[pl.BlockSpec((nb, tt, D_MODEL), blk),
                   pl.BlockSpec((nb, SUBLANES, 2 * D_FF), bonly)],
        out_shape=[jax.ShapeDtypeStruct((bsz, s, D_MODEL), F32),
                   jax.ShapeDtypeStruct((bsz, SUBLANES, 2 * D_FF), F32)],
        scratch_shapes=[pltpu.VMEM((nb, SUBLANES, 2 * D_FF), F32)],
        compiler_params=_cparams(("parallel", "arbitrary")),
        name="mix_ffn",
    )(att, ssm, sc, x, w_out, g1, b1, w1, cw, cb, c0, w2, g2, b2)


def _rel_bias_table(rel_bias, tq, chunk):
    nk = PAST + tq
    period = nk + tq
    dist = (nk - 1) - jnp.arange(period)
    by_lag = rel_bias[:, jnp.clip(dist, -(CHUNK - 1), REL_MAX) + (CHUNK - 1)].astype(F32)
    rolled = jnp.roll(by_lag, -(tq - 1), axis=-1)
    n_heads = rel_bias.shape[0]
    bias = jnp.tile(rolled, (1, tq))[:, :tq * (period - 1)].reshape(n_heads, tq, period - 1)[:, :, :nk]
    if tq == chunk:
        return bias
    t_idx = np.arange(tq)[:, None]
    s_idx = np.arange(nk)[None, :]
    key_chunk = s_idx // chunk - t_idx // chunk
    in_band = (key_chunk >= 0) & (key_chunk <= N_PAST_CHUNKS)
    return jnp.where(in_band[None], bias, -jnp.inf)


def _ssm_params(lam_re, lam_im, log_dt, b_re, b_im, c_re, c_im):
    lam = lax.complex(lam_re.astype(F32), lam_im.astype(F32))
    dt = jnp.exp(log_dt.astype(F32))[:, None]
    lbar = jnp.exp(lam * dt)
    bbar = ((lbar - 1.0) / lam)[:, :, None] * lax.complex(b_re.astype(F32), b_im.astype(F32))
    eye = jnp.eye(N_SSM_GROUPS, dtype=F32)

    def in_map(m):
        return jnp.einsum('gpc,gh->gchp', m, eye).reshape(D_SSM, D_STATE)

    def out_map(m):
        return jnp.einsum('gcp,gh->gphc', m, eye).reshape(D_STATE, D_SSM)

    bblk = jnp.concatenate([in_map(bbar.real), in_map(bbar.imag)], axis=1).astype(BF16)
    cblk = jnp.concatenate([out_map(c_re.astype(F32)), out_map(-c_im.astype(F32))], axis=0).astype(BF16)
    lbar2 = jnp.stack([lbar.real.reshape(D_STATE), lbar.imag.reshape(D_STATE)], axis=0)
    return lbar2, bblk, cblk


def _pad_rows(buf):
    return jnp.pad(buf.astype(F32), ((0, 0), (SUBLANES - (CONV_K - 1), 0), (0, 0)))


def _trunk_layer(x, lp, past, tiles):
    bsz, s, _ = x.shape
    tt_in, nb_ffn, tt_ffn = tiles
    if past is None:
        keep = min(PAST, s)
        h0 = jnp.zeros((bsz, 2 * D_STATE), F32)
        sc0 = jnp.zeros((bsz, SUBLANES, D_SCONV), F32)
        ff0 = jnp.zeros((bsz, SUBLANES, 2 * D_FF), F32)
    else:
        k_past, v_past, h_re0, h_im0, sconv_buf, ffn_buf = past
        keep = s
        pk = k_past.reshape(bsz, PAST, D_ATT).astype(BF16)
        pv = v_past.reshape(bsz, PAST, D_ATT).astype(BF16)
        h0 = jnp.concatenate([h_re0.reshape(bsz, D_STATE), h_im0.reshape(bsz, D_STATE)], axis=1).astype(F32)
        sc0 = _pad_rows(sconv_buf)
        ff0 = _pad_rows(ffn_buf)

    qkv, k_new, v_new, ssm_out, sconv_out, sc_new, h_new = _in_ssm_call(
        x, lp['w_in'], lp['sconv_w'], lp['sconv_b'], sc0, h0,
        lp['lbar'], lp['bblk'], lp['cblk'], lp['d'], lp['w_glu'], tt_in, keep)
    if past is None:
        att = _attn_prompt_call(qkv, _rel_bias_table(lp['rel_bias'], Q_STEP, CHUNK))
    else:
        att = _attn_sample_call(qkv, pk, pv, _rel_bias_table(lp['rel_bias'], s, s))
    x2, ff_new = _mix_ffn_call(att, ssm_out, sconv_out, x, lp['w_out'], lp['ln1_g'], lp['ln1_b'],
                               lp['w_ff_in'], lp['ffn_conv_w'], lp['ffn_conv_b'], ff0,
                               lp['w_ff_out'], lp['ln2_g'], lp['ln2_b'], nb_ffn, tt_ffn)
    new = (k_new.reshape(bsz, keep, N_HEADS, HEAD_DIM), v_new.reshape(bsz, keep, N_HEADS, HEAD_DIM),
           h_new[:, 0:D_STATE].reshape(bsz, N_SSM_GROUPS, SSM_STATE),
           h_new[:, D_STATE:].reshape(bsz, N_SSM_GROUPS, SSM_STATE),
           sc_new[:, SUBLANES - (CONV_K - 1):], ff_new[:, SUBLANES - (CONV_K - 1):])
    return x2, new


ROW_TILE = 512


def _tiles(bsz, s):
    tt_in = min(ROW_TILE // SUBLANES, s)
    tt_ffn = min(ROW_TILE, s)
    nb_ffn = min(bsz, ROW_TILE // tt_ffn)
    return tt_in, nb_ffn, tt_ffn


def kernel(x_prompt, x_sample, cache_k, cache_v, state_ssm_re, state_ssm_im, cache_sconv, cache_ffn_conv, ln_in_g, ln_in_b, w_in, rel_bias, ssm_lam_re, ssm_lam_im, ssm_log_dt, ssm_b_re, ssm_b_im, ssm_c_re, ssm_c_im, ssm_d, w_glu, sconv_w, sconv_b, w_out, ln1_g, ln1_b, w_ff_in, ffn_conv_w, ffn_conv_b, w_ff_out, ln2_g, ln2_b):
    bp, sp, _ = x_prompt.shape
    bs, ss, _ = x_sample.shape
    g_in = ln_in_g.reshape(1, D_MODEL)
    b_in = ln_in_b.reshape(1, D_MODEL)
    xp = _ln_call(x_prompt.reshape(bp * sp, D_MODEL), g_in, b_in, min(ROW_TILE, bp * sp)).reshape(bp, sp, D_MODEL)
    xs = _ln_call(x_sample.reshape(bs * ss, D_MODEL), g_in, b_in, min(ROW_TILE, bs * ss)).reshape(bs, ss, D_MODEL)
    st_p = [[] for _ in range(6)]
    st_s = [[] for _ in range(6)]
    for l in range(DEPTH):
        lbar, bblk, cblk = _ssm_params(ssm_lam_re[l], ssm_lam_im[l], ssm_log_dt[l], ssm_b_re[l], ssm_b_im[l],
                                       ssm_c_re[l], ssm_c_im[l])
        lp = {'w_in': w_in[l].astype(BF16), 'rel_bias': rel_bias[l],
              'lbar': lbar, 'bblk': bblk, 'cblk': cblk,
              'd': ssm_d[l].reshape(1, D_SSM), 'w_glu': w_glu[l].astype(BF16),
              'sconv_w': sconv_w[l], 'sconv_b': sconv_b[l].reshape(1, D_SCONV),
              'w_out': w_out[l].astype(BF16),
              'ln1_g': ln1_g[l].reshape(1, D_MODEL), 'ln1_b': ln1_b[l].reshape(1, D_MODEL),
              'w_ff_in': w_ff_in[l].astype(BF16), 'ffn_conv_w': ffn_conv_w[l],
              'ffn_conv_b': ffn_conv_b[l].reshape(1, 2 * D_FF),
              'w_ff_out': w_ff_out[l].astype(BF16),
              'ln2_g': ln2_g[l].reshape(1, D_MODEL), 'ln2_b': ln2_b[l].reshape(1, D_MODEL)}
        xp, new_p = _trunk_layer(xp, lp, None, _tiles(bp, sp))
        past = (cache_k[l], cache_v[l], state_ssm_re[l], state_ssm_im[l], cache_sconv[l], cache_ffn_conv[l])
        xs, new_s = _trunk_layer(xs, lp, past, _tiles(bs, ss))
        for i in range(6):
            st_p[i].append(new_p[i])
            st_s[i].append(new_s[i])
    k_p, v_p, hre_p, him_p, sc_p, ff_p = [jnp.stack(a, axis=0) for a in st_p]
    k_s, v_s, hre_s, him_s, sc_s, ff_s = [jnp.stack(a, axis=0) for a in st_s]
    return (xp, xs, k_p, v_p, k_s, v_s, hre_p, him_p, hre_s, him_s, sc_p, sc_s, ff_p, ff_s)
```

```python
import functools
import math

import jax
import jax.numpy as jnp
import numpy as np
from jax import lax
from jax.experimental import pallas as pl
from jax.experimental.pallas import tpu as pltpu

D_MODEL = 1024
DEPTH = 4
CHUNK = 64
N_HEADS = 8
HEAD_DIM = 64
D_ATT = N_HEADS * HEAD_DIM
N_PAST_CHUNKS = 8
PAST = N_PAST_CHUNKS * CHUNK
REL_MAX = 128
SSM_GROUP = 16
N_SSM_GROUPS = 16
D_SSM = SSM_GROUP * N_SSM_GROUPS
SSM_STATE = 64
D_STATE = N_SSM_GROUPS * SSM_STATE
D_SCONV = 256
CONV_K = 3
D_FF = 2048
D_QKV = 3 * D_ATT
D_REST = D_SSM + 3 * D_SCONV
ALPHA = (2 * DEPTH) ** 0.25
LN_EPS = 1e-5

SUBLANES = 8
VMEM_LIMIT = 56 * 1024 * 1024

F32 = jnp.float32
BF16 = jnp.bfloat16


def _cparams(sem):
    return pltpu.CompilerParams(dimension_semantics=sem, vmem_limit_bytes=VMEM_LIMIT)


def _layer_norm(x, g, b):
    mu = jnp.mean(x, axis=-1, keepdims=True)
    xc = x - mu
    var = jnp.mean(xc * xc, axis=-1, keepdims=True)
    return xc * lax.rsqrt(var + LN_EPS) * g + b


def _dot(a, b):
    return jnp.dot(a, b, preferred_element_type=F32)


def _causal_conv3(cur, hist_ref, w_ref, b_ref, nb, tt):
    outs = []
    for s in range(nb):
        v = cur[s * tt:(s + 1) * tt]
        ext = jnp.concatenate([hist_ref[s], v], axis=0)
        m1 = pltpu.roll(ext, 1, 0)[SUBLANES:]
        m2 = pltpu.roll(ext, 2, 0)[SUBLANES:]
        hist_ref[s] = v[tt - SUBLANES:tt]
        outs.append(w_ref[0:1, :] * m2 + w_ref[1:2, :] * m1 + w_ref[2:3, :] * v + b_ref[...])
    return outs[0] if nb == 1 else jnp.concatenate(outs, axis=0)


def _ln_kernel(x_ref, g_ref, b_ref, o_ref):
    o_ref[...] = _layer_norm(x_ref[...], g_ref[...], b_ref[...])


def _ln_call(x2d, g, b, tm):
    rows = x2d.shape[0]
    return pl.pallas_call(
        _ln_kernel,
        grid=(rows // tm,),
        in_specs=[pl.BlockSpec((tm, D_MODEL), lambda i: (i, 0)),
                  pl.BlockSpec((1, D_MODEL), lambda i: (0, 0)),
                  pl.BlockSpec((1, D_MODEL), lambda i: (0, 0))],
        out_specs=pl.BlockSpec((tm, D_MODEL), lambda i: (i, 0)),
        out_shape=jax.ShapeDtypeStruct((rows, D_MODEL), F32),
        compiler_params=_cparams(("parallel",)),
        name="ln_in",
    )(x2d, g, b)


def _gelu_tanh(x):
    c = math.sqrt(2.0 / math.pi)
    return 0.5 * x * (1.0 + jnp.tanh(c * (x + 0.044715 * (x * x * x))))


def _in_ssm_kernel(x_ref, w_ref, scw_ref, scb_ref, sc0_ref, h0_ref,
                   lbar_ref, bblk_ref, cblk_ref, d_ref, wglu_ref,
                   qkv_ref, kt_ref, vt_ref, ssm_ref, sc_ref, scn_ref, hn_ref,
                   hist_ref, h_ref, st_ref, *, tt):
    ti = pl.program_id(1)
    nb = SUBLANES
    rows = nb * tt

    @pl.when(ti == 0)
    def _():
        hist_ref[...] = sc0_ref[...]
        h_ref[...] = h0_ref[...]

    xb = x_ref[...].reshape(rows, D_MODEL).astype(BF16)
    u = _dot(xb, w_ref[:, D_QKV:D_QKV + D_SSM])
    ut = jnp.swapaxes(u.reshape(nb, tt, D_SSM), 0, 1).reshape(rows, D_SSM)
    bu = _dot(ut.astype(BF16), bblk_ref[...])

    q = _dot(xb, w_ref[:, 0:D_ATT]) * (HEAD_DIM ** -0.5)
    qkv_ref[:, :, 0:D_ATT] = q.astype(BF16).reshape(nb, tt, D_ATT)
    for tail_ref, lo in ((kt_ref, D_ATT), (vt_ref, 2 * D_ATT)):
        kv = _dot(xb, w_ref[:, lo:lo + D_ATT])
        qkv_ref[:, :, lo:lo + D_ATT] = kv.astype(BF16).reshape(nb, tt, D_ATT)
        tail_ref[...] = kv.reshape(nb, tt, D_ATT)

    g = _dot(xb, w_ref[:, D_QKV + D_SSM:D_QKV + D_REST])
    gate_b = g[:, 0:D_SCONV]
    gate_c = g[:, D_SCONV:2 * D_SCONV]
    xv = g[:, 2 * D_SCONV:3 * D_SCONV]
    conv = _causal_conv3(gate_c * xv, hist_ref, scw_ref, scb_ref, nb, tt)
    sc_ref[...] = (gate_b * conv).astype(BF16).reshape(nb, tt, D_SCONV)
    scn_ref[...] = hist_ref[...]

    lr = jnp.broadcast_to(lbar_ref[0:1, :], (SUBLANES, D_STATE))
    li = jnp.broadcast_to(lbar_ref[1:2, :], (SUBLANES, D_STATE))
    hr = h_ref[:, 0:D_STATE]
    hi = h_ref[:, D_STATE:2 * D_STATE]
    for t in range(tt):
        r = slice(t * SUBLANES, (t + 1) * SUBLANES)
        hr, hi = (lr * hr - li * hi + bu[r, 0:D_STATE],
                  lr * hi + li * hr + bu[r, D_STATE:2 * D_STATE])
        st_ref[r, 0:D_STATE] = hr
        st_ref[r, D_STATE:2 * D_STATE] = hi
    y = _dot(st_ref[...].astype(BF16), cblk_ref[...]) + d_ref[...] * ut
    z = _gelu_tanh(y)
    gate = jax.nn.sigmoid(_dot(z.astype(BF16), wglu_ref[...]))
    h_ref[:, 0:D_STATE] = hr
    h_ref[:, D_STATE:2 * D_STATE] = hi
    hn_ref[:, 0:D_STATE] = hr
    hn_ref[:, D_STATE:2 * D_STATE] = hi
    out = (z * gate).reshape(tt, nb, D_SSM)
    ssm_ref[...] = jnp.swapaxes(out, 0, 1).astype(BF16)


def _in_ssm_call(x, w_in, scw, scb, sc0, h0, lbar, bblk, cblk, d, wglu, tt, keep):
    bsz, s, _ = x.shape
    nb = SUBLANES
    nt = s // tt
    tail_first = nt - keep // tt
    blk = lambda b, t: (b, t, 0)
    tail = lambda b, t: (b, jnp.maximum(t - tail_first, 0), 0)
    bonly3 = lambda b, t: (b, 0, 0)
    bonly2 = lambda b, t: (b, 0)
    const2 = lambda b, t: (0, 0)
    return pl.pallas_call(
        functools.partial(_in_ssm_kernel, tt=tt),
        grid=(bsz // nb, nt),
        in_specs=[pl.BlockSpec((nb, tt, D_MODEL), blk),
                  pl.BlockSpec((D_MODEL, D_QKV + D_REST), const2),
                  pl.BlockSpec((CONV_K, D_SCONV), const2),
                  pl.BlockSpec((1, D_SCONV), const2),
                  pl.BlockSpec((nb, SUBLANES, D_SCONV), bonly3),
                  pl.BlockSpec((nb, 2 * D_STATE), bonly2),
                  pl.BlockSpec((2, D_STATE), const2),
                  pl.BlockSpec((D_SSM, 2 * D_STATE), const2),
                  pl.BlockSpec((2 * D_STATE, D_SSM), const2),
                  pl.BlockSpec((1, D_SSM), const2),
                  pl.BlockSpec((D_SSM, D_SSM), const2)],
        out_specs=[pl.BlockSpec((nb, tt, D_QKV), blk),
                   pl.BlockSpec((nb, tt, D_ATT), tail),
                   pl.BlockSpec((nb, tt, D_ATT), tail),
                   pl.BlockSpec((nb, tt, D_SSM), blk),
                   pl.BlockSpec((nb, tt, D_SCONV), blk),
                   pl.BlockSpec((nb, SUBLANES, D_SCONV), bonly3),
                   pl.BlockSpec((nb, 2 * D_STATE), bonly2)],
        out_shape=[jax.ShapeDtypeStruct((bsz, s, D_QKV), BF16),
                   jax.ShapeDtypeStruct((bsz, keep, D_ATT), F32),
                   jax.ShapeDtypeStruct((bsz, keep, D_ATT), F32),
                   jax.ShapeDtypeStruct((bsz, s, D_SSM), BF16),
                   jax.ShapeDtypeStruct((bsz, s, D_SCONV), BF16),
                   jax.ShapeDtypeStruct((bsz, SUBLANES, D_SCONV), F32),
                   jax.ShapeDtypeStruct((bsz, 2 * D_STATE), F32)],
        scratch_shapes=[pltpu.VMEM((nb, SUBLANES, D_SCONV), F32),
                        pltpu.VMEM((nb, 2 * D_STATE), F32),
                        pltpu.VMEM((nb * tt, 2 * D_STATE), F32)],
        compiler_params=_cparams(("parallel", "arbitrary")),
        name="in_ssm",
    )(x, w_in, scw, scb, sc0, h0, lbar, bblk, cblk, d, wglu)


HEADS_PER_GROUP = 4
GROUP_W = HEADS_PER_GROUP * HEAD_DIM
N_GROUPS = N_HEADS // HEADS_PER_GROUP
Q_STEP = 4 * CHUNK
K_WIN = PAST + Q_STEP


def _softmax_unnormalised(sc):
    m = jnp.max(sc, axis=-1, keepdims=True)
    e = jnp.exp(sc - m)
    return e.astype(BF16), jnp.sum(e, axis=-1, keepdims=True)


def _group_head_of_lane(shape):
    lane = lax.broadcasted_iota(jnp.int32, shape, len(shape) - 1)
    return (lane // HEAD_DIM) % HEADS_PER_GROUP


def _attn_group(q4, k4, v_of_head, bias_of_head):
    head = _group_head_of_lane(q4.shape)
    acc = None
    scale = None
    for h in range(HEADS_PER_GROUP):
        qm = jnp.where(head == h, q4, jnp.zeros_like(q4))
        sc = lax.dot_general(qm, k4, (((1,), (1,)), ((), ())), preferred_element_type=F32)
        e, l = _softmax_unnormalised(sc + bias_of_head(h))
        pv = _dot(e, v_of_head(h))
        acc = pv if acc is None else acc + pv
        inv = 1.0 / l
        scale = inv if scale is None else jnp.where(head == h, inv, scale)
    return acc * scale


def _attn_prompt_kernel(qkv_ref, bias_ref, o_ref, vmask_ref, *, s):
    v = qkv_ref[:, 2 * D_ATT:3 * D_ATT]
    head = _group_head_of_lane(v.shape)
    for h in range(HEADS_PER_GROUP):
        vmask_ref[h] = jnp.where(head == h, v, jnp.zeros_like(v))

    def step(r0, k0, nk):
        for g in range(N_GROUPS):
            cols = slice(g * GROUP_W, (g + 1) * GROUP_W)
            kcols = slice(D_ATT + g * GROUP_W, D_ATT + (g + 1) * GROUP_W)
            out = _attn_group(
                qkv_ref[pl.ds(r0, Q_STEP), cols],
                qkv_ref[pl.ds(k0, nk), kcols],
                lambda h: vmask_ref[h, pl.ds(k0, nk), cols],
                lambda h: bias_ref[g * HEADS_PER_GROUP + h, :, K_WIN - nk:K_WIN])
            o_ref[pl.ds(r0, Q_STEP), cols] = out.astype(BF16)

    n_steps = s // Q_STEP
    n_head_steps = min(PAST // Q_STEP, n_steps)
    for j in range(n_head_steps):
        step(j * Q_STEP, 0, (j + 1) * Q_STEP)

    def body(j, carry):
        r0 = pl.multiple_of(j * Q_STEP, Q_STEP)
        step(r0, pl.multiple_of(r0 - PAST, Q_STEP), K_WIN)
        return carry

    if n_steps > n_head_steps:
        lax.fori_loop(n_head_steps, n_steps, body, 0)


def _attn_prompt_call(qkv, bias):
    bsz, s, _ = qkv.shape
    assert s % Q_STEP == 0
    bsel = lambda b: (b, 0, 0)
    return pl.pallas_call(
        functools.partial(_attn_prompt_kernel, s=s),
        grid=(bsz,),
        in_specs=[pl.BlockSpec((None, s, D_QKV), bsel),
                  pl.BlockSpec((N_HEADS, Q_STEP, K_WIN), lambda b: (0, 0, 0))],
        out_specs=pl.BlockSpec((None, s, D_ATT), bsel),
        out_shape=jax.ShapeDtypeStruct((bsz, s, D_ATT), BF16),
        scratch_shapes=[pltpu.VMEM((HEADS_PER_GROUP, s, D_ATT), BF16)],
        compiler_params=_cparams(("parallel",)),
        name="attention_prompt",
    )(qkv, bias)


def _attn_sample_kernel(qkv_ref, pk_ref, pv_ref, bias_ref, o_ref, k_ref, vmask_ref, *, s):
    k_ref[0:PAST, :] = pk_ref[...]
    k_ref[PAST:PAST + s, :] = qkv_ref[:, D_ATT:2 * D_ATT]
    head = _group_head_of_lane((PAST, D_ATT))
    head_new = _group_head_of_lane((s, D_ATT))
    for h in range(HEADS_PER_GROUP):
        vmask_ref[h, 0:PAST, :] = jnp.where(head == h, pv_ref[...], jnp.zeros_like(pv_ref[...]))
        v = qkv_ref[:, 2 * D_ATT:3 * D_ATT]
        vmask_ref[h, PAST:PAST + s, :] = jnp.where(head_new == h, v, jnp.zeros_like(v))
    for g in range(N_GROUPS):
        cols = slice(g * GROUP_W, (g + 1) * GROUP_W)
        out = _attn_group(qkv_ref[:, cols], k_ref[:, cols],
                          lambda h: vmask_ref[h, :, cols],
                          lambda h: bias_ref[g * HEADS_PER_GROUP + h])
        o_ref[:, cols] = out.astype(BF16)


def _attn_sample_call(qkv, pk, pv, bias):
    bsz, s, _ = qkv.shape
    band = PAST + s
    bsel = lambda b: (b, 0, 0)
    return pl.pallas_call(
        functools.partial(_attn_sample_kernel, s=s),
        grid=(bsz,),
        in_specs=[pl.BlockSpec((None, s, D_QKV), bsel),
                  pl.BlockSpec((None, PAST, D_ATT), bsel),
                  pl.BlockSpec((None, PAST, D_ATT), bsel),
                  pl.BlockSpec((N_HEADS, s, band), lambda b: (0, 0, 0))],
        out_specs=pl.BlockSpec((None, s, D_ATT), bsel),
        out_shape=jax.ShapeDtypeStruct((bsz, s, D_ATT), BF16),
        scratch_shapes=[pltpu.VMEM((band, D_ATT), BF16),
                        pltpu.VMEM((HEADS_PER_GROUP, band, D_ATT), BF16)],
        compiler_params=_cparams(("parallel",)),
        name="attention_sample",
    )(qkv, pk, pv, bias)


def _mix_ffn_kernel(att_ref, ssm_ref, sc_ref, x_ref, wo_ref, g1_ref, b1_ref,
                    w1_ref, cw_ref, cb_ref, c0_ref, w2_ref, g2_ref, b2_ref,
                    o_ref, cn_ref, hist_ref, *, nb, tt):
    ti = pl.program_id(1)
    rows = nb * tt

    @pl.when(ti == 0)
    def _():
        hist_ref[...] = c0_ref[...]

    mix = _dot(att_ref[...].reshape(rows, D_ATT), wo_ref[0:D_ATT, :])
    mix = mix + _dot(ssm_ref[...].reshape(rows, D_SSM), wo_ref[D_ATT:D_ATT + D_SSM, :])
    mix = mix + _dot(sc_ref[...].reshape(rows, D_SCONV), wo_ref[D_ATT + D_SSM:D_MODEL, :])
    x1 = _layer_norm(ALPHA * x_ref[...].reshape(rows, D_MODEL) + mix, g1_ref[...], b1_ref[...])

    up = _dot(x1.astype(BF16), w1_ref[...])
    conv = _causal_conv3(up, hist_ref, cw_ref, cb_ref, nb, tt)
    cn_ref[...] = hist_ref[...]
    gate = conv[:, 0:D_FF]
    val = conv[:, D_FF:2 * D_FF]
    h = (gate * jax.nn.sigmoid(gate) * val).astype(BF16)
    y = _dot(h, w2_ref[...])
    o_ref[...] = _layer_norm(ALPHA * x1 + y, g2_ref[...], b2_ref[...]).reshape(nb, tt, D_MODEL)


def _mix_ffn_call(att, ssm, sc, x, w_out, g1, b1, w1, cw, cb, c0, w2, g2, b2, nb, tt):
    bsz, s, _ = x.shape
    blk = lambda b, t: (b, t, 0)
    bonly = lambda b, t: (b, 0, 0)
    const2 = lambda b, t: (0, 0)
    vec = pl.BlockSpec((1, D_MODEL), const2)
    return pl.pallas_call(
        functools.partial(_mix_ffn_kernel, nb=nb, tt=tt),
        grid=(bsz // nb, s // tt),
        in_specs=[pl.BlockSpec((nb, tt, D_ATT), blk),
                  pl.BlockSpec((nb, tt, D_SSM), blk),
                  pl.BlockSpec((nb, tt, D_SCONV), blk),
                  pl.BlockSpec((nb, tt, D_MODEL), blk),
                  pl.BlockSpec((D_MODEL, D_MODEL), const2), vec, vec,
                  pl.BlockSpec((D_MODEL, 2 * D_FF), const2),
                  pl.BlockSpec((CONV_K, 2 * D_FF), const2),
                  pl.BlockSpec((1, 2 * D_FF), const2),
                  pl.BlockSpec((nb, SUBLANES, 2 * D_FF), bonly),
                  pl.BlockSpec((D_FF, D_MODEL), const2), vec, vec],
        out_specs=[pl.BlockSpec((nb, tt, D_MODEL), blk),
                   pl.BlockSpec((nb, SUBLANES, 2 * D_FF), bonly)],
        out_shape=[jax.ShapeDtypeStruct((bsz, s, D_MODEL), F32),
                   jax.ShapeDtypeStruct((bsz, SUBLANES, 2 * D_FF), F32)],
        scratch_shapes=[pltpu.VMEM((nb, SUBLANES, 2 * D_FF), F32)],
        compiler_params=_cparams(("parallel", "arbitrary")),
        name="mix_ffn",
    )(att, ssm, sc, x, w_out, g1, b1, w1, cw, cb, c0, w2, g2, b2)


def _rel_bias_table(rel_bias, tq, chunk):
    nk = PAST + tq
    period = nk + tq
    dist = (nk - 1) - jnp.arange(period)
    by_lag = rel_bias[:, jnp.clip(dist, -(CHUNK - 1), REL_MAX) + (CHUNK - 1)].astype(F32)
    rolled = jnp.roll(by_lag, -(tq - 1), axis=-1)
    n_heads = rel_bias.shape[0]
    bias = jnp.tile(rolled, (1, tq))[:, :tq * (period - 1)].reshape(n_heads, tq, period - 1)[:, :, :nk]
    if tq == chunk:
        return bias
    t_idx = np.arange(tq)[:, None]
    s_idx = np.arange(nk)[None, :]
    key_chunk = s_idx // chunk - t_idx // chunk
    in_band = (key_chunk >= 0) & (key_chunk <= N_PAST_CHUNKS)
    return jnp.where(in_band[None], bias, -jnp.inf)


def _ssm_params(lam_re, lam_im, log_dt, b_re, b_im, c_re, c_im):
    lam = lax.complex(lam_re.astype(F32), lam_im.astype(F32))
    dt = jnp.exp(log_dt.astype(F32))[:, None]
    lbar = jnp.exp(lam * dt)
    bbar = ((lbar - 1.0) / lam)[:, :, None] * lax.complex(b_re.astype(F32), b_im.astype(F32))
    eye = jnp.eye(N_SSM_GROUPS, dtype=F32)

    def in_map(m):
        return jnp.einsum('gpc,gh->gchp', m, eye).reshape(D_SSM, D_STATE)

    def out_map(m):
        return jnp.einsum('gcp,gh->gphc', m, eye).reshape(D_STATE, D_SSM)

    bblk = jnp.concatenate([in_map(bbar.real), in_map(bbar.imag)], axis=1).astype(BF16)
    cblk = jnp.concatenate([out_map(c_re.astype(F32)), out_map(-c_im.astype(F32))], axis=0).astype(BF16)
    lbar2 = jnp.stack([lbar.real.reshape(D_STATE), lbar.imag.reshape(D_STATE)], axis=0)
    return lbar2, bblk, cblk


def _pad_rows(buf):
    return jnp.pad(buf.astype(F32), ((0, 0), (SUBLANES - (CONV_K - 1), 0), (0, 0)))


def _trunk_layer(x, lp, past, tiles):
    bsz, s, _ = x.shape
    tt_in, nb_ffn, tt_ffn = tiles
    if past is None:
        keep = min(PAST, s)
        h0 = jnp.zeros((bsz, 2 * D_STATE), F32)
        sc0 = jnp.zeros((bsz, SUBLANES, D_SCONV), F32)
        ff0 = jnp.zeros((bsz, SUBLANES, 2 * D_FF), F32)
    else:
        k_past, v_past, h_re0, h_im0, sconv_buf, ffn_buf = past
        keep = s
        pk = k_past.reshape(bsz, PAST, D_ATT).astype(BF16)
        pv = v_past.reshape(bsz, PAST, D_ATT).astype(BF16)
        h0 = jnp.concatenate([h_re0.reshape(bsz, D_STATE), h_im0.reshape(bsz, D_STATE)], axis=1).astype(F32)
        sc0 = _pad_rows(sconv_buf)
        ff0 = _pad_rows(ffn_buf)

    qkv, k_new, v_new, ssm_out, sconv_out, sc_new, h_new = _in_ssm_call(
        x, lp['w_in'], lp['sconv_w'], lp['sconv_b'], sc0, h0,
        lp['lbar'], lp['bblk'], lp['cblk'], lp['d'], lp['w_glu'], tt_in, keep)
    if past is None:
        att = _attn_prompt_call(qkv, _rel_bias_table(lp['rel_bias'], Q_STEP, CHUNK))
    else:
        att = _attn_sample_call(qkv, pk, pv, _rel_bias_table(lp['rel_bias'], s, s))
    x2, ff_new = _mix_ffn_call(att, ssm_out, sconv_out, x, lp['w_out'], lp['ln1_g'], lp['ln1_b'],
                               lp['w_ff_in'], lp['ffn_conv_w'], lp['ffn_conv_b'], ff0,
                               lp['w_ff_out'], lp['ln2_g'], lp['ln2_b'], nb_ffn, tt_ffn)
    new = (k_new.reshape(bsz, keep, N_HEADS, HEAD_DIM), v_new.reshape(bsz, keep, N_HEADS, HEAD_DIM),
           h_new[:, 0:D_STATE].reshape(bsz, N_SSM_GROUPS, SSM_STATE),
           h_new[:, D_STATE:].reshape(bsz, N_SSM_GROUPS, SSM_STATE),
           sc_new[:, SUBLANES - (CONV_K - 1):], ff_new[:, SUBLANES - (CONV_K - 1):])
    return x2, new


ROW_TILE = 512
IN_SSM_POSITIONS = 128


def _tiles(bsz, s):
    tt_in = min(IN_SSM_POSITIONS, s)
    tt_ffn = min(ROW_TILE, s)
    nb_ffn = min(bsz, ROW_TILE // tt_ffn)
    return tt_in, nb_ffn, tt_ffn


def kernel(x_prompt, x_sample, cache_k, cache_v, state_ssm_re, state_ssm_im, cache_sconv, cache_ffn_conv, ln_in_g, ln_in_b, w_in, rel_bias, ssm_lam_re, ssm_lam_im, ssm_log_dt, ssm_b_re, ssm_b_im, ssm_c_re, ssm_c_im, ssm_d, w_glu, sconv_w, sconv_b, w_out, ln1_g, ln1_b, w_ff_in, ffn_conv_w, ffn_conv_b, w_ff_out, ln2_g, ln2_b):
    bp, sp, _ = x_prompt.shape
    bs, ss, _ = x_sample.shape
    g_in = ln_in_g.reshape(1, D_MODEL)
    b_in = ln_in_b.reshape(1, D_MODEL)
    xp = _ln_call(x_prompt.reshape(bp * sp, D_MODEL), g_in, b_in, min(ROW_TILE, bp * sp)).reshape(bp, sp, D_MODEL)
    xs = _ln_call(x_sample.reshape(bs * ss, D_MODEL), g_in, b_in, min(ROW_TILE, bs * ss)).reshape(bs, ss, D_MODEL)
    st_p = [[] for _ in range(6)]
    st_s = [[] for _ in range(6)]
    for l in range(DEPTH):
        lbar, bblk, cblk = _ssm_params(ssm_lam_re[l], ssm_lam_im[l], ssm_log_dt[l], ssm_b_re[l], ssm_b_im[l],
                                       ssm_c_re[l], ssm_c_im[l])
        lp = {'w_in': w_in[l].astype(BF16), 'rel_bias': rel_bias[l],
              'lbar': lbar, 'bblk': bblk, 'cblk': cblk,
              'd': ssm_d[l].reshape(1, D_SSM), 'w_glu': w_glu[l].astype(BF16),
              'sconv_w': sconv_w[l], 'sconv_b': sconv_b[l].reshape(1, D_SCONV),
              'w_out': w_out[l].astype(BF16),
              'ln1_g': ln1_g[l].reshape(1, D_MODEL), 'ln1_b': ln1_b[l].reshape(1, D_MODEL),
              'w_ff_in': w_ff_in[l].astype(BF16), 'ffn_conv_w': ffn_conv_w[l],
              'ffn_conv_b': ffn_conv_b[l].reshape(1, 2 * D_FF),
              'w_ff_out': w_ff_out[l].astype(BF16),
              'ln2_g': ln2_g[l].reshape(1, D_MODEL), 'ln2_b': ln2_b[l].reshape(1, D_MODEL)}
        xp, new_p = _trunk_layer(xp, lp, None, _tiles(bp, sp))
        past = (cache_k[l], cache_v[l], state_ssm_re[l], state_ssm_im[l], cache_sconv[l], cache_ffn_conv[l])
        xs, new_s = _trunk_layer(xs, lp, past, _tiles(bs, ss))
        for i in range(6):
            st_p[i].append(new_p[i])
            st_s[i].append(new_s[i])
    k_p, v_p, hre_p, him_p, sc_p, ff_p = [jnp.stack(a, axis=0) for a in st_p]
    k_s, v_s, hre_s, him_s, sc_s, ff_s = [jnp.stack(a, axis=0) for a in st_s]
    return (xp, xs, k_p, v_p, k_s, v_s, hre_p, him_p, hre_s, him_s, sc_p, sc_s, ff_p, ff_s)
```

```python
import functools
import math

import jax
import jax.numpy as jnp
import numpy as np
from jax import lax
from jax.experimental import pallas as pl
from jax.experimental.pallas import tpu as pltpu

D_MODEL = 1024
DEPTH = 4
CHUNK = 64
N_HEADS = 8
HEAD_DIM = 64
D_ATT = N_HEADS * HEAD_DIM
N_PAST_CHUNKS = 8
PAST = N_PAST_CHUNKS * CHUNK
REL_MAX = 128
SSM_GROUP = 16
N_SSM_GROUPS = 16
D_SSM = SSM_GROUP * N_SSM_GROUPS
SSM_STATE = 64
D_STATE = N_SSM_GROUPS * SSM_STATE
D_SCONV = 256
CONV_K = 3
D_FF = 2048
D_QKV = 3 * D_ATT
D_REST = D_SSM + 3 * D_SCONV
ALPHA = (2 * DEPTH) ** 0.25
LN_EPS = 1e-5

SUBLANES = 8
VMEM_LIMIT = 56 * 1024 * 1024

F32 = jnp.float32
BF16 = jnp.bfloat16


def _cparams(sem):
    return pltpu.CompilerParams(dimension_semantics=sem, vmem_limit_bytes=VMEM_LIMIT)


def _layer_norm(x, g, b):
    mu = jnp.mean(x, axis=-1, keepdims=True)
    xc = x - mu
    var = jnp.mean(xc * xc, axis=-1, keepdims=True)
    return xc * lax.rsqrt(var + LN_EPS) * g + b


def _dot(a, b):
    return jnp.dot(a, b, preferred_element_type=F32)


def _causal_conv3(cur, hist_ref, w_ref, b_ref, nb, tt):
    outs = []
    for s in range(nb):
        v = cur[s * tt:(s + 1) * tt]
        ext = jnp.concatenate([hist_ref[s], v], axis=0)
        m1 = pltpu.roll(ext, 1, 0)[SUBLANES:]
        m2 = pltpu.roll(ext, 2, 0)[SUBLANES:]
        hist_ref[s] = v[tt - SUBLANES:tt]
        outs.append(w_ref[0:1, :] * m2 + w_ref[1:2, :] * m1 + w_ref[2:3, :] * v + b_ref[...])
    return outs[0] if nb == 1 else jnp.concatenate(outs, axis=0)


def _gelu_tanh(x):
    c = math.sqrt(2.0 / math.pi)
    return 0.5 * x * (1.0 + jnp.tanh(c * (x + 0.044715 * (x * x * x))))


def _in_ssm_kernel(x_ref, gin_ref, bin_ref, w_ref, scw_ref, scb_ref, sc0_ref, h0_ref,
                   lbar_ref, bblk_ref, cblk_ref, d_ref, wglu_ref,
                   qkv_ref, kt_ref, vt_ref, ssm_ref, sc_ref, scn_ref, hn_ref,
                   hist_ref, h_ref, st_ref, *, tt, ln_in):
    ti = pl.program_id(1)
    nb = SUBLANES
    rows = nb * tt

    @pl.when(ti == 0)
    def _():
        hist_ref[...] = sc0_ref[...]
        h_ref[...] = h0_ref[...]

    x = x_ref[...].reshape(rows, D_MODEL)
    if ln_in:
        x = _layer_norm(x, gin_ref[...], bin_ref[...])
    xb = x.astype(BF16)
    u = _dot(xb, w_ref[:, D_QKV:D_QKV + D_SSM])
    ut = jnp.swapaxes(u.reshape(nb, tt, D_SSM), 0, 1).reshape(rows, D_SSM)
    bu = _dot(ut.astype(BF16), bblk_ref[...])

    q = _dot(xb, w_ref[:, 0:D_ATT]) * (HEAD_DIM ** -0.5)
    qkv_ref[:, :, 0:D_ATT] = q.astype(BF16).reshape(nb, tt, D_ATT)
    for tail_ref, lo in ((kt_ref, D_ATT), (vt_ref, 2 * D_ATT)):
        kv = _dot(xb, w_ref[:, lo:lo + D_ATT])
        qkv_ref[:, :, lo:lo + D_ATT] = kv.astype(BF16).reshape(nb, tt, D_ATT)
        tail_ref[...] = kv.reshape(nb, tt, D_ATT)

    g = _dot(xb, w_ref[:, D_QKV + D_SSM:D_QKV + D_REST])
    gate_b = g[:, 0:D_SCONV]
    gate_c = g[:, D_SCONV:2 * D_SCONV]
    xv = g[:, 2 * D_SCONV:3 * D_SCONV]
    conv = _causal_conv3(gate_c * xv, hist_ref, scw_ref, scb_ref, nb, tt)
    sc_ref[...] = (gate_b * conv).astype(BF16).reshape(nb, tt, D_SCONV)
    scn_ref[...] = hist_ref[...]

    lr = jnp.broadcast_to(lbar_ref[0:1, :], (SUBLANES, D_STATE))
    li = jnp.broadcast_to(lbar_ref[1:2, :], (SUBLANES, D_STATE))
    hr = h_ref[:, 0:D_STATE]
    hi = h_ref[:, D_STATE:2 * D_STATE]
    for t in range(tt):
        r = slice(t * SUBLANES, (t + 1) * SUBLANES)
        hr, hi = (lr * hr - li * hi + bu[r, 0:D_STATE],
                  lr * hi + li * hr + bu[r, D_STATE:2 * D_STATE])
        st_ref[r, 0:D_STATE] = hr
        st_ref[r, D_STATE:2 * D_STATE] = hi
    y = _dot(st_ref[...].astype(BF16), cblk_ref[...]) + d_ref[...] * ut
    z = _gelu_tanh(y)
    gate = jax.nn.sigmoid(_dot(z.astype(BF16), wglu_ref[...]))
    h_ref[:, 0:D_STATE] = hr
    h_ref[:, D_STATE:2 * D_STATE] = hi
    hn_ref[:, 0:D_STATE] = hr
    hn_ref[:, D_STATE:2 * D_STATE] = hi
    out = (z * gate).reshape(tt, nb, D_SSM)
    ssm_ref[...] = jnp.swapaxes(out, 0, 1).astype(BF16)


def _in_ssm_call(x, ln_in, w_in, scw, scb, sc0, h0, lbar, bblk, cblk, d, wglu, tt, keep):
    apply_ln, g_in, b_in = ln_in
    bsz, s, _ = x.shape
    nb = SUBLANES
    nt = s // tt
    tail_first = nt - keep // tt
    blk = lambda b, t: (b, t, 0)
    tail = lambda b, t: (b, jnp.maximum(t - tail_first, 0), 0)
    bonly3 = lambda b, t: (b, 0, 0)
    bonly2 = lambda b, t: (b, 0)
    const2 = lambda b, t: (0, 0)
    return pl.pallas_call(
        functools.partial(_in_ssm_kernel, tt=tt, ln_in=apply_ln),
        grid=(bsz // nb, nt),
        in_specs=[pl.BlockSpec((nb, tt, D_MODEL), blk),
                  pl.BlockSpec((1, D_MODEL), const2),
                  pl.BlockSpec((1, D_MODEL), const2),
                  pl.BlockSpec((D_MODEL, D_QKV + D_REST), const2),
                  pl.BlockSpec((CONV_K, D_SCONV), const2),
                  pl.BlockSpec((1, D_SCONV), const2),
                  pl.BlockSpec((nb, SUBLANES, D_SCONV), bonly3),
                  pl.BlockSpec((nb, 2 * D_STATE), bonly2),
                  pl.BlockSpec((2, D_STATE), const2),
                  pl.BlockSpec((D_SSM, 2 * D_STATE), const2),
                  pl.BlockSpec((2 * D_STATE, D_SSM), const2),
                  pl.BlockSpec((1, D_SSM), const2),
                  pl.BlockSpec((D_SSM, D_SSM), const2)],
        out_specs=[pl.BlockSpec((nb, tt, D_QKV), blk),
                   pl.BlockSpec((nb, tt, D_ATT), tail),
                   pl.BlockSpec((nb, tt, D_ATT), tail),
                   pl.BlockSpec((nb, tt, D_SSM), blk),
                   pl.BlockSpec((nb, tt, D_SCONV), blk),
                   pl.BlockSpec((nb, SUBLANES, D_SCONV), bonly3),
                   pl.BlockSpec((nb, 2 * D_STATE), bonly2)],
        out_shape=[jax.ShapeDtypeStruct((bsz, s, D_QKV), BF16),
                   jax.ShapeDtypeStruct((bsz, keep, D_ATT), F32),
                   jax.ShapeDtypeStruct((bsz, keep, D_ATT), F32),
                   jax.ShapeDtypeStruct((bsz, s, D_SSM), BF16),
                   jax.ShapeDtypeStruct((bsz, s, D_SCONV), BF16),
                   jax.ShapeDtypeStruct((bsz, SUBLANES, D_SCONV), F32),
                   jax.ShapeDtypeStruct((bsz, 2 * D_STATE), F32)],
        scratch_shapes=[pltpu.VMEM((nb, SUBLANES, D_SCONV), F32),
                        pltpu.VMEM((nb, 2 * D_STATE), F32),
                        pltpu.VMEM((nb * tt, 2 * D_STATE), F32)],
        compiler_params=_cparams(("parallel", "arbitrary")),
        name="in_ssm",
    )(x, g_in, b_in, w_in, scw, scb, sc0, h0, lbar, bblk, cblk, d, wglu)


HEADS_PER_GROUP = 4
GROUP_W = HEADS_PER_GROUP * HEAD_DIM
N_GROUPS = N_HEADS // HEADS_PER_GROUP
Q_STEP = 4 * CHUNK
K_WIN = PAST + Q_STEP


def _softmax_unnormalised(sc):
    m = jnp.max(sc, axis=-1, keepdims=True)
    e = jnp.exp(sc - m)
    return e.astype(BF16), jnp.sum(e, axis=-1, keepdims=True)


def _group_head_of_lane(shape):
    lane = lax.broadcasted_iota(jnp.int32, shape, len(shape) - 1)
    return (lane // HEAD_DIM) % HEADS_PER_GROUP


def _attn_group(q4, k4, v_of_head, bias_of_head):
    head = _group_head_of_lane(q4.shape)
    acc = None
    scale = None
    for h in range(HEADS_PER_GROUP):
        qm = jnp.where(head == h, q4, jnp.zeros_like(q4))
        sc = lax.dot_general(qm, k4, (((1,), (1,)), ((), ())), preferred_element_type=F32)
        e, l = _softmax_unnormalised(sc + bias_of_head(h))
        pv = _dot(e, v_of_head(h))
        acc = pv if acc is None else acc + pv
        inv = 1.0 / l
        scale = inv if scale is None else jnp.where(head == h, inv, scale)
    return acc * scale


def _attn_prompt_kernel(qkv_ref, bias_ref, o_ref, vmask_ref, *, s):
    v = qkv_ref[:, 2 * D_ATT:3 * D_ATT]
    head = _group_head_of_lane(v.shape)
    for h in range(HEADS_PER_GROUP):
        vmask_ref[h] = jnp.where(head == h, v, jnp.zeros_like(v))

    def step(r0, k0, nk):
        for g in range(N_GROUPS):
            cols = slice(g * GROUP_W, (g + 1) * GROUP_W)
            kcols = slice(D_ATT + g * GROUP_W, D_ATT + (g + 1) * GROUP_W)
            out = _attn_group(
                qkv_ref[pl.ds(r0, Q_STEP), cols],
                qkv_ref[pl.ds(k0, nk), kcols],
                lambda h: vmask_ref[h, pl.ds(k0, nk), cols],
                lambda h: bias_ref[g * HEADS_PER_GROUP + h, :, K_WIN - nk:K_WIN])
            o_ref[pl.ds(r0, Q_STEP), cols] = out.astype(BF16)

    n_steps = s // Q_STEP
    n_head_steps = min(PAST // Q_STEP, n_steps)
    for j in range(n_head_steps):
        step(j * Q_STEP, 0, (j + 1) * Q_STEP)

    def body(j, carry):
        r0 = pl.multiple_of(j * Q_STEP, Q_STEP)
        step(r0, pl.multiple_of(r0 - PAST, Q_STEP), K_WIN)
        return carry

    if n_steps > n_head_steps:
        lax.fori_loop(n_head_steps, n_steps, body, 0)


def _attn_prompt_call(qkv, bias):
    bsz, s, _ = qkv.shape
    assert s % Q_STEP == 0
    bsel = lambda b: (b, 0, 0)
    return pl.pallas_call(
        functools.partial(_attn_prompt_kernel, s=s),
        grid=(bsz,),
        in_specs=[pl.BlockSpec((None, s, D_QKV), bsel),
                  pl.BlockSpec((N_HEADS, Q_STEP, K_WIN), lambda b: (0, 0, 0))],
        out_specs=pl.BlockSpec((None, s, D_ATT), bsel),
        out_shape=jax.ShapeDtypeStruct((bsz, s, D_ATT), BF16),
        scratch_shapes=[pltpu.VMEM((HEADS_PER_GROUP, s, D_ATT), BF16)],
        compiler_params=_cparams(("parallel",)),
        name="attention_prompt",
    )(qkv, bias)


def _attn_sample_kernel(qkv_ref, pk_ref, pv_ref, bias_ref, o_ref, k_ref, vmask_ref, *, s):
    k_ref[0:PAST, :] = pk_ref[...]
    k_ref[PAST:PAST + s, :] = qkv_ref[:, D_ATT:2 * D_ATT]
    head = _group_head_of_lane((PAST, D_ATT))
    head_new = _group_head_of_lane((s, D_ATT))
    for h in range(HEADS_PER_GROUP):
        vmask_ref[h, 0:PAST, :] = jnp.where(head == h, pv_ref[...], jnp.zeros_like(pv_ref[...]))
        v = qkv_ref[:, 2 * D_ATT:3 * D_ATT]
        vmask_ref[h, PAST:PAST + s, :] = jnp.where(head_new == h, v, jnp.zeros_like(v))
    for g in range(N_GROUPS):
        cols = slice(g * GROUP_W, (g + 1) * GROUP_W)
        out = _attn_group(qkv_ref[:, cols], k_ref[:, cols],
                          lambda h: vmask_ref[h, :, cols],
                          lambda h: bias_ref[g * HEADS_PER_GROUP + h])
        o_ref[:, cols] = out.astype(BF16)


def _attn_sample_call(qkv, pk, pv, bias):
    bsz, s, _ = qkv.shape
    band = PAST + s
    bsel = lambda b: (b, 0, 0)
    return pl.pallas_call(
        functools.partial(_attn_sample_kernel, s=s),
        grid=(bsz,),
        in_specs=[pl.BlockSpec((None, s, D_QKV), bsel),
                  pl.BlockSpec((None, PAST, D_ATT), bsel),
                  pl.BlockSpec((None, PAST, D_ATT), bsel),
                  pl.BlockSpec((N_HEADS, s, band), lambda b: (0, 0, 0))],
        out_specs=pl.BlockSpec((None, s, D_ATT), bsel),
        out_shape=jax.ShapeDtypeStruct((bsz, s, D_ATT), BF16),
        scratch_shapes=[pltpu.VMEM((band, D_ATT), BF16),
                        pltpu.VMEM((HEADS_PER_GROUP, band, D_ATT), BF16)],
        compiler_params=_cparams(("parallel",)),
        name="attention_sample",
    )(qkv, pk, pv, bias)


def _mix_ffn_kernel(att_ref, ssm_ref, sc_ref, x_ref, gin_ref, bin_ref, wo_ref, g1_ref, b1_ref,
                    w1_ref, cw_ref, cb_ref, c0_ref, w2_ref, g2_ref, b2_ref,
                    o_ref, cn_ref, hist_ref, *, nb, tt, ln_in):
    ti = pl.program_id(1)
    rows = nb * tt

    @pl.when(ti == 0)
    def _():
        hist_ref[...] = c0_ref[...]

    x = x_ref[...].reshape(rows, D_MODEL)
    if ln_in:
        x = _layer_norm(x, gin_ref[...], bin_ref[...])
    mix = _dot(att_ref[...].reshape(rows, D_ATT), wo_ref[0:D_ATT, :])
    mix = mix + _dot(ssm_ref[...].reshape(rows, D_SSM), wo_ref[D_ATT:D_ATT + D_SSM, :])
    mix = mix + _dot(sc_ref[...].reshape(rows, D_SCONV), wo_ref[D_ATT + D_SSM:D_MODEL, :])
    x1 = _layer_norm(ALPHA * x + mix, g1_ref[...], b1_ref[...])

    up = _dot(x1.astype(BF16), w1_ref[...])
    conv = _causal_conv3(up, hist_ref, cw_ref, cb_ref, nb, tt)
    cn_ref[...] = hist_ref[...]
    gate = conv[:, 0:D_FF]
    val = conv[:, D_FF:2 * D_FF]
    h = (gate * jax.nn.sigmoid(gate) * val).astype(BF16)
    y = _dot(h, w2_ref[...])
    o_ref[...] = _layer_norm(ALPHA * x1 + y, g2_ref[...], b2_ref[...]).reshape(nb, tt, D_MODEL)


def _mix_ffn_call(att, ssm, sc, x, ln_in, w_out, g1, b1, w1, cw, cb, c0, w2, g2, b2, nb, tt):
    apply_ln, g_in, b_in = ln_in
    bsz, s, _ = x.shape
    blk = lambda b, t: (b, t, 0)
    bonly = lambda b, t: (b, 0, 0)
    const2 = lambda b, t: (0, 0)
    vec = pl.BlockSpec((1, D_MODEL), const2)
    return pl.pallas_call(
        functools.partial(_mix_ffn_kernel, nb=nb, tt=tt, ln_in=apply_ln),
        grid=(bsz // nb, s // tt),
        in_specs=[pl.BlockSpec((nb, tt, D_ATT), blk),
                  pl.BlockSpec((nb, tt, D_SSM), blk),
                  pl.BlockSpec((nb, tt, D_SCONV), blk),
                  pl.BlockSpec((nb, tt, D_MODEL), blk), vec, vec,
                  pl.BlockSpec((D_MODEL, D_MODEL), const2), vec, vec,
                  pl.BlockSpec((D_MODEL, 2 * D_FF), const2),
                  pl.BlockSpec((CONV_K, 2 * D_FF), const2),
                  pl.BlockSpec((1, 2 * D_FF), const2),
                  pl.BlockSpec((nb, SUBLANES, 2 * D_FF), bonly),
                  pl.BlockSpec((D_FF, D_MODEL), const2), vec, vec],
        out_specs=[pl.BlockSpec((nb, tt, D_MODEL), blk),
                   pl.BlockSpec((nb, SUBLANES, 2 * D_FF), bonly)],
        out_shape=[jax.ShapeDtypeStruct((bsz, s, D_MODEL), F32),
                   jax.ShapeDtypeStruct((bsz, SUBLANES, 2 * D_FF), F32)],
        scratch_shapes=[pltpu.VMEM((nb, SUBLANES, 2 * D_FF), F32)],
        compiler_params=_cparams(("parallel", "arbitrary")),
        name="mix_ffn",
    )(att, ssm, sc, x, g_in, b_in, w_out, g1, b1, w1, cw, cb, c0, w2, g2, b2)


def _rel_bias_table(rel_bias, tq, chunk):
    nk = PAST + tq
    period = nk + tq
    dist = (nk - 1) - jnp.arange(period)
    by_lag = rel_bias[:, jnp.clip(dist, -(CHUNK - 1), REL_MAX) + (CHUNK - 1)].astype(F32)
    rolled = jnp.roll(by_lag, -(tq - 1), axis=-1)
    n_heads = rel_bias.shape[0]
    bias = jnp.tile(rolled, (1, tq))[:, :tq * (period - 1)].reshape(n_heads, tq, period - 1)[:, :, :nk]
    if tq == chunk:
        return bias
    t_idx = np.arange(tq)[:, None]
    s_idx = np.arange(nk)[None, :]
    key_chunk = s_idx // chunk - t_idx // chunk
    in_band = (key_chunk >= 0) & (key_chunk <= N_PAST_CHUNKS)
    return jnp.where(in_band[None], bias, -jnp.inf)


def _ssm_params(lam_re, lam_im, log_dt, b_re, b_im, c_re, c_im):
    lam = lax.complex(lam_re.astype(F32), lam_im.astype(F32))
    dt = jnp.exp(log_dt.astype(F32))[:, None]
    lbar = jnp.exp(lam * dt)
    bbar = ((lbar - 1.0) / lam)[:, :, None] * lax.complex(b_re.astype(F32), b_im.astype(F32))
    eye = jnp.eye(N_SSM_GROUPS, dtype=F32)

    def in_map(m):
        return jnp.einsum('gpc,gh->gchp', m, eye).reshape(D_SSM, D_STATE)

    def out_map(m):
        return jnp.einsum('gcp,gh->gphc', m, eye).reshape(D_STATE, D_SSM)

    bblk = jnp.concatenate([in_map(bbar.real), in_map(bbar.imag)], axis=1).astype(BF16)
    cblk = jnp.concatenate([out_map(c_re.astype(F32)), out_map(-c_im.astype(F32))], axis=0).astype(BF16)
    lbar2 = jnp.stack([lbar.real.reshape(D_STATE), lbar.imag.reshape(D_STATE)], axis=0)
    return lbar2, bblk, cblk


def _pad_rows(buf):
    return jnp.pad(buf.astype(F32), ((0, 0), (SUBLANES - (CONV_K - 1), 0), (0, 0)))


def _trunk_layer(x, ln_in, lp, past, tiles):
    bsz, s, _ = x.shape
    tt_in, nb_ffn, tt_ffn = tiles
    if past is None:
        keep = min(PAST, s)
        h0 = jnp.zeros((bsz, 2 * D_STATE), F32)
        sc0 = jnp.zeros((bsz, SUBLANES, D_SCONV), F32)
        ff0 = jnp.zeros((bsz, SUBLANES, 2 * D_FF), F32)
    else:
        k_past, v_past, h_re0, h_im0, sconv_buf, ffn_buf = past
        keep = s
        pk = k_past.reshape(bsz, PAST, D_ATT).astype(BF16)
        pv = v_past.reshape(bsz, PAST, D_ATT).astype(BF16)
        h0 = jnp.concatenate([h_re0.reshape(bsz, D_STATE), h_im0.reshape(bsz, D_STATE)], axis=1).astype(F32)
        sc0 = _pad_rows(sconv_buf)
        ff0 = _pad_rows(ffn_buf)

    qkv, k_new, v_new, ssm_out, sconv_out, sc_new, h_new = _in_ssm_call(
        x, ln_in, lp['w_in'], lp['sconv_w'], lp['sconv_b'], sc0, h0,
        lp['lbar'], lp['bblk'], lp['cblk'], lp['d'], lp['w_glu'], tt_in, keep)
    if past is None:
        att = _attn_prompt_call(qkv, _rel_bias_table(lp['rel_bias'], Q_STEP, CHUNK))
    else:
        att = _attn_sample_call(qkv, pk, pv, _rel_bias_table(lp['rel_bias'], s, s))
    x2, ff_new = _mix_ffn_call(att, ssm_out, sconv_out, x, ln_in, lp['w_out'], lp['ln1_g'], lp['ln1_b'],
                               lp['w_ff_in'], lp['ffn_conv_w'], lp['ffn_conv_b'], ff0,
                               lp['w_ff_out'], lp['ln2_g'], lp['ln2_b'], nb_ffn, tt_ffn)
    new = (k_new.reshape(bsz, keep, N_HEADS, HEAD_DIM), v_new.reshape(bsz, keep, N_HEADS, HEAD_DIM),
           h_new[:, 0:D_STATE].reshape(bsz, N_SSM_GROUPS, SSM_STATE),
           h_new[:, D_STATE:].reshape(bsz, N_SSM_GROUPS, SSM_STATE),
           sc_new[:, SUBLANES - (CONV_K - 1):], ff_new[:, SUBLANES - (CONV_K - 1):])
    return x2, new


ROW_TILE = 512
IN_SSM_POSITIONS = 128


def _tiles(bsz, s):
    tt_in = min(IN_SSM_POSITIONS, s)
    tt_ffn = min(ROW_TILE, s)
    nb_ffn = min(bsz, ROW_TILE // tt_ffn)
    return tt_in, nb_ffn, tt_ffn


def kernel(x_prompt, x_sample, cache_k, cache_v, state_ssm_re, state_ssm_im, cache_sconv, cache_ffn_conv, ln_in_g, ln_in_b, w_in, rel_bias, ssm_lam_re, ssm_lam_im, ssm_log_dt, ssm_b_re, ssm_b_im, ssm_c_re, ssm_c_im, ssm_d, w_glu, sconv_w, sconv_b, w_out, ln1_g, ln1_b, w_ff_in, ffn_conv_w, ffn_conv_b, w_ff_out, ln2_g, ln2_b):
    bp, sp, _ = x_prompt.shape
    bs, ss, _ = x_sample.shape
    g_in = ln_in_g.reshape(1, D_MODEL)
    b_in = ln_in_b.reshape(1, D_MODEL)
    xp, xs = x_prompt, x_sample
    st_p = [[] for _ in range(6)]
    st_s = [[] for _ in range(6)]
    for l in range(DEPTH):
        lbar, bblk, cblk = _ssm_params(ssm_lam_re[l], ssm_lam_im[l], ssm_log_dt[l], ssm_b_re[l], ssm_b_im[l],
                                       ssm_c_re[l], ssm_c_im[l])
        lp = {'w_in': w_in[l].astype(BF16), 'rel_bias': rel_bias[l],
              'lbar': lbar, 'bblk': bblk, 'cblk': cblk,
              'd': ssm_d[l].reshape(1, D_SSM), 'w_glu': w_glu[l].astype(BF16),
              'sconv_w': sconv_w[l], 'sconv_b': sconv_b[l].reshape(1, D_SCONV),
              'w_out': w_out[l].astype(BF16),
              'ln1_g': ln1_g[l].reshape(1, D_MODEL), 'ln1_b': ln1_b[l].reshape(1, D_MODEL),
              'w_ff_in': w_ff_in[l].astype(BF16), 'ffn_conv_w': ffn_conv_w[l],
              'ffn_conv_b': ffn_conv_b[l].reshape(1, 2 * D_FF),
              'w_ff_out': w_ff_out[l].astype(BF16),
              'ln2_g': ln2_g[l].reshape(1, D_MODEL), 'ln2_b': ln2_b[l].reshape(1, D_MODEL)}
        ln_in = (l == 0, g_in, b_in)
        xp, new_p = _trunk_layer(xp, ln_in, lp, None, _tiles(bp, sp))
        past = (cache_k[l], cache_v[l], state_ssm_re[l], state_ssm_im[l], cache_sconv[l], cache_ffn_conv[l])
        xs, new_s = _trunk_layer(xs, ln_in, lp, past, _tiles(bs, ss))
        for i in range(6):
            st_p[i].append(new_p[i])
            st_s[i].append(new_s[i])
    k_p, v_p, hre_p, him_p, sc_p, ff_p = [jnp.stack(a, axis=0) for a in st_p]
    k_s, v_s, hre_s, him_s, sc_s, ff_s = [jnp.stack(a, axis=0) for a in st_s]
    return (xp, xs, k_p, v_p, k_s, v_s, hre_p, him_p, hre_s, him_s, sc_p, sc_s, ff_p, ff_s)
```

```python
import functools
import math

import jax
import jax.numpy as jnp
import numpy as np
from jax import lax
from jax.experimental import pallas as pl
from jax.experimental.pallas import tpu as pltpu

D_MODEL = 1024
DEPTH = 4
CHUNK = 64
N_HEADS = 8
HEAD_DIM = 64
D_ATT = N_HEADS * HEAD_DIM
N_PAST_CHUNKS = 8
PAST = N_PAST_CHUNKS * CHUNK
REL_MAX = 128
SSM_GROUP = 16
N_SSM_GROUPS = 16
D_SSM = SSM_GROUP * N_SSM_GROUPS
SSM_STATE = 64
D_STATE = N_SSM_GROUPS * SSM_STATE
D_SCONV = 256
CONV_K = 3
D_FF = 2048
D_QKV = 3 * D_ATT
D_REST = D_SSM + 3 * D_SCONV
ALPHA = (2 * DEPTH) ** 0.25
LN_EPS = 1e-5

SUBLANES = 8
VMEM_LIMIT = 56 * 1024 * 1024

F32 = jnp.float32
BF16 = jnp.bfloat16


def _cparams(sem):
    return pltpu.CompilerParams(dimension_semantics=sem, vmem_limit_bytes=VMEM_LIMIT)


def _layer_norm(x, g, b):
    mu = jnp.mean(x, axis=-1, keepdims=True)
    xc = x - mu
    var = jnp.mean(xc * xc, axis=-1, keepdims=True)
    return xc * lax.rsqrt(var + LN_EPS) * g + b


def _dot(a, b):
    return jnp.dot(a, b, preferred_element_type=F32)


def _causal_conv3(cur, hist_ref, w_ref, b_ref, nb, tt):
    outs = []
    for s in range(nb):
        v = cur[s * tt:(s + 1) * tt]
        ext = jnp.concatenate([hist_ref[s], v], axis=0)
        m1 = pltpu.roll(ext, 1, 0)[SUBLANES:]
        m2 = pltpu.roll(ext, 2, 0)[SUBLANES:]
        hist_ref[s] = v[tt - SUBLANES:tt]
        outs.append(w_ref[0:1, :] * m2 + w_ref[1:2, :] * m1 + w_ref[2:3, :] * v + b_ref[...])
    return outs[0] if nb == 1 else jnp.concatenate(outs, axis=0)


def _gelu_tanh(x):
    c = math.sqrt(2.0 / math.pi)
    return 0.5 * x * (1.0 + jnp.tanh(c * (x + 0.044715 * (x * x * x))))


def _in_ssm_kernel(x_ref, gin_ref, bin_ref, w_ref, scw_ref, scb_ref, sc0_ref, h0_ref,
                   lbar_ref, bblk_ref, cblk_ref, d_ref, wglu_ref,
                   qkv_ref, kt_ref, vt_ref, ssm_ref, sc_ref, scn_ref, hn_ref,
                   hist_ref, h_ref, st_ref, *, tt, ln_in):
    ti = pl.program_id(1)
    nb = SUBLANES
    rows = nb * tt

    @pl.when(ti == 0)
    def _():
        hist_ref[...] = sc0_ref[...]
        h_ref[...] = h0_ref[...]

    x = x_ref[...].reshape(rows, D_MODEL)
    if ln_in:
        x = _layer_norm(x, gin_ref[...], bin_ref[...])
    xb = x.astype(BF16)
    u = _dot(xb, w_ref[:, D_QKV:D_QKV + D_SSM])
    ut = jnp.swapaxes(u.reshape(nb, tt, D_SSM), 0, 1).reshape(rows, D_SSM)
    bu = _dot(ut.astype(BF16), bblk_ref[...])

    q = _dot(xb, w_ref[:, 0:D_ATT]) * (HEAD_DIM ** -0.5)
    qkv_ref[:, :, 0:D_ATT] = q.astype(BF16).reshape(nb, tt, D_ATT)
    for tail_ref, lo in ((kt_ref, D_ATT), (vt_ref, 2 * D_ATT)):
        kv = _dot(xb, w_ref[:, lo:lo + D_ATT])
        qkv_ref[:, :, lo:lo + D_ATT] = kv.astype(BF16).reshape(nb, tt, D_ATT)
        tail_ref[...] = kv.reshape(nb, tt, D_ATT)

    g = _dot(xb, w_ref[:, D_QKV + D_SSM:D_QKV + D_REST])
    gate_b = g[:, 0:D_SCONV]
    gate_c = g[:, D_SCONV:2 * D_SCONV]
    xv = g[:, 2 * D_SCONV:3 * D_SCONV]
    conv = _causal_conv3(gate_c * xv, hist_ref, scw_ref, scb_ref, nb, tt)
    sc_ref[...] = (gate_b * conv).astype(BF16).reshape(nb, tt, D_SCONV)
    scn_ref[...] = hist_ref[...]

    lr = jnp.broadcast_to(lbar_ref[0:1, :], (SUBLANES, D_STATE))
    li = jnp.broadcast_to(lbar_ref[1:2, :], (SUBLANES, D_STATE))
    hr = h_ref[:, 0:D_STATE]
    hi = h_ref[:, D_STATE:2 * D_STATE]
    for t in range(tt):
        r = slice(t * SUBLANES, (t + 1) * SUBLANES)
        hr, hi = (lr * hr - li * hi + bu[r, 0:D_STATE],
                  lr * hi + li * hr + bu[r, D_STATE:2 * D_STATE])
        st_ref[r, 0:D_STATE] = hr
        st_ref[r, D_STATE:2 * D_STATE] = hi
    y = _dot(st_ref[...].astype(BF16), cblk_ref[...]) + d_ref[...] * ut
    z = _gelu_tanh(y)
    gate = jax.nn.sigmoid(_dot(z.astype(BF16), wglu_ref[...]))
    h_ref[:, 0:D_STATE] = hr
    h_ref[:, D_STATE:2 * D_STATE] = hi
    hn_ref[:, 0:D_STATE] = hr
    hn_ref[:, D_STATE:2 * D_STATE] = hi
    out = (z * gate).reshape(tt, nb, D_SSM)
    ssm_ref[...] = jnp.swapaxes(out, 0, 1).astype(BF16)


def _in_ssm_call(x, ln_in, w_in, scw, scb, sc0, h0, lbar, bblk, cblk, d, wglu, tt, keep):
    apply_ln, g_in, b_in = ln_in
    bsz, s, _ = x.shape
    nb = SUBLANES
    nt = s // tt
    tail_first = nt - keep // tt
    blk = lambda b, t: (b, t, 0)
    tail = lambda b, t: (b, jnp.maximum(t - tail_first, 0), 0)
    bonly3 = lambda b, t: (b, 0, 0)
    bonly2 = lambda b, t: (b, 0)
    const2 = lambda b, t: (0, 0)
    return pl.pallas_call(
        functools.partial(_in_ssm_kernel, tt=tt, ln_in=apply_ln),
        grid=(bsz // nb, nt),
        in_specs=[pl.BlockSpec((nb, tt, D_MODEL), blk),
                  pl.BlockSpec((1, D_MODEL), const2),
                  pl.BlockSpec((1, D_MODEL), const2),
                  pl.BlockSpec((D_MODEL, D_QKV + D_REST), const2),
                  pl.BlockSpec((CONV_K, D_SCONV), const2),
                  pl.BlockSpec((1, D_SCONV), const2),
                  pl.BlockSpec((nb, SUBLANES, D_SCONV), bonly3),
                  pl.BlockSpec((nb, 2 * D_STATE), bonly2),
                  pl.BlockSpec((2, D_STATE), const2),
                  pl.BlockSpec((D_SSM, 2 * D_STATE), const2),
                  pl.BlockSpec((2 * D_STATE, D_SSM), const2),
                  pl.BlockSpec((1, D_SSM), const2),
                  pl.BlockSpec((D_SSM, D_SSM), const2)],
        out_specs=[pl.BlockSpec((nb, tt, D_QKV), blk),
                   pl.BlockSpec((nb, tt, D_ATT), tail),
                   pl.BlockSpec((nb, tt, D_ATT), tail),
                   pl.BlockSpec((nb, tt, D_SSM), blk),
                   pl.BlockSpec((nb, tt, D_SCONV), blk),
                   pl.BlockSpec((nb, SUBLANES, D_SCONV), bonly3),
                   pl.BlockSpec((nb, 2 * D_STATE), bonly2)],
        out_shape=[jax.ShapeDtypeStruct((bsz, s, D_QKV), BF16),
                   jax.ShapeDtypeStruct((bsz, keep, D_ATT), F32),
                   jax.ShapeDtypeStruct((bsz, keep, D_ATT), F32),
                   jax.ShapeDtypeStruct((bsz, s, D_SSM), BF16),
                   jax.ShapeDtypeStruct((bsz, s, D_SCONV), BF16),
                   jax.ShapeDtypeStruct((bsz, SUBLANES, D_SCONV), F32),
                   jax.ShapeDtypeStruct((bsz, 2 * D_STATE), F32)],
        scratch_shapes=[pltpu.VMEM((nb, SUBLANES, D_SCONV), F32),
                        pltpu.VMEM((nb, 2 * D_STATE), F32),
                        pltpu.VMEM((nb * tt, 2 * D_STATE), F32)],
        compiler_params=_cparams(("parallel", "arbitrary")),
        name="in_ssm",
    )(x, g_in, b_in, w_in, scw, scb, sc0, h0, lbar, bblk, cblk, d, wglu)


HEADS_PER_GROUP = 4
GROUP_W = HEADS_PER_GROUP * HEAD_DIM
N_GROUPS = N_HEADS // HEADS_PER_GROUP
Q_STEP = 4 * CHUNK
K_WIN = PAST + Q_STEP


def _softmax_unnormalised(sc):
    m = jnp.max(sc, axis=-1, keepdims=True)
    e = jnp.exp(sc - m)
    return e.astype(BF16), jnp.sum(e, axis=-1, keepdims=True)


def _group_head_of_lane(shape):
    lane = lax.broadcasted_iota(jnp.int32, shape, len(shape) - 1)
    return (lane // HEAD_DIM) % HEADS_PER_GROUP


def _softmax_banded(sc, bias):
    half, skip = Q_STEP // 2, 2 * CHUNK
    zeros = jnp.zeros((half, skip), BF16)
    e_top, l_top = _softmax_unnormalised(sc[0:half, 0:K_WIN - skip] + bias[0:half, 0:K_WIN - skip])
    e_bot, l_bot = _softmax_unnormalised(sc[half:Q_STEP, skip:K_WIN] + bias[half:Q_STEP, skip:K_WIN])
    e = jnp.concatenate([jnp.concatenate([e_top, zeros], axis=1),
                         jnp.concatenate([zeros, e_bot], axis=1)], axis=0)
    return e, jnp.concatenate([l_top, l_bot], axis=0)


def _attn_group(q4, k4, v_of_head, bias_of_head):
    head = _group_head_of_lane(q4.shape)
    full_window = q4.shape[0] == Q_STEP and k4.shape[0] == K_WIN
    acc = None
    scale = None
    for h in range(HEADS_PER_GROUP):
        qm = jnp.where(head == h, q4, jnp.zeros_like(q4))
        sc = lax.dot_general(qm, k4, (((1,), (1,)), ((), ())), preferred_element_type=F32)
        if full_window:
            e, l = _softmax_banded(sc, bias_of_head(h))
        else:
            e, l = _softmax_unnormalised(sc + bias_of_head(h))
        pv = _dot(e, v_of_head(h))
        acc = pv if acc is None else acc + pv
        inv = 1.0 / l
        scale = inv if scale is None else jnp.where(head == h, inv, scale)
    return acc * scale


def _attn_prompt_kernel(qkv_ref, bias_ref, o_ref, vmask_ref, *, s):
    v = qkv_ref[:, 2 * D_ATT:3 * D_ATT]
    head = _group_head_of_lane(v.shape)
    for h in range(HEADS_PER_GROUP):
        vmask_ref[h] = jnp.where(head == h, v, jnp.zeros_like(v))

    def step(r0, k0, nk):
        for g in range(N_GROUPS):
            cols = slice(g * GROUP_W, (g + 1) * GROUP_W)
            kcols = slice(D_ATT + g * GROUP_W, D_ATT + (g + 1) * GROUP_W)
            out = _attn_group(
                qkv_ref[pl.ds(r0, Q_STEP), cols],
                qkv_ref[pl.ds(k0, nk), kcols],
                lambda h: vmask_ref[h, pl.ds(k0, nk), cols],
                lambda h: bias_ref[g * HEADS_PER_GROUP + h, :, K_WIN - nk:K_WIN])
            o_ref[pl.ds(r0, Q_STEP), cols] = out.astype(BF16)

    n_steps = s // Q_STEP
    n_head_steps = min(PAST // Q_STEP, n_steps)
    for j in range(n_head_steps):
        step(j * Q_STEP, 0, (j + 1) * Q_STEP)

    def body(j, carry):
        r0 = pl.multiple_of(j * Q_STEP, Q_STEP)
        step(r0, pl.multiple_of(r0 - PAST, Q_STEP), K_WIN)
        return carry

    if n_steps > n_head_steps:
        lax.fori_loop(n_head_steps, n_steps, body, 0)


def _attn_prompt_call(qkv, bias):
    bsz, s, _ = qkv.shape
    assert s % Q_STEP == 0
    bsel = lambda b: (b, 0, 0)
    return pl.pallas_call(
        functools.partial(_attn_prompt_kernel, s=s),
        grid=(bsz,),
        in_specs=[pl.BlockSpec((None, s, D_QKV), bsel),
                  pl.BlockSpec((N_HEADS, Q_STEP, K_WIN), lambda b: (0, 0, 0))],
        out_specs=pl.BlockSpec((None, s, D_ATT), bsel),
        out_shape=jax.ShapeDtypeStruct((bsz, s, D_ATT), BF16),
        scratch_shapes=[pltpu.VMEM((HEADS_PER_GROUP, s, D_ATT), BF16)],
        compiler_params=_cparams(("parallel",)),
        name="attention_prompt",
    )(qkv, bias)


def _attn_sample_kernel(qkv_ref, pk_ref, pv_ref, bias_ref, o_ref, k_ref, vmask_ref, *, s):
    k_ref[0:PAST, :] = pk_ref[...]
    k_ref[PAST:PAST + s, :] = qkv_ref[:, D_ATT:2 * D_ATT]
    head = _group_head_of_lane((PAST, D_ATT))
    head_new = _group_head_of_lane((s, D_ATT))
    for h in range(HEADS_PER_GROUP):
        vmask_ref[h, 0:PAST, :] = jnp.where(head == h, pv_ref[...], jnp.zeros_like(pv_ref[...]))
        v = qkv_ref[:, 2 * D_ATT:3 * D_ATT]
        vmask_ref[h, PAST:PAST + s, :] = jnp.where(head_new == h, v, jnp.zeros_like(v))
    for g in range(N_GROUPS):
        cols = slice(g * GROUP_W, (g + 1) * GROUP_W)
        out = _attn_group(qkv_ref[:, cols], k_ref[:, cols],
                          lambda h: vmask_ref[h, :, cols],
                          lambda h: bias_ref[g * HEADS_PER_GROUP + h])
        o_ref[:, cols] = out.astype(BF16)


def _attn_sample_call(qkv, pk, pv, bias):
    bsz, s, _ = qkv.shape
    band = PAST + s
    bsel = lambda b: (b, 0, 0)
    return pl.pallas_call(
        functools.partial(_attn_sample_kernel, s=s),
        grid=(bsz,),
        in_specs=[pl.BlockSpec((None, s, D_QKV), bsel),
                  pl.BlockSpec((None, PAST, D_ATT), bsel),
                  pl.BlockSpec((None, PAST, D_ATT), bsel),
                  pl.BlockSpec((N_HEADS, s, band), lambda b: (0, 0, 0))],
        out_specs=pl.BlockSpec((None, s, D_ATT), bsel),
        out_shape=jax.ShapeDtypeStruct((bsz, s, D_ATT), BF16),
        scratch_shapes=[pltpu.VMEM((band, D_ATT), BF16),
                        pltpu.VMEM((HEADS_PER_GROUP, band, D_ATT), BF16)],
        compiler_params=_cparams(("parallel",)),
        name="attention_sample",
    )(qkv, pk, pv, bias)


def _mix_ffn_kernel(att_ref, ssm_ref, sc_ref, x_ref, gin_ref, bin_ref, wo_ref, g1_ref, b1_ref,
                    w1_ref, cw_ref, cb_ref, c0_ref, w2_ref, g2_ref, b2_ref,
                    o_ref, cn_ref, hist_ref, *, nb, tt, ln_in):
    ti = pl.program_id(1)
    rows = nb * tt

    @pl.when(ti == 0)
    def _():
        hist_ref[...] = c0_ref[...]

    x = x_ref[...].reshape(rows, D_MODEL)
    if ln_in:
        x = _layer_norm(x, gin_ref[...], bin_ref[...])
    mix = _dot(att_ref[...].reshape(rows, D_ATT), wo_ref[0:D_ATT, :])
    mix = mix + _dot(ssm_ref[...].reshape(rows, D_SSM), wo_ref[D_ATT:D_ATT + D_SSM, :])
    mix = mix + _dot(sc_ref[...].reshape(rows, D_SCONV), wo_ref[D_ATT + D_SSM:D_MODEL, :])
    x1 = _layer_norm(ALPHA * x + mix, g1_ref[...], b1_ref[...])

    up = _dot(x1.astype(BF16), w1_ref[...])
    conv = _causal_conv3(up, hist_ref, cw_ref, cb_ref, nb, tt)
    cn_ref[...] = hist_ref[...]
    gate = conv[:, 0:D_FF]
    val = conv[:, D_FF:2 * D_FF]
    h = (gate * jax.nn.sigmoid(gate) * val).astype(BF16)
    y = _dot(h, w2_ref[...])
    o_ref[...] = _layer_norm(ALPHA * x1 + y, g2_ref[...], b2_ref[...]).reshape(nb, tt, D_MODEL)


def _mix_ffn_call(att, ssm, sc, x, ln_in, w_out, g1, b1, w1, cw, cb, c0, w2, g2, b2, nb, tt):
    apply_ln, g_in, b_in = ln_in
    bsz, s, _ = x.shape
    blk = lambda b, t: (b, t, 0)
    bonly = lambda b, t: (b, 0, 0)
    const2 = lambda b, t: (0, 0)
    vec = pl.BlockSpec((1, D_MODEL), const2)
    return pl.pallas_call(
        functools.partial(_mix_ffn_kernel, nb=nb, tt=tt, ln_in=apply_ln),
        grid=(bsz // nb, s // tt),
        in_specs=[pl.BlockSpec((nb, tt, D_ATT), blk),
                  pl.BlockSpec((nb, tt, D_SSM), blk),
                  pl.BlockSpec((nb, tt, D_SCONV), blk),
                  pl.BlockSpec((nb, tt, D_MODEL), blk), vec, vec,
                  pl.BlockSpec((D_MODEL, D_MODEL), const2), vec, vec,
                  pl.BlockSpec((D_MODEL, 2 * D_FF), const2),
                  pl.BlockSpec((CONV_K, 2 * D_FF), const2),
                  pl.BlockSpec((1, 2 * D_FF), const2),
                  pl.BlockSpec((nb, SUBLANES, 2 * D_FF), bonly),
                  pl.BlockSpec((D_FF, D_MODEL), const2), vec, vec],
        out_specs=[pl.BlockSpec((nb, tt, D_MODEL), blk),
                   pl.BlockSpec((nb, SUBLANES, 2 * D_FF), bonly)],
        out_shape=[jax.ShapeDtypeStruct((bsz, s, D_MODEL), F32),
                   jax.ShapeDtypeStruct((bsz, SUBLANES, 2 * D_FF), F32)],
        scratch_shapes=[pltpu.VMEM((nb, SUBLANES, 2 * D_FF), F32)],
        compiler_params=_cparams(("parallel", "arbitrary")),
        name="mix_ffn",
    )(att, ssm, sc, x, g_in, b_in, w_out, g1, b1, w1, cw, cb, c0, w2, g2, b2)


def _rel_bias_table(rel_bias, tq, chunk):
    nk = PAST + tq
    period = nk + tq
    dist = (nk - 1) - jnp.arange(period)
    by_lag = rel_bias[:, jnp.clip(dist, -(CHUNK - 1), REL_MAX) + (CHUNK - 1)].astype(F32)
    rolled = jnp.roll(by_lag, -(tq - 1), axis=-1)
    n_heads = rel_bias.shape[0]
    bias = jnp.tile(rolled, (1, tq))[:, :tq * (period - 1)].reshape(n_heads, tq, period - 1)[:, :, :nk]
    if tq == chunk:
        return bias
    t_idx = np.arange(tq)[:, None]
    s_idx = np.arange(nk)[None, :]
    key_chunk = s_idx // chunk - t_idx // chunk
    in_band = (key_chunk >= 0) & (key_chunk <= N_PAST_CHUNKS)
    return jnp.where(in_band[None], bias, -jnp.inf)


def _ssm_params(lam_re, lam_im, log_dt, b_re, b_im, c_re, c_im):
    lam = lax.complex(lam_re.astype(F32), lam_im.astype(F32))
    dt = jnp.exp(log_dt.astype(F32))[:, None]
    lbar = jnp.exp(lam * dt)
    bbar = ((lbar - 1.0) / lam)[:, :, None] * lax.complex(b_re.astype(F32), b_im.astype(F32))
    eye = jnp.eye(N_SSM_GROUPS, dtype=F32)

    def in_map(m):
        return jnp.einsum('gpc,gh->gchp', m, eye).reshape(D_SSM, D_STATE)

    def out_map(m):
        return jnp.einsum('gcp,gh->gphc', m, eye).reshape(D_STATE, D_SSM)

    bblk = jnp.concatenate([in_map(bbar.real), in_map(bbar.imag)], axis=1).astype(BF16)
    cblk = jnp.concatenate([out_map(c_re.astype(F32)), out_map(-c_im.astype(F32))], axis=0).astype(BF16)
    lbar2 = jnp.stack([lbar.real.reshape(D_STATE), lbar.imag.reshape(D_STATE)], axis=0)
    return lbar2, bblk, cblk


def _pad_rows(buf):
    return jnp.pad(buf.astype(F32), ((0, 0), (SUBLANES - (CONV_K - 1), 0), (0, 0)))


def _trunk_layer(x, ln_in, lp, past, tiles):
    bsz, s, _ = x.shape
    tt_in, nb_ffn, tt_ffn = tiles
    if past is None:
        keep = min(PAST, s)
        h0 = jnp.zeros((bsz, 2 * D_STATE), F32)
        sc0 = jnp.zeros((bsz, SUBLANES, D_SCONV), F32)
        ff0 = jnp.zeros((bsz, SUBLANES, 2 * D_FF), F32)
    else:
        k_past, v_past, h_re0, h_im0, sconv_buf, ffn_buf = past
        keep = s
        pk = k_past.reshape(bsz, PAST, D_ATT).astype(BF16)
        pv = v_past.reshape(bsz, PAST, D_ATT).astype(BF16)
        h0 = jnp.concatenate([h_re0.reshape(bsz, D_STATE), h_im0.reshape(bsz, D_STATE)], axis=1).astype(F32)
        sc0 = _pad_rows(sconv_buf)
        ff0 = _pad_rows(ffn_buf)

    qkv, k_new, v_new, ssm_out, sconv_out, sc_new, h_new = _in_ssm_call(
        x, ln_in, lp['w_in'], lp['sconv_w'], lp['sconv_b'], sc0, h0,
        lp['lbar'], lp['bblk'], lp['cblk'], lp['d'], lp['w_glu'], tt_in, keep)
    if past is None:
        att = _attn_prompt_call(qkv, _rel_bias_table(lp['rel_bias'], Q_STEP, CHUNK))
    else:
        att = _attn_sample_call(qkv, pk, pv, _rel_bias_table(lp['rel_bias'], s, s))
    x2, ff_new = _mix_ffn_call(att, ssm_out, sconv_out, x, ln_in, lp['w_out'], lp['ln1_g'], lp['ln1_b'],
                               lp['w_ff_in'], lp['ffn_conv_w'], lp['ffn_conv_b'], ff0,
                               lp['w_ff_out'], lp['ln2_g'], lp['ln2_b'], nb_ffn, tt_ffn)
    new = (k_new, v_new,
           h_new[:, 0:D_STATE].reshape(bsz, N_SSM_GROUPS, SSM_STATE),
           h_new[:, D_STATE:].reshape(bsz, N_SSM_GROUPS, SSM_STATE),
           sc_new[:, SUBLANES - (CONV_K - 1):], ff_new[:, SUBLANES - (CONV_K - 1):])
    return x2, new


ROW_TILE = 512
IN_SSM_POSITIONS = 128


def _tiles(bsz, s):
    tt_in = min(IN_SSM_POSITIONS, s)
    tt_ffn = min(ROW_TILE, s)
    nb_ffn = min(bsz, ROW_TILE // tt_ffn)
    return tt_in, nb_ffn, tt_ffn


def kernel(x_prompt, x_sample, cache_k, cache_v, state_ssm_re, state_ssm_im, cache_sconv, cache_ffn_conv, ln_in_g, ln_in_b, w_in, rel_bias, ssm_lam_re, ssm_lam_im, ssm_log_dt, ssm_b_re, ssm_b_im, ssm_c_re, ssm_c_im, ssm_d, w_glu, sconv_w, sconv_b, w_out, ln1_g, ln1_b, w_ff_in, ffn_conv_w, ffn_conv_b, w_ff_out, ln2_g, ln2_b):
    bp, sp, _ = x_prompt.shape
    bs, ss, _ = x_sample.shape
    g_in = ln_in_g.reshape(1, D_MODEL)
    b_in = ln_in_b.reshape(1, D_MODEL)
    xp, xs = x_prompt, x_sample
    st_p = [[] for _ in range(6)]
    st_s = [[] for _ in range(6)]
    for l in range(DEPTH):
        lbar, bblk, cblk = _ssm_params(ssm_lam_re[l], ssm_lam_im[l], ssm_log_dt[l], ssm_b_re[l], ssm_b_im[l],
                                       ssm_c_re[l], ssm_c_im[l])
        lp = {'w_in': w_in[l].astype(BF16), 'rel_bias': rel_bias[l],
              'lbar': lbar, 'bblk': bblk, 'cblk': cblk,
              'd': ssm_d[l].reshape(1, D_SSM), 'w_glu': w_glu[l].astype(BF16),
              'sconv_w': sconv_w[l], 'sconv_b': sconv_b[l].reshape(1, D_SCONV),
              'w_out': w_out[l].astype(BF16),
              'ln1_g': ln1_g[l].reshape(1, D_MODEL), 'ln1_b': ln1_b[l].reshape(1, D_MODEL),
              'w_ff_in': w_ff_in[l].astype(BF16), 'ffn_conv_w': ffn_conv_w[l],
              'ffn_conv_b': ffn_conv_b[l].reshape(1, 2 * D_FF),
              'w_ff_out': w_ff_out[l].astype(BF16),
              'ln2_g': ln2_g[l].reshape(1, D_MODEL), 'ln2_b': ln2_b[l].reshape(1, D_MODEL)}
        ln_in = (l == 0, g_in, b_in)
        xp, new_p = _trunk_layer(xp, ln_in, lp, None, _tiles(bp, sp))
        past = (cache_k[l], cache_v[l], state_ssm_re[l], state_ssm_im[l], cache_sconv[l], cache_ffn_conv[l])
        xs, new_s = _trunk_layer(xs, ln_in, lp, past, _tiles(bs, ss))
        for i in range(6):
            st_p[i].append(new_p[i])
            st_s[i].append(new_s[i])
    k_p, v_p, hre_p, him_p, sc_p, ff_p = [jnp.stack(a, axis=0) for a in st_p]
    k_s, v_s, hre_s, him_s, sc_s, ff_s = [jnp.stack(a, axis=0) for a in st_s]
    heads = lambda a: a.reshape(a.shape[:-1] + (N_HEADS, HEAD_DIM))
    return (xp, xs, heads(k_p), heads(v_p), heads(k_s), heads(v_s),
            hre_p, him_p, hre_s, him_s, sc_p, sc_s, ff_p, ff_s)
```

```python
import functools
import math

import jax
import jax.numpy as jnp
import numpy as np
from jax import lax
from jax.experimental import pallas as pl
from jax.experimental.pallas import tpu as pltpu

D_MODEL = 1024
DEPTH = 4
CHUNK = 64
N_HEADS = 8
HEAD_DIM = 64
D_ATT = N_HEADS * HEAD_DIM
N_PAST_CHUNKS = 8
PAST = N_PAST_CHUNKS * CHUNK
REL_MAX = 128
SSM_GROUP = 16
N_SSM_GROUPS = 16
D_SSM = SSM_GROUP * N_SSM_GROUPS
SSM_STATE = 64
D_STATE = N_SSM_GROUPS * SSM_STATE
D_SCONV = 256
CONV_K = 3
D_FF = 2048
D_QKV = 3 * D_ATT
D_REST = D_SSM + 3 * D_SCONV
ALPHA = (2 * DEPTH) ** 0.25
LN_EPS = 1e-5

SUBLANES = 8
VMEM_LIMIT = 56 * 1024 * 1024

F32 = jnp.float32
BF16 = jnp.bfloat16


def _cparams(sem):
    return pltpu.CompilerParams(dimension_semantics=sem, vmem_limit_bytes=VMEM_LIMIT)


def _layer_norm(x, g, b):
    mu = jnp.mean(x, axis=-1, keepdims=True)
    xc = x - mu
    var = jnp.mean(xc * xc, axis=-1, keepdims=True)
    return xc * lax.rsqrt(var + LN_EPS) * g + b


def _dot(a, b):
    return jnp.dot(a, b, preferred_element_type=F32)


def _causal_conv3(cur, hist_ref, w_ref, b_ref, nb, tt):
    outs = []
    for s in range(nb):
        v = cur[s * tt:(s + 1) * tt]
        ext = jnp.concatenate([hist_ref[s], v], axis=0)
        m1 = pltpu.roll(ext, 1, 0)[SUBLANES:]
        m2 = pltpu.roll(ext, 2, 0)[SUBLANES:]
        hist_ref[s] = v[tt - SUBLANES:tt]
        outs.append(w_ref[0:1, :] * m2 + w_ref[1:2, :] * m1 + w_ref[2:3, :] * v + b_ref[...])
    return outs[0] if nb == 1 else jnp.concatenate(outs, axis=0)


def _gelu_tanh(x):
    c = math.sqrt(2.0 / math.pi)
    return 0.5 * x * (1.0 + jnp.tanh(c * (x + 0.044715 * (x * x * x))))


def _in_ssm_kernel(x_ref, gin_ref, bin_ref, w_ref, scw_ref, scb_ref, sc0_ref, h0_ref,
                   lbar_ref, bblk_ref, cblk_ref, d_ref, wglu_ref,
                   qkv_ref, kt_ref, vt_ref, ssm_ref, sc_ref, scn_ref, hn_ref,
                   hist_ref, h_ref, st_ref, *, tt, ln_in):
    ti = pl.program_id(1)
    nb = SUBLANES
    rows = nb * tt

    @pl.when(ti == 0)
    def _():
        hist_ref[...] = sc0_ref[...]
        h_ref[...] = h0_ref[...]

    x = x_ref[...].reshape(rows, D_MODEL)
    if ln_in:
        x = _layer_norm(x, gin_ref[...], bin_ref[...])
    xb = x.astype(BF16)
    u = _dot(xb, w_ref[:, D_QKV:D_QKV + D_SSM])
    ut = jnp.swapaxes(u.reshape(nb, tt, D_SSM), 0, 1).reshape(rows, D_SSM)
    bu = _dot(ut.astype(BF16), bblk_ref[...])

    q = _dot(xb, w_ref[:, 0:D_ATT]) * (HEAD_DIM ** -0.5)
    qkv_ref[:, :, 0:D_ATT] = q.astype(BF16).reshape(nb, tt, D_ATT)
    for tail_ref, lo in ((kt_ref, D_ATT), (vt_ref, 2 * D_ATT)):
        kv = _dot(xb, w_ref[:, lo:lo + D_ATT])
        qkv_ref[:, :, lo:lo + D_ATT] = kv.astype(BF16).reshape(nb, tt, D_ATT)
        tail_ref[...] = kv.reshape(nb, tt, D_ATT)

    g = _dot(xb, w_ref[:, D_QKV + D_SSM:D_QKV + D_REST])
    gate_b = g[:, 0:D_SCONV]
    gate_c = g[:, D_SCONV:2 * D_SCONV]
    xv = g[:, 2 * D_SCONV:3 * D_SCONV]
    conv = _causal_conv3(gate_c * xv, hist_ref, scw_ref, scb_ref, nb, tt)
    sc_ref[...] = (gate_b * conv).astype(BF16).reshape(nb, tt, D_SCONV)
    scn_ref[...] = hist_ref[...]

    lr = jnp.broadcast_to(lbar_ref[0:1, :], (SUBLANES, D_STATE))
    li = jnp.broadcast_to(lbar_ref[1:2, :], (SUBLANES, D_STATE))
    hr = h_ref[:, 0:D_STATE]
    hi = h_ref[:, D_STATE:2 * D_STATE]
    for t in range(tt):
        r = slice(t * SUBLANES, (t + 1) * SUBLANES)
        hr, hi = (lr * hr - li * hi + bu[r, 0:D_STATE],
                  lr * hi + li * hr + bu[r, D_STATE:2 * D_STATE])
        st_ref[r, 0:D_STATE] = hr
        st_ref[r, D_STATE:2 * D_STATE] = hi
    y = _dot(st_ref[...].astype(BF16), cblk_ref[...]) + d_ref[...] * ut
    z = _gelu_tanh(y)
    gate = jax.nn.sigmoid(_dot(z.astype(BF16), wglu_ref[...]))
    h_ref[:, 0:D_STATE] = hr
    h_ref[:, D_STATE:2 * D_STATE] = hi
    hn_ref[:, 0:D_STATE] = hr
    hn_ref[:, D_STATE:2 * D_STATE] = hi
    out = (z * gate).reshape(tt, nb, D_SSM)
    ssm_ref[...] = jnp.swapaxes(out, 0, 1).astype(BF16)


def _in_ssm_call(x, ln_in, w_in, scw, scb, sc0, h0, lbar, bblk, cblk, d, wglu, tt, keep):
    apply_ln, g_in, b_in = ln_in
    bsz, s, _ = x.shape
    nb = SUBLANES
    nt = s // tt
    tail_first = nt - keep // tt
    blk = lambda b, t: (b, t, 0)
    tail = lambda b, t: (b, jnp.maximum(t - tail_first, 0), 0)
    bonly3 = lambda b, t: (b, 0, 0)
    bonly2 = lambda b, t: (b, 0)
    const2 = lambda b, t: (0, 0)
    return pl.pallas_call(
        functools.partial(_in_ssm_kernel, tt=tt, ln_in=apply_ln),
        grid=(bsz // nb, nt),
        in_specs=[pl.BlockSpec((nb, tt, D_MODEL), blk),
                  pl.BlockSpec((1, D_MODEL), const2),
                  pl.BlockSpec((1, D_MODEL), const2),
                  pl.BlockSpec((D_MODEL, D_QKV + D_REST), const2),
                  pl.BlockSpec((CONV_K, D_SCONV), const2),
                  pl.BlockSpec((1, D_SCONV), const2),
                  pl.BlockSpec((nb, SUBLANES, D_SCONV), bonly3),
                  pl.BlockSpec((nb, 2 * D_STATE), bonly2),
                  pl.BlockSpec((2, D_STATE), const2),
                  pl.BlockSpec((D_SSM, 2 * D_STATE), const2),
                  pl.BlockSpec((2 * D_STATE, D_SSM), const2),
                  pl.BlockSpec((1, D_SSM), const2),
                  pl.BlockSpec((D_SSM, D_SSM), const2)],
        out_specs=[pl.BlockSpec((nb, tt, D_QKV), blk),
                   pl.BlockSpec((nb, tt, D_ATT), tail),
                   pl.BlockSpec((nb, tt, D_ATT), tail),
                   pl.BlockSpec((nb, tt, D_SSM), blk),
                   pl.BlockSpec((nb, tt, D_SCONV), blk),
                   pl.BlockSpec((nb, SUBLANES, D_SCONV), bonly3),
                   pl.BlockSpec((nb, 2 * D_STATE), bonly2)],
        out_shape=[jax.ShapeDtypeStruct((bsz, s, D_QKV), BF16),
                   jax.ShapeDtypeStruct((bsz, keep, D_ATT), F32),
                   jax.ShapeDtypeStruct((bsz, keep, D_ATT), F32),
                   jax.ShapeDtypeStruct((bsz, s, D_SSM), BF16),
                   jax.ShapeDtypeStruct((bsz, s, D_SCONV), BF16),
                   jax.ShapeDtypeStruct((bsz, SUBLANES, D_SCONV), F32),
                   jax.ShapeDtypeStruct((bsz, 2 * D_STATE), F32)],
        scratch_shapes=[pltpu.VMEM((nb, SUBLANES, D_SCONV), F32),
                        pltpu.VMEM((nb, 2 * D_STATE), F32),
                        pltpu.VMEM((nb * tt, 2 * D_STATE), F32)],
        compiler_params=_cparams(("parallel", "arbitrary")),
        name="in_ssm",
    )(x, g_in, b_in, w_in, scw, scb, sc0, h0, lbar, bblk, cblk, d, wglu)


HEADS_PER_GROUP = 4
GROUP_W = HEADS_PER_GROUP * HEAD_DIM
N_GROUPS = N_HEADS // HEADS_PER_GROUP
Q_STEP = 4 * CHUNK
K_WIN = PAST + Q_STEP


def _softmax_unnormalised(sc):
    m = jnp.max(sc, axis=-1, keepdims=True)
    e = jnp.exp(sc - m)
    return e.astype(BF16), jnp.sum(e, axis=-1, keepdims=True)


def _group_head_of_lane(shape):
    lane = lax.broadcasted_iota(jnp.int32, shape, len(shape) - 1)
    return (lane // HEAD_DIM) % HEADS_PER_GROUP


def _softmax_banded(sc, bias):
    half, skip = Q_STEP // 2, 2 * CHUNK
    zeros = jnp.zeros((half, skip), BF16)
    e_top, l_top = _softmax_unnormalised(sc[0:half, 0:K_WIN - skip] + bias[0:half, 0:K_WIN - skip])
    e_bot, l_bot = _softmax_unnormalised(sc[half:Q_STEP, skip:K_WIN] + bias[half:Q_STEP, skip:K_WIN])
    e = jnp.concatenate([jnp.concatenate([e_top, zeros], axis=1),
                         jnp.concatenate([zeros, e_bot], axis=1)], axis=0)
    return e, jnp.concatenate([l_top, l_bot], axis=0)


def _attn_group(q4, k4, v_of_head, bias_of_head):
    head = _group_head_of_lane(q4.shape)
    full_window = q4.shape[0] == Q_STEP and k4.shape[0] == K_WIN
    acc = None
    scale = None
    for h in range(HEADS_PER_GROUP):
        qm = jnp.where(head == h, q4, jnp.zeros_like(q4))
        sc = lax.dot_general(qm, k4, (((1,), (1,)), ((), ())), preferred_element_type=F32)
        if full_window:
            e, l = _softmax_banded(sc, bias_of_head(h))
        else:
            e, l = _softmax_unnormalised(sc + bias_of_head(h))
        pv = _dot(e, v_of_head(h))
        acc = pv if acc is None else acc + pv
        inv = 1.0 / l
        scale = inv if scale is None else jnp.where(head == h, inv, scale)
    return acc * scale


def _attn_prompt_kernel(qkv_ref, bias_ref, o_ref, vmask_ref, *, s):
    v = qkv_ref[:, 2 * D_ATT:3 * D_ATT]
    head = _group_head_of_lane(v.shape)
    for h in range(HEADS_PER_GROUP):
        vmask_ref[h] = jnp.where(head == h, v, jnp.zeros_like(v))

    def step(r0, k0, nk):
        for g in range(N_GROUPS):
            cols = slice(g * GROUP_W, (g + 1) * GROUP_W)
            kcols = slice(D_ATT + g * GROUP_W, D_ATT + (g + 1) * GROUP_W)
            out = _attn_group(
                qkv_ref[pl.ds(r0, Q_STEP), cols],
                qkv_ref[pl.ds(k0, nk), kcols],
                lambda h: vmask_ref[h, pl.ds(k0, nk), cols],
                lambda h: bias_ref[g * HEADS_PER_GROUP + h, :, K_WIN - nk:K_WIN])
            o_ref[pl.ds(r0, Q_STEP), cols] = out.astype(BF16)

    n_steps = s // Q_STEP
    n_head_steps = min(PAST // Q_STEP, n_steps)
    for j in range(n_head_steps):
        step(j * Q_STEP, 0, (j + 1) * Q_STEP)

    def body(j, carry):
        r0 = pl.multiple_of(j * Q_STEP, Q_STEP)
        step(r0, pl.multiple_of(r0 - PAST, Q_STEP), K_WIN)
        return carry

    if n_steps > n_head_steps:
        lax.fori_loop(n_head_steps, n_steps, body, 0)


def _attn_prompt_call(qkv, bias):
    bsz, s, _ = qkv.shape
    assert s % Q_STEP == 0
    bsel = lambda b: (b, 0, 0)
    return pl.pallas_call(
        functools.partial(_attn_prompt_kernel, s=s),
        grid=(bsz,),
        in_specs=[pl.BlockSpec((None, s, D_QKV), bsel),
                  pl.BlockSpec((N_HEADS, Q_STEP, K_WIN), lambda b: (0, 0, 0))],
        out_specs=pl.BlockSpec((None, s, D_ATT), bsel),
        out_shape=jax.ShapeDtypeStruct((bsz, s, D_ATT), BF16),
        scratch_shapes=[pltpu.VMEM((HEADS_PER_GROUP, s, D_ATT), BF16)],
        compiler_params=_cparams(("parallel",)),
        name="attention_prompt",
    )(qkv, bias)


def _attn_sample_kernel(qkv_ref, pkt_ref, pvt_ref, bias_ref, o_ref, *, s):
    contract_last = (((1,), (1,)), ((), ()))
    for g in range(N_GROUPS):
        cols = slice(g * GROUP_W, (g + 1) * GROUP_W)
        q4 = qkv_ref[:, cols]
        k_new = qkv_ref[:, D_ATT + g * GROUP_W:D_ATT + (g + 1) * GROUP_W]
        v_new = qkv_ref[:, 2 * D_ATT + g * GROUP_W:2 * D_ATT + (g + 1) * GROUP_W]
        kt_past = pkt_ref[cols, :].astype(BF16)
        vt_past = pvt_ref[cols, :].astype(BF16)
        head = _group_head_of_lane(q4.shape)
        out = None
        scale = None
        for h in range(HEADS_PER_GROUP):
            qm = jnp.where(head == h, q4, jnp.zeros_like(q4))
            bias = bias_ref[g * HEADS_PER_GROUP + h]
            sc_past = _dot(qm, kt_past) + bias[:, 0:PAST]
            sc_new = lax.dot_general(qm, k_new, contract_last, preferred_element_type=F32) + bias[:, PAST:]
            m = jnp.maximum(jnp.max(sc_past, axis=-1, keepdims=True), jnp.max(sc_new, axis=-1, keepdims=True))
            e_past = jnp.exp(sc_past - m)
            e_new = jnp.exp(sc_new - m)
            l = jnp.sum(e_past, axis=-1, keepdims=True) + jnp.sum(e_new, axis=-1, keepdims=True)
            pv = (lax.dot_general(e_past.astype(BF16), vt_past, contract_last, preferred_element_type=F32)
                  + _dot(e_new.astype(BF16), v_new))
            inv = 1.0 / l
            out = pv if out is None else jnp.where(head == h, pv, out)
            scale = inv if scale is None else jnp.where(head == h, inv, scale)
        o_ref[:, cols] = (out * scale).astype(BF16)


def _attn_sample_call(qkv, pkt, pvt, bias):
    bsz, s, _ = qkv.shape
    band = PAST + s
    bsel = lambda b: (b, 0, 0)
    return pl.pallas_call(
        functools.partial(_attn_sample_kernel, s=s),
        grid=(bsz,),
        in_specs=[pl.BlockSpec((None, s, D_QKV), bsel),
                  pl.BlockSpec((None, D_ATT, PAST), bsel),
                  pl.BlockSpec((None, D_ATT, PAST), bsel),
                  pl.BlockSpec((N_HEADS, s, band), lambda b: (0, 0, 0))],
        out_specs=pl.BlockSpec((None, s, D_ATT), bsel),
        out_shape=jax.ShapeDtypeStruct((bsz, s, D_ATT), BF16),
        compiler_params=_cparams(("parallel",)),
        name="attention_sample",
    )(qkv, pkt, pvt, bias)


def _mix_ffn_kernel(att_ref, ssm_ref, sc_ref, x_ref, gin_ref, bin_ref, wo_ref, g1_ref, b1_ref,
                    w1_ref, cw_ref, cb_ref, c0_ref, w2_ref, g2_ref, b2_ref,
                    o_ref, cn_ref, hist_ref, *, nb, tt, ln_in):
    ti = pl.program_id(1)
    rows = nb * tt

    @pl.when(ti == 0)
    def _():
        hist_ref[...] = c0_ref[...]

    x = x_ref[...].reshape(rows, D_MODEL)
    if ln_in:
        x = _layer_norm(x, gin_ref[...], bin_ref[...])
    mix = _dot(att_ref[...].reshape(rows, D_ATT), wo_ref[0:D_ATT, :])
    mix = mix + _dot(ssm_ref[...].reshape(rows, D_SSM), wo_ref[D_ATT:D_ATT + D_SSM, :])
    mix = mix + _dot(sc_ref[...].reshape(rows, D_SCONV), wo_ref[D_ATT + D_SSM:D_MODEL, :])
    x1 = _layer_norm(ALPHA * x + mix, g1_ref[...], b1_ref[...])

    up = _dot(x1.astype(BF16), w1_ref[...])
    conv = _causal_conv3(up, hist_ref, cw_ref, cb_ref, nb, tt)
    cn_ref[...] = hist_ref[...]
    gate = conv[:, 0:D_FF]
    val = conv[:, D_FF:2 * D_FF]
    h = (gate * jax.nn.sigmoid(gate) * val).astype(BF16)
    y = _dot(h, w2_ref[...])
    o_ref[...] = _layer_norm(ALPHA * x1 + y, g2_ref[...], b2_ref[...]).reshape(nb, tt, D_MODEL)


def _mix_ffn_call(att, ssm, sc, x, ln_in, w_out, g1, b1, w1, cw, cb, c0, w2, g2, b2, nb, tt):
    apply_ln, g_in, b_in = ln_in
    bsz, s, _ = x.shape
    blk = lambda b, t: (b, t, 0)
    bonly = lambda b, t: (b, 0, 0)
    const2 = lambda b, t: (0, 0)
    vec = pl.BlockSpec((1, D_MODEL), const2)
    return pl.pallas_call(
        functools.partial(_mix_ffn_kernel, nb=nb, tt=tt, ln_in=apply_ln),
        grid=(bsz // nb, s // tt),
        in_specs=[pl.BlockSpec((nb, tt, D_ATT), blk),
                  pl.BlockSpec((nb, tt, D_SSM), blk),
                  pl.BlockSpec((nb, tt, D_SCONV), blk),
                  pl.BlockSpec((nb, tt, D_MODEL), blk), vec, vec,
                  pl.BlockSpec((D_MODEL, D_MODEL), const2), vec, vec,
                  pl.BlockSpec((D_MODEL, 2 * D_FF), const2),
                  pl.BlockSpec((CONV_K, 2 * D_FF), const2),
                  pl.BlockSpec((1, 2 * D_FF), const2),
                  pl.BlockSpec((nb, SUBLANES, 2 * D_FF), bonly),
                  pl.BlockSpec((D_FF, D_MODEL), const2), vec, vec],
        out_specs=[pl.BlockSpec((nb, tt, D_MODEL), blk),
                   pl.BlockSpec((nb, SUBLANES, 2 * D_FF), bonly)],
        out_shape=[jax.ShapeDtypeStruct((bsz, s, D_MODEL), F32),
                   jax.ShapeDtypeStruct((bsz, SUBLANES, 2 * D_FF), F32)],
        scratch_shapes=[pltpu.VMEM((nb, SUBLANES, 2 * D_FF), F32)],
        compiler_params=_cparams(("parallel", "arbitrary")),
        name="mix_ffn",
    )(att, ssm, sc, x, g_in, b_in, w_out, g1, b1, w1, cw, cb, c0, w2, g2, b2)


def _rel_bias_table(rel_bias, tq, chunk):
    nk = PAST + tq
    period = nk + tq
    dist = (nk - 1) - jnp.arange(period)
    by_lag = rel_bias[:, jnp.clip(dist, -(CHUNK - 1), REL_MAX) + (CHUNK - 1)].astype(F32)
    rolled = jnp.roll(by_lag, -(tq - 1), axis=-1)
    n_heads = rel_bias.shape[0]
    bias = jnp.tile(rolled, (1, tq))[:, :tq * (period - 1)].reshape(n_heads, tq, period - 1)[:, :, :nk]
    if tq == chunk:
        return bias
    t_idx = np.arange(tq)[:, None]
    s_idx = np.arange(nk)[None, :]
    key_chunk = s_idx // chunk - t_idx // chunk
    in_band = (key_chunk >= 0) & (key_chunk <= N_PAST_CHUNKS)
    return jnp.where(in_band[None], bias, -jnp.inf)


def _ssm_params(lam_re, lam_im, log_dt, b_re, b_im, c_re, c_im):
    lam = lax.complex(lam_re.astype(F32), lam_im.astype(F32))
    dt = jnp.exp(log_dt.astype(F32))[:, None]
    lbar = jnp.exp(lam * dt)
    bbar = ((lbar - 1.0) / lam)[:, :, None] * lax.complex(b_re.astype(F32), b_im.astype(F32))
    eye = jnp.eye(N_SSM_GROUPS, dtype=F32)

    def in_map(m):
        return jnp.einsum('gpc,gh->gchp', m, eye).reshape(D_SSM, D_STATE)

    def out_map(m):
        return jnp.einsum('gcp,gh->gphc', m, eye).reshape(D_STATE, D_SSM)

    bblk = jnp.concatenate([in_map(bbar.real), in_map(bbar.imag)], axis=1).astype(BF16)
    cblk = jnp.concatenate([out_map(c_re.astype(F32)), out_map(-c_im.astype(F32))], axis=0).astype(BF16)
    lbar2 = jnp.stack([lbar.real.reshape(D_STATE), lbar.imag.reshape(D_STATE)], axis=0)
    return lbar2, bblk, cblk


def _pad_rows(buf):
    return jnp.pad(buf.astype(F32), ((0, 0), (SUBLANES - (CONV_K - 1), 0), (0, 0)))


def _trunk_layer(x, ln_in, lp, past, tiles):
    bsz, s, _ = x.shape
    tt_in, nb_ffn, tt_ffn = tiles
    if past is None:
        keep = min(PAST, s)
        h0 = jnp.zeros((bsz, 2 * D_STATE), F32)
        sc0 = jnp.zeros((bsz, SUBLANES, D_SCONV), F32)
        ff0 = jnp.zeros((bsz, SUBLANES, 2 * D_FF), F32)
    else:
        pk, pv, h_re0, h_im0, sconv_buf, ffn_buf = past
        keep = s
        h0 = jnp.concatenate([h_re0.reshape(bsz, D_STATE), h_im0.reshape(bsz, D_STATE)], axis=1).astype(F32)
        sc0 = _pad_rows(sconv_buf)
        ff0 = _pad_rows(ffn_buf)

    qkv, k_new, v_new, ssm_out, sconv_out, sc_new, h_new = _in_ssm_call(
        x, ln_in, lp['w_in'], lp['sconv_w'], lp['sconv_b'], sc0, h0,
        lp['lbar'], lp['bblk'], lp['cblk'], lp['d'], lp['w_glu'], tt_in, keep)
    if past is None:
        att = _attn_prompt_call(qkv, _rel_bias_table(lp['rel_bias'], Q_STEP, CHUNK))
    else:
        att = _attn_sample_call(qkv, pk, pv, _rel_bias_table(lp['rel_bias'], s, s))
    x2, ff_new = _mix_ffn_call(att, ssm_out, sconv_out, x, ln_in, lp['w_out'], lp['ln1_g'], lp['ln1_b'],
                               lp['w_ff_in'], lp['ffn_conv_w'], lp['ffn_conv_b'], ff0,
                               lp['w_ff_out'], lp['ln2_g'], lp['ln2_b'], nb_ffn, tt_ffn)
    new = (k_new, v_new,
           h_new[:, 0:D_STATE].reshape(bsz, N_SSM_GROUPS, SSM_STATE),
           h_new[:, D_STATE:].reshape(bsz, N_SSM_GROUPS, SSM_STATE),
           sc_new[:, SUBLANES - (CONV_K - 1):], ff_new[:, SUBLANES - (CONV_K - 1):])
    return x2, new


ROW_TILE = 512
IN_SSM_POSITIONS = 128


def _tiles(bsz, s):
    tt_in = min(IN_SSM_POSITIONS, s)
    tt_ffn = min(ROW_TILE, s)
    nb_ffn = min(bsz, ROW_TILE // tt_ffn)
    return tt_in, nb_ffn, tt_ffn


def kernel(x_prompt, x_sample, cache_k, cache_v, state_ssm_re, state_ssm_im, cache_sconv, cache_ffn_conv, ln_in_g, ln_in_b, w_in, rel_bias, ssm_lam_re, ssm_lam_im, ssm_log_dt, ssm_b_re, ssm_b_im, ssm_c_re, ssm_c_im, ssm_d, w_glu, sconv_w, sconv_b, w_out, ln1_g, ln1_b, w_ff_in, ffn_conv_w, ffn_conv_b, w_ff_out, ln2_g, ln2_b):
    bp, sp, _ = x_prompt.shape
    bs, ss, _ = x_sample.shape
    g_in = ln_in_g.reshape(1, D_MODEL)
    b_in = ln_in_b.reshape(1, D_MODEL)
    xp, xs = x_prompt, x_sample
    to_feature_major = lambda c: jnp.transpose(c, (0, 1, 3, 4, 2)).reshape(DEPTH, bs, D_ATT, PAST)
    pk_all = to_feature_major(cache_k)
    pv_all = to_feature_major(cache_v)
    st_p = [[] for _ in range(6)]
    st_s = [[] for _ in range(6)]
    for l in range(DEPTH):
        lbar, bblk, cblk = _ssm_params(ssm_lam_re[l], ssm_lam_im[l], ssm_log_dt[l], ssm_b_re[l], ssm_b_im[l],
                                       ssm_c_re[l], ssm_c_im[l])
        lp = {'w_in': w_in[l].astype(BF16), 'rel_bias': rel_bias[l],
              'lbar': lbar, 'bblk': bblk, 'cblk': cblk,
              'd': ssm_d[l].reshape(1, D_SSM), 'w_glu': w_glu[l].astype(BF16),
              'sconv_w': sconv_w[l], 'sconv_b': sconv_b[l].reshape(1, D_SCONV),
              'w_out': w_out[l].astype(BF16),
              'ln1_g': ln1_g[l].reshape(1, D_MODEL), 'ln1_b': ln1_b[l].reshape(1, D_MODEL),
              'w_ff_in': w_ff_in[l].astype(BF16), 'ffn_conv_w': ffn_conv_w[l],
              'ffn_conv_b': ffn_conv_b[l].reshape(1, 2 * D_FF),
              'w_ff_out': w_ff_out[l].astype(BF16),
              'ln2_g': ln2_g[l].reshape(1, D_MODEL), 'ln2_b': ln2_b[l].reshape(1, D_MODEL)}
        ln_in = (l == 0, g_in, b_in)
        xp, new_p = _trunk_layer(xp, ln_in, lp, None, _tiles(bp, sp))
        past = (pk_all[l], pv_all[l], state_ssm_re[l], state_ssm_im[l], cache_sconv[l], cache_ffn_conv[l])
        xs, new_s = _trunk_layer(xs, ln_in, lp, past, _tiles(bs, ss))
        for i in range(6):
            st_p[i].append(new_p[i])
            st_s[i].append(new_s[i])
    k_p, v_p, hre_p, him_p, sc_p, ff_p = [jnp.stack(a, axis=0) for a in st_p]
    k_s, v_s, hre_s, him_s, sc_s, ff_s = [jnp.stack(a, axis=0) for a in st_s]
    heads = lambda a: a.reshape(a.shape[:-1] + (N_HEADS, HEAD_DIM))
    return (xp, xs, heads(k_p), heads(v_p), heads(k_s), heads(v_s),
            hre_p, him_p, hre_s, him_s, sc_p, sc_s, ff_p, ff_s)
```

```python
import functools
import math

import jax
import jax.numpy as jnp
import numpy as np
from jax import lax
from jax.experimental import pallas as pl
from jax.experimental.pallas import tpu as pltpu

D_MODEL = 1024
DEPTH = 4
CHUNK = 64
N_HEADS = 8
HEAD_DIM = 64
D_ATT = N_HEADS * HEAD_DIM
N_PAST_CHUNKS = 8
PAST = N_PAST_CHUNKS * CHUNK
REL_MAX = 128
SSM_GROUP = 16
N_SSM_GROUPS = 16
D_SSM = SSM_GROUP * N_SSM_GROUPS
SSM_STATE = 64
D_STATE = N_SSM_GROUPS * SSM_STATE
D_SCONV = 256
CONV_K = 3
D_FF = 2048
D_QKV = 3 * D_ATT
D_REST = D_SSM + 3 * D_SCONV
ALPHA = (2 * DEPTH) ** 0.25
LN_EPS = 1e-5

SUBLANES = 8
VMEM_LIMIT = 56 * 1024 * 1024

F32 = jnp.float32
BF16 = jnp.bfloat16


def _cparams(sem):
    return pltpu.CompilerParams(dimension_semantics=sem, vmem_limit_bytes=VMEM_LIMIT)


def _layer_norm(x, g, b):
    mu = jnp.mean(x, axis=-1, keepdims=True)
    xc = x - mu
    var = jnp.mean(xc * xc, axis=-1, keepdims=True)
    return xc * lax.rsqrt(var + LN_EPS) * g + b


def _dot(a, b):
    return jnp.dot(a, b, preferred_element_type=F32)


def _layer_spec(shape, layer):
    return pl.BlockSpec((None,) + shape, lambda *_: (layer,) + (0,) * len(shape))


def _causal_conv3(cur, hist_ref, w_ref, b_ref, nb, tt):
    outs = []
    for s in range(nb):
        v = cur[s * tt:(s + 1) * tt]
        ext = jnp.concatenate([hist_ref[s], v], axis=0)
        m1 = pltpu.roll(ext, 1, 0)[SUBLANES:]
        m2 = pltpu.roll(ext, 2, 0)[SUBLANES:]
        hist_ref[s] = v[tt - SUBLANES:tt]
        outs.append(w_ref[0:1, :] * m2 + w_ref[1:2, :] * m1 + w_ref[2:3, :] * v + b_ref[...])
    return outs[0] if nb == 1 else jnp.concatenate(outs, axis=0)


def _gelu_tanh(x):
    c = math.sqrt(2.0 / math.pi)
    return 0.5 * x * (1.0 + jnp.tanh(c * (x + 0.044715 * (x * x * x))))


def _in_ssm_kernel(x_ref, gin_ref, bin_ref, w_ref, scw_ref, scb_ref, sc0_ref, h0_ref,
                   lbar_ref, bblk_ref, cblk_ref, d_ref, wglu_ref,
                   qkv_ref, kt_ref, vt_ref, ssm_ref, sc_ref, scn_ref, hn_ref,
                   hist_ref, h_ref, st_ref, *, tt, ln_in):
    ti = pl.program_id(1)
    nb = SUBLANES
    rows = nb * tt

    @pl.when(ti == 0)
    def _():
        hist_ref[...] = sc0_ref[...]
        h_ref[...] = h0_ref[...]

    x = x_ref[...].reshape(rows, D_MODEL)
    if ln_in:
        x = _layer_norm(x, gin_ref[...], bin_ref[...])
    xb = x.astype(BF16)
    u = _dot(xb, w_ref[:, D_QKV:D_QKV + D_SSM])
    ut = jnp.swapaxes(u.reshape(nb, tt, D_SSM), 0, 1).reshape(rows, D_SSM)
    bu = _dot(ut.astype(BF16), bblk_ref[...])

    q = _dot(xb, w_ref[:, 0:D_ATT]) * (HEAD_DIM ** -0.5)
    qkv_ref[:, :, 0:D_ATT] = q.astype(BF16).reshape(nb, tt, D_ATT)
    for tail_ref, lo in ((kt_ref, D_ATT), (vt_ref, 2 * D_ATT)):
        kv = _dot(xb, w_ref[:, lo:lo + D_ATT])
        qkv_ref[:, :, lo:lo + D_ATT] = kv.astype(BF16).reshape(nb, tt, D_ATT)
        tail_ref[...] = kv.reshape(nb, tt, D_ATT)

    g = _dot(xb, w_ref[:, D_QKV + D_SSM:D_QKV + D_REST])
    gate_b = g[:, 0:D_SCONV]
    gate_c = g[:, D_SCONV:2 * D_SCONV]
    xv = g[:, 2 * D_SCONV:3 * D_SCONV]
    conv = _causal_conv3(gate_c * xv, hist_ref, scw_ref, scb_ref, nb, tt)
    sc_ref[...] = (gate_b * conv).astype(BF16).reshape(nb, tt, D_SCONV)
    scn_ref[...] = hist_ref[...]

    lr = jnp.broadcast_to(lbar_ref[0:1, :], (SUBLANES, D_STATE))
    li = jnp.broadcast_to(lbar_ref[1:2, :], (SUBLANES, D_STATE))
    hr = h_ref[:, 0:D_STATE]
    hi = h_ref[:, D_STATE:2 * D_STATE]
    for t in range(tt):
        r = slice(t * SUBLANES, (t + 1) * SUBLANES)
        hr, hi = (lr * hr - li * hi + bu[r, 0:D_STATE],
                  lr * hi + li * hr + bu[r, D_STATE:2 * D_STATE])
        st_ref[r, 0:D_STATE] = hr
        st_ref[r, D_STATE:2 * D_STATE] = hi
    y = _dot(st_ref[...].astype(BF16), cblk_ref[...]) + d_ref[...] * ut
    z = _gelu_tanh(y)
    gate = jax.nn.sigmoid(_dot(z.astype(BF16), wglu_ref[...]))
    h_ref[:, 0:D_STATE] = hr
    h_ref[:, D_STATE:2 * D_STATE] = hi
    hn_ref[:, 0:D_STATE] = hr
    hn_ref[:, D_STATE:2 * D_STATE] = hi
    out = (z * gate).reshape(tt, nb, D_SSM)
    ssm_ref[...] = jnp.swapaxes(out, 0, 1).astype(BF16)


def _in_ssm_call(x, ln_in, layer, w_in, scw, scb, sc0, h0, lbar, bblk, cblk, d, wglu, tt, keep):
    apply_ln, g_in, b_in = ln_in
    bsz, s, _ = x.shape
    nb = SUBLANES
    nt = s // tt
    tail_first = nt - keep // tt
    blk = lambda b, t: (b, t, 0)
    tail = lambda b, t: (b, jnp.maximum(t - tail_first, 0), 0)
    bonly3 = lambda b, t: (b, 0, 0)
    bonly2 = lambda b, t: (b, 0)
    const2 = lambda b, t: (0, 0)
    return pl.pallas_call(
        functools.partial(_in_ssm_kernel, tt=tt, ln_in=apply_ln),
        grid=(bsz // nb, nt),
        in_specs=[pl.BlockSpec((nb, tt, D_MODEL), blk),
                  pl.BlockSpec((1, D_MODEL), const2),
                  pl.BlockSpec((1, D_MODEL), const2),
                  _layer_spec((D_MODEL, D_QKV + D_REST), layer),
                  pl.BlockSpec((CONV_K, D_SCONV), const2),
                  pl.BlockSpec((1, D_SCONV), const2),
                  pl.BlockSpec((nb, SUBLANES, D_SCONV), bonly3),
                  pl.BlockSpec((nb, 2 * D_STATE), bonly2),
                  pl.BlockSpec((2, D_STATE), const2),
                  pl.BlockSpec((D_SSM, 2 * D_STATE), const2),
                  pl.BlockSpec((2 * D_STATE, D_SSM), const2),
                  pl.BlockSpec((1, D_SSM), const2),
                  pl.BlockSpec((D_SSM, D_SSM), const2)],
        out_specs=[pl.BlockSpec((nb, tt, D_QKV), blk),
                   pl.BlockSpec((nb, tt, D_ATT), tail),
                   pl.BlockSpec((nb, tt, D_ATT), tail),
                   pl.BlockSpec((nb, tt, D_SSM), blk),
                   pl.BlockSpec((nb, tt, D_SCONV), blk),
                   pl.BlockSpec((nb, SUBLANES, D_SCONV), bonly3),
                   pl.BlockSpec((nb, 2 * D_STATE), bonly2)],
        out_shape=[jax.ShapeDtypeStruct((bsz, s, D_QKV), BF16),
                   jax.ShapeDtypeStruct((bsz, keep, D_ATT), F32),
                   jax.ShapeDtypeStruct((bsz, keep, D_ATT), F32),
                   jax.ShapeDtypeStruct((bsz, s, D_SSM), BF16),
                   jax.ShapeDtypeStruct((bsz, s, D_SCONV), BF16),
                   jax.ShapeDtypeStruct((bsz, SUBLANES, D_SCONV), F32),
                   jax.ShapeDtypeStruct((bsz, 2 * D_STATE), F32)],
        scratch_shapes=[pltpu.VMEM((nb, SUBLANES, D_SCONV), F32),
                        pltpu.VMEM((nb, 2 * D_STATE), F32),
                        pltpu.VMEM((nb * tt, 2 * D_STATE), F32)],
        compiler_params=_cparams(("parallel", "arbitrary")),
        name="in_ssm",
    )(x, g_in, b_in, w_in, scw, scb, sc0, h0, lbar, bblk, cblk, d, wglu)


HEADS_PER_GROUP = 4
GROUP_W = HEADS_PER_GROUP * HEAD_DIM
N_GROUPS = N_HEADS // HEADS_PER_GROUP
Q_STEP = 4 * CHUNK
K_WIN = PAST + Q_STEP


def _softmax_unnormalised(sc):
    m = jnp.max(sc, axis=-1, keepdims=True)
    e = jnp.exp(sc - m)
    return e.astype(BF16), jnp.sum(e, axis=-1, keepdims=True)


def _group_head_of_lane(shape):
    lane = lax.broadcasted_iota(jnp.int32, shape, len(shape) - 1)
    return (lane // HEAD_DIM) % HEADS_PER_GROUP


def _softmax_banded(sc, bias):
    half, skip = Q_STEP // 2, 2 * CHUNK
    zeros = jnp.zeros((half, skip), BF16)
    e_top, l_top = _softmax_unnormalised(sc[0:half, 0:K_WIN - skip] + bias[0:half, 0:K_WIN - skip])
    e_bot, l_bot = _softmax_unnormalised(sc[half:Q_STEP, skip:K_WIN] + bias[half:Q_STEP, skip:K_WIN])
    e = jnp.concatenate([jnp.concatenate([e_top, zeros], axis=1),
                         jnp.concatenate([zeros, e_bot], axis=1)], axis=0)
    return e, jnp.concatenate([l_top, l_bot], axis=0)


def _attn_group(q4, k4, v_of_head, bias_of_head):
    head = _group_head_of_lane(q4.shape)
    full_window = q4.shape[0] == Q_STEP and k4.shape[0] == K_WIN
    acc = None
    scale = None
    for h in range(HEADS_PER_GROUP):
        qm = jnp.where(head == h, q4, jnp.zeros_like(q4))
        sc = lax.dot_general(qm, k4, (((1,), (1,)), ((), ())), preferred_element_type=F32)
        if full_window:
            e, l = _softmax_banded(sc, bias_of_head(h))
        else:
            e, l = _softmax_unnormalised(sc + bias_of_head(h))
        pv = _dot(e, v_of_head(h))
        acc = pv if acc is None else acc + pv
        inv = 1.0 / l
        scale = inv if scale is None else jnp.where(head == h, inv, scale)
    return acc * scale


def _attn_prompt_kernel(qkv_ref, bias_ref, o_ref, vmask_ref, *, s):
    v = qkv_ref[:, 2 * D_ATT:3 * D_ATT]
    head = _group_head_of_lane(v.shape)
    for h in range(HEADS_PER_GROUP):
        vmask_ref[h] = jnp.where(head == h, v, jnp.zeros_like(v))

    def step(r0, k0, nk):
        for g in range(N_GROUPS):
            cols = slice(g * GROUP_W, (g + 1) * GROUP_W)
            kcols = slice(D_ATT + g * GROUP_W, D_ATT + (g + 1) * GROUP_W)
            out = _attn_group(
                qkv_ref[pl.ds(r0, Q_STEP), cols],
                qkv_ref[pl.ds(k0, nk), kcols],
                lambda h: vmask_ref[h, pl.ds(k0, nk), cols],
                lambda h: bias_ref[g * HEADS_PER_GROUP + h, :, K_WIN - nk:K_WIN])
            o_ref[pl.ds(r0, Q_STEP), cols] = out.astype(BF16)

    n_steps = s // Q_STEP
    n_head_steps = min(PAST // Q_STEP, n_steps)
    for j in range(n_head_steps):
        step(j * Q_STEP, 0, (j + 1) * Q_STEP)

    def body(j, carry):
        r0 = pl.multiple_of(j * Q_STEP, Q_STEP)
        step(r0, pl.multiple_of(r0 - PAST, Q_STEP), K_WIN)
        return carry

    if n_steps > n_head_steps:
        lax.fori_loop(n_head_steps, n_steps, body, 0)


def _attn_prompt_call(qkv, bias):
    bsz, s, _ = qkv.shape
    assert s % Q_STEP == 0
    bsel = lambda b: (b, 0, 0)
    return pl.pallas_call(
        functools.partial(_attn_prompt_kernel, s=s),
        grid=(bsz,),
        in_specs=[pl.BlockSpec((None, s, D_QKV), bsel),
                  pl.BlockSpec((N_HEADS, Q_STEP, K_WIN), lambda b: (0, 0, 0))],
        out_specs=pl.BlockSpec((None, s, D_ATT), bsel),
        out_shape=jax.ShapeDtypeStruct((bsz, s, D_ATT), BF16),
        scratch_shapes=[pltpu.VMEM((HEADS_PER_GROUP, s, D_ATT), BF16)],
        compiler_params=_cparams(("parallel",)),
        name="attention_prompt",
    )(qkv, bias)


def _attn_sample_kernel(qkv_ref, pkt_ref, pvt_ref, bias_ref, o_ref, *, s):
    contract_last = (((1,), (1,)), ((), ()))
    for g in range(N_GROUPS):
        cols = slice(g * GROUP_W, (g + 1) * GROUP_W)
        q4 = qkv_ref[:, cols]
        k_new = qkv_ref[:, D_ATT + g * GROUP_W:D_ATT + (g + 1) * GROUP_W]
        v_new = qkv_ref[:, 2 * D_ATT + g * GROUP_W:2 * D_ATT + (g + 1) * GROUP_W]
        kt_past = pkt_ref[cols, :].astype(BF16)
        vt_past = pvt_ref[cols, :].astype(BF16)
        head = _group_head_of_lane(q4.shape)
        out = None
        scale = None
        for h in range(HEADS_PER_GROUP):
            qm = jnp.where(head == h, q4, jnp.zeros_like(q4))
            bias = bias_ref[g * HEADS_PER_GROUP + h]
            sc_past = _dot(qm, kt_past) + bias[:, 0:PAST]
            sc_new = lax.dot_general(qm, k_new, contract_last, preferred_element_type=F32) + bias[:, PAST:]
            m = jnp.maximum(jnp.max(sc_past, axis=-1, keepdims=True), jnp.max(sc_new, axis=-1, keepdims=True))
            e_past = jnp.exp(sc_past - m)
            e_new = jnp.exp(sc_new - m)
            l = jnp.sum(e_past, axis=-1, keepdims=True) + jnp.sum(e_new, axis=-1, keepdims=True)
            pv = (lax.dot_general(e_past.astype(BF16), vt_past, contract_last, preferred_element_type=F32)
                  + _dot(e_new.astype(BF16), v_new))
            inv = 1.0 / l
            out = pv if out is None else jnp.where(head == h, pv, out)
            scale = inv if scale is None else jnp.where(head == h, inv, scale)
        o_ref[:, cols] = (out * scale).astype(BF16)


def _attn_sample_call(qkv, layer, pkt, pvt, bias):
    bsz, s, _ = qkv.shape
    band = PAST + s
    bsel = lambda b: (b, 0, 0)
    return pl.pallas_call(
        functools.partial(_attn_sample_kernel, s=s),
        grid=(bsz,),
        in_specs=[pl.BlockSpec((None, s, D_QKV), bsel),
                  pl.BlockSpec((None, None, D_ATT, PAST), lambda b: (layer, b, 0, 0)),
                  pl.BlockSpec((None, None, D_ATT, PAST), lambda b: (layer, b, 0, 0)),
                  pl.BlockSpec((N_HEADS, s, band), lambda b: (0, 0, 0))],
        out_specs=pl.BlockSpec((None, s, D_ATT), bsel),
        out_shape=jax.ShapeDtypeStruct((bsz, s, D_ATT), BF16),
        compiler_params=_cparams(("parallel",)),
        name="attention_sample",
    )(qkv, pkt, pvt, bias)


def _mix_ffn_kernel(att_ref, ssm_ref, sc_ref, x_ref, gin_ref, bin_ref, wo_ref, g1_ref, b1_ref,
                    w1_ref, cw_ref, cb_ref, c0_ref, w2_ref, g2_ref, b2_ref,
                    o_ref, cn_ref, hist_ref, *, nb, tt, ln_in):
    ti = pl.program_id(1)
    rows = nb * tt

    @pl.when(ti == 0)
    def _():
        hist_ref[...] = c0_ref[...]

    x = x_ref[...].reshape(rows, D_MODEL)
    if ln_in:
        x = _layer_norm(x, gin_ref[...], bin_ref[...])
    mix = _dot(att_ref[...].reshape(rows, D_ATT), wo_ref[0:D_ATT, :])
    mix = mix + _dot(ssm_ref[...].reshape(rows, D_SSM), wo_ref[D_ATT:D_ATT + D_SSM, :])
    mix = mix + _dot(sc_ref[...].reshape(rows, D_SCONV), wo_ref[D_ATT + D_SSM:D_MODEL, :])
    x1 = _layer_norm(ALPHA * x + mix, g1_ref[...], b1_ref[...])

    up = _dot(x1.astype(BF16), w1_ref[...])
    conv = _causal_conv3(up, hist_ref, cw_ref, cb_ref, nb, tt)
    cn_ref[...] = hist_ref[...]
    gate = conv[:, 0:D_FF]
    val = conv[:, D_FF:2 * D_FF]
    h = (gate * jax.nn.sigmoid(gate) * val).astype(BF16)
    y = _dot(h, w2_ref[...])
    o_ref[...] = _layer_norm(ALPHA * x1 + y, g2_ref[...], b2_ref[...]).reshape(nb, tt, D_MODEL)


def _mix_ffn_call(att, ssm, sc, x, ln_in, layer, w_out, g1, b1, w1, cw, cb, c0, w2, g2, b2, nb, tt):
    apply_ln, g_in, b_in = ln_in
    bsz, s, _ = x.shape
    blk = lambda b, t: (b, t, 0)
    bonly = lambda b, t: (b, 0, 0)
    const2 = lambda b, t: (0, 0)
    vec = pl.BlockSpec((1, D_MODEL), const2)
    return pl.pallas_call(
        functools.partial(_mix_ffn_kernel, nb=nb, tt=tt, ln_in=apply_ln),
        grid=(bsz // nb, s // tt),
        in_specs=[pl.BlockSpec((nb, tt, D_ATT), blk),
                  pl.BlockSpec((nb, tt, D_SSM), blk),
                  pl.BlockSpec((nb, tt, D_SCONV), blk),
                  pl.BlockSpec((nb, tt, D_MODEL), blk), vec, vec,
                  _layer_spec((D_MODEL, D_MODEL), layer), vec, vec,
                  _layer_spec((D_MODEL, 2 * D_FF), layer),
                  pl.BlockSpec((CONV_K, 2 * D_FF), const2),
                  pl.BlockSpec((1, 2 * D_FF), const2),
                  pl.BlockSpec((nb, SUBLANES, 2 * D_FF), bonly),
                  _layer_spec((D_FF, D_MODEL), layer), vec, vec],
        out_specs=[pl.BlockSpec((nb, tt, D_MODEL), blk),
                   pl.BlockSpec((nb, SUBLANES, 2 * D_FF), bonly)],
        out_shape=[jax.ShapeDtypeStruct((bsz, s, D_MODEL), F32),
                   jax.ShapeDtypeStruct((bsz, SUBLANES, 2 * D_FF), F32)],
        scratch_shapes=[pltpu.VMEM((nb, SUBLANES, 2 * D_FF), F32)],
        compiler_params=_cparams(("parallel", "arbitrary")),
        name="mix_ffn",
    )(att, ssm, sc, x, g_in, b_in, w_out, g1, b1, w1, cw, cb, c0, w2, g2, b2)


def _rel_bias_table(rel_bias, tq, chunk):
    nk = PAST + tq
    period = nk + tq
    dist = (nk - 1) - jnp.arange(period)
    by_lag = rel_bias[:, jnp.clip(dist, -(CHUNK - 1), REL_MAX) + (CHUNK - 1)].astype(F32)
    rolled = jnp.roll(by_lag, -(tq - 1), axis=-1)
    n_heads = rel_bias.shape[0]
    bias = jnp.tile(rolled, (1, tq))[:, :tq * (period - 1)].reshape(n_heads, tq, period - 1)[:, :, :nk]
    if tq == chunk:
        return bias
    t_idx = np.arange(tq)[:, None]
    s_idx = np.arange(nk)[None, :]
    key_chunk = s_idx // chunk - t_idx // chunk
    in_band = (key_chunk >= 0) & (key_chunk <= N_PAST_CHUNKS)
    return jnp.where(in_band[None], bias, -jnp.inf)


def _ssm_params(lam_re, lam_im, log_dt, b_re, b_im, c_re, c_im):
    lam = lax.complex(lam_re.astype(F32), lam_im.astype(F32))
    dt = jnp.exp(log_dt.astype(F32))[:, None]
    lbar = jnp.exp(lam * dt)
    bbar = ((lbar - 1.0) / lam)[:, :, None] * lax.complex(b_re.astype(F32), b_im.astype(F32))
    eye = jnp.eye(N_SSM_GROUPS, dtype=F32)

    def in_map(m):
        return jnp.einsum('gpc,gh->gchp', m, eye).reshape(D_SSM, D_STATE)

    def out_map(m):
        return jnp.einsum('gcp,gh->gphc', m, eye).reshape(D_STATE, D_SSM)

    bblk = jnp.concatenate([in_map(bbar.real), in_map(bbar.imag)], axis=1).astype(BF16)
    cblk = jnp.concatenate([out_map(c_re.astype(F32)), out_map(-c_im.astype(F32))], axis=0).astype(BF16)
    lbar2 = jnp.stack([lbar.real.reshape(D_STATE), lbar.imag.reshape(D_STATE)], axis=0)
    return lbar2, bblk, cblk


def _pad_rows(buf):
    return jnp.pad(buf.astype(F32), ((0, 0), (SUBLANES - (CONV_K - 1), 0), (0, 0)))


def _trunk_layer(x, ln_in, layer, lp, past, tiles):
    bsz, s, _ = x.shape
    tt_in, nb_ffn, tt_ffn = tiles
    if past is None:
        keep = min(PAST, s)
        h0 = jnp.zeros((bsz, 2 * D_STATE), F32)
        sc0 = jnp.zeros((bsz, SUBLANES, D_SCONV), F32)
        ff0 = jnp.zeros((bsz, SUBLANES, 2 * D_FF), F32)
    else:
        pk, pv, h_re0, h_im0, sconv_buf, ffn_buf = past
        keep = s
        h0 = jnp.concatenate([h_re0.reshape(bsz, D_STATE), h_im0.reshape(bsz, D_STATE)], axis=1).astype(F32)
        sc0 = _pad_rows(sconv_buf)
        ff0 = _pad_rows(ffn_buf)

    qkv, k_new, v_new, ssm_out, sconv_out, sc_new, h_new = _in_ssm_call(
        x, ln_in, layer, lp['w_in'], lp['sconv_w'], lp['sconv_b'], sc0, h0,
        lp['lbar'], lp['bblk'], lp['cblk'], lp['d'], lp['w_glu'], tt_in, keep)
    if past is None:
        att = _attn_prompt_call(qkv, _rel_bias_table(lp['rel_bias'], Q_STEP, CHUNK))
    else:
        att = _attn_sample_call(qkv, layer, pk, pv, _rel_bias_table(lp['rel_bias'], s, s))
    x2, ff_new = _mix_ffn_call(att, ssm_out, sconv_out, x, ln_in, layer, lp['w_out'], lp['ln1_g'], lp['ln1_b'],
                               lp['w_ff_in'], lp['ffn_conv_w'], lp['ffn_conv_b'], ff0,
                               lp['w_ff_out'], lp['ln2_g'], lp['ln2_b'], nb_ffn, tt_ffn)
    new = (k_new, v_new,
           h_new[:, 0:D_STATE].reshape(bsz, N_SSM_GROUPS, SSM_STATE),
           h_new[:, D_STATE:].reshape(bsz, N_SSM_GROUPS, SSM_STATE),
           sc_new[:, SUBLANES - (CONV_K - 1):], ff_new[:, SUBLANES - (CONV_K - 1):])
    return x2, new


ROW_TILE = 512
IN_SSM_POSITIONS = 128


def _tiles(bsz, s):
    tt_in = min(IN_SSM_POSITIONS, s)
    tt_ffn = min(ROW_TILE, s)
    nb_ffn = min(bsz, ROW_TILE // tt_ffn)
    return tt_in, nb_ffn, tt_ffn


def kernel(x_prompt, x_sample, cache_k, cache_v, state_ssm_re, state_ssm_im, cache_sconv, cache_ffn_conv, ln_in_g, ln_in_b, w_in, rel_bias, ssm_lam_re, ssm_lam_im, ssm_log_dt, ssm_b_re, ssm_b_im, ssm_c_re, ssm_c_im, ssm_d, w_glu, sconv_w, sconv_b, w_out, ln1_g, ln1_b, w_ff_in, ffn_conv_w, ffn_conv_b, w_ff_out, ln2_g, ln2_b):
    bp, sp, _ = x_prompt.shape
    bs, ss, _ = x_sample.shape
    g_in = ln_in_g.reshape(1, D_MODEL)
    b_in = ln_in_b.reshape(1, D_MODEL)
    xp, xs = x_prompt, x_sample
    to_feature_major = lambda c: jnp.transpose(c, (0, 1, 3, 4, 2)).reshape(DEPTH, bs, D_ATT, PAST)
    pk_all = to_feature_major(cache_k)
    pv_all = to_feature_major(cache_v)
    w_in_b, w_out_b = w_in.astype(BF16), w_out.astype(BF16)
    w_ff_in_b, w_ff_out_b = w_ff_in.astype(BF16), w_ff_out.astype(BF16)
    st_p = [[] for _ in range(6)]
    st_s = [[] for _ in range(6)]
    for l in range(DEPTH):
        lbar, bblk, cblk = _ssm_params(ssm_lam_re[l], ssm_lam_im[l], ssm_log_dt[l], ssm_b_re[l], ssm_b_im[l],
                                       ssm_c_re[l], ssm_c_im[l])
        lp = {'w_in': w_in_b, 'rel_bias': rel_bias[l],
              'lbar': lbar, 'bblk': bblk, 'cblk': cblk,
              'd': ssm_d[l].reshape(1, D_SSM), 'w_glu': w_glu[l].astype(BF16),
              'sconv_w': sconv_w[l], 'sconv_b': sconv_b[l].reshape(1, D_SCONV),
              'w_out': w_out_b,
              'ln1_g': ln1_g[l].reshape(1, D_MODEL), 'ln1_b': ln1_b[l].reshape(1, D_MODEL),
              'w_ff_in': w_ff_in_b, 'ffn_conv_w': ffn_conv_w[l],
              'ffn_conv_b': ffn_conv_b[l].reshape(1, 2 * D_FF),
              'w_ff_out': w_ff_out_b,
              'ln2_g': ln2_g[l].reshape(1, D_MODEL), 'ln2_b': ln2_b[l].reshape(1, D_MODEL)}
        ln_in = (l == 0, g_in, b_in)
        xp, new_p = _trunk_layer(xp, ln_in, l, lp, None, _tiles(bp, sp))
        past = (pk_all, pv_all, state_ssm_re[l], state_ssm_im[l], cache_sconv[l], cache_ffn_conv[l])
        xs, new_s = _trunk_layer(xs, ln_in, l, lp, past, _tiles(bs, ss))
        for i in range(6):
            st_p[i].append(new_p[i])
            st_s[i].append(new_s[i])
    k_p, v_p, hre_p, him_p, sc_p, ff_p = [jnp.stack(a, axis=0) for a in st_p]
    k_s, v_s, hre_s, him_s, sc_s, ff_s = [jnp.stack(a, axis=0) for a in st_s]
    heads = lambda a: a.reshape(a.shape[:-1] + (N_HEADS, HEAD_DIM))
    return (xp, xs, heads(k_p), heads(v_p), heads(k_s), heads(v_s),
            hre_p, him_p, hre_s, him_s, sc_p, sc_s, ff_p, ff_s)
```

```python
import functools
import math

import jax
import jax.numpy as jnp
import numpy as np
from jax import lax
from jax.experimental import pallas as pl
from jax.experimental.pallas import tpu as pltpu

D_MODEL = 1024
DEPTH = 4
CHUNK = 64
N_HEADS = 8
HEAD_DIM = 64
D_ATT = N_HEADS * HEAD_DIM
N_PAST_CHUNKS = 8
PAST = N_PAST_CHUNKS * CHUNK
REL_MAX = 128
SSM_GROUP = 16
N_SSM_GROUPS = 16
D_SSM = SSM_GROUP * N_SSM_GROUPS
SSM_STATE = 64
D_STATE = N_SSM_GROUPS * SSM_STATE
D_SCONV = 256
CONV_K = 3
D_FF = 2048
D_QKV = 3 * D_ATT
D_REST = D_SSM + 3 * D_SCONV
ALPHA = (2 * DEPTH) ** 0.25
LN_EPS = 1e-5

SUBLANES = 8
VMEM_LIMIT = 56 * 1024 * 1024

F32 = jnp.float32
BF16 = jnp.bfloat16


def _cparams(sem):
    return pltpu.CompilerParams(dimension_semantics=sem, vmem_limit_bytes=VMEM_LIMIT)


def _layer_norm(x, g, b):
    mu = jnp.mean(x, axis=-1, keepdims=True)
    xc = x - mu
    var = jnp.mean(xc * xc, axis=-1, keepdims=True)
    return xc * lax.rsqrt(var + LN_EPS) * g + b


def _dot(a, b):
    return jnp.dot(a, b, preferred_element_type=F32)


def _layer_spec(shape, layer):
    return pl.BlockSpec((None,) + shape, lambda *_: (layer,) + (0,) * len(shape))


def _causal_conv3(cur, hist_ref, w_ref, b_ref, nb, tt):
    outs = []
    for s in range(nb):
        v = cur[s * tt:(s + 1) * tt]
        ext = jnp.concatenate([hist_ref[s], v], axis=0)
        m1 = pltpu.roll(ext, 1, 0)[SUBLANES:]
        m2 = pltpu.roll(ext, 2, 0)[SUBLANES:]
        hist_ref[s] = v[tt - SUBLANES:tt]
        outs.append(w_ref[0:1, :] * m2 + w_ref[1:2, :] * m1 + w_ref[2:3, :] * v + b_ref[...])
    return outs[0] if nb == 1 else jnp.concatenate(outs, axis=0)


def _gelu_tanh(x):
    c = math.sqrt(2.0 / math.pi)
    return 0.5 * x * (1.0 + jnp.tanh(c * (x + 0.044715 * (x * x * x))))


def _in_ssm_kernel(x_ref, gin_ref, bin_ref, w_ref, scw_ref, scb_ref, sc0_ref, h0_ref,
                   lbar_ref, bblk_ref, cblk_ref, d_ref, wglu_ref, kt_all_ref, vt_all_ref,
                   qkv_ref, kt_ref, vt_ref, ssm_ref, sc_ref, scn_ref, hn_ref,
                   hist_ref, h_ref, st_ref, *, tt, ln_in):
    del kt_all_ref, vt_all_ref
    ti = pl.program_id(1)
    nb = SUBLANES
    rows = nb * tt

    @pl.when(ti == 0)
    def _():
        hist_ref[...] = sc0_ref[...]
        h_ref[...] = h0_ref[...]

    x = x_ref[...].reshape(rows, D_MODEL)
    if ln_in:
        x = _layer_norm(x, gin_ref[...], bin_ref[...])
    xb = x.astype(BF16)
    u = _dot(xb, w_ref[:, D_QKV:D_QKV + D_SSM])
    ut = jnp.swapaxes(u.reshape(nb, tt, D_SSM), 0, 1).reshape(rows, D_SSM)
    bu = _dot(ut.astype(BF16), bblk_ref[...])

    q = _dot(xb, w_ref[:, 0:D_ATT]) * (HEAD_DIM ** -0.5)
    qkv_ref[:, :, 0:D_ATT] = q.astype(BF16).reshape(nb, tt, D_ATT)
    for tail_ref, lo in ((kt_ref, D_ATT), (vt_ref, 2 * D_ATT)):
        kv = _dot(xb, w_ref[:, lo:lo + D_ATT])
        qkv_ref[:, :, lo:lo + D_ATT] = kv.astype(BF16).reshape(nb, tt, D_ATT)
        tail_ref[...] = kv.reshape(nb, tt, D_ATT)

    g = _dot(xb, w_ref[:, D_QKV + D_SSM:D_QKV + D_REST])
    gate_b = g[:, 0:D_SCONV]
    gate_c = g[:, D_SCONV:2 * D_SCONV]
    xv = g[:, 2 * D_SCONV:3 * D_SCONV]
    conv = _causal_conv3(gate_c * xv, hist_ref, scw_ref, scb_ref, nb, tt)
    sc_ref[...] = (gate_b * conv).astype(BF16).reshape(nb, tt, D_SCONV)
    scn_ref[...] = hist_ref[...]

    lr = jnp.broadcast_to(lbar_ref[0:1, :], (SUBLANES, D_STATE))
    li = jnp.broadcast_to(lbar_ref[1:2, :], (SUBLANES, D_STATE))
    hr = h_ref[:, 0:D_STATE]
    hi = h_ref[:, D_STATE:2 * D_STATE]
    for t in range(tt):
        r = slice(t * SUBLANES, (t + 1) * SUBLANES)
        hr, hi = (lr * hr - li * hi + bu[r, 0:D_STATE],
                  lr * hi + li * hr + bu[r, D_STATE:2 * D_STATE])
        st_ref[r, 0:D_STATE] = hr
        st_ref[r, D_STATE:2 * D_STATE] = hi
    y = _dot(st_ref[...].astype(BF16), cblk_ref[...]) + d_ref[...] * ut
    z = _gelu_tanh(y)
    gate = jax.nn.sigmoid(_dot(z.astype(BF16), wglu_ref[...]))
    h_ref[:, 0:D_STATE] = hr
    h_ref[:, D_STATE:2 * D_STATE] = hi
    hn_ref[:, 0:D_STATE] = hr
    hn_ref[:, D_STATE:2 * D_STATE] = hi
    out = (z * gate).reshape(tt, nb, D_SSM)
    ssm_ref[...] = jnp.swapaxes(out, 0, 1).astype(BF16)


def _in_ssm_call(x, ln_in, layer, w_in, scw, scb, sc0, h0, lbar, bblk, cblk, d, wglu, kt_all, vt_all, tt):
    apply_ln, g_in, b_in = ln_in
    bsz, s, _ = x.shape
    keep = kt_all.shape[2]
    nb = SUBLANES
    nt = s // tt
    tail_first = nt - keep // tt
    blk = lambda b, t: (b, t, 0)
    tail = lambda b, t: (layer, b, jnp.maximum(t - tail_first, 0), 0)
    n_in = 13
    bonly3 = lambda b, t: (b, 0, 0)
    bonly2 = lambda b, t: (b, 0)
    const2 = lambda b, t: (0, 0)
    return pl.pallas_call(
        functools.partial(_in_ssm_kernel, tt=tt, ln_in=apply_ln),
        grid=(bsz // nb, nt),
        in_specs=[pl.BlockSpec((nb, tt, D_MODEL), blk),
                  pl.BlockSpec((1, D_MODEL), const2),
                  pl.BlockSpec((1, D_MODEL), const2),
                  _layer_spec((D_MODEL, D_QKV + D_REST), layer),
                  pl.BlockSpec((CONV_K, D_SCONV), const2),
                  pl.BlockSpec((1, D_SCONV), const2),
                  pl.BlockSpec((nb, SUBLANES, D_SCONV), bonly3),
                  pl.BlockSpec((nb, 2 * D_STATE), bonly2),
                  pl.BlockSpec((2, D_STATE), const2),
                  pl.BlockSpec((D_SSM, 2 * D_STATE), const2),
                  pl.BlockSpec((2 * D_STATE, D_SSM), const2),
                  pl.BlockSpec((1, D_SSM), const2),
                  pl.BlockSpec((D_SSM, D_SSM), const2),
                  pl.BlockSpec(memory_space=pl.ANY),
                  pl.BlockSpec(memory_space=pl.ANY)],
        out_specs=[pl.BlockSpec((nb, tt, D_QKV), blk),
                   pl.BlockSpec((None, nb, tt, D_ATT), tail),
                   pl.BlockSpec((None, nb, tt, D_ATT), tail),
                   pl.BlockSpec((nb, tt, D_SSM), blk),
                   pl.BlockSpec((nb, tt, D_SCONV), blk),
                   pl.BlockSpec((nb, SUBLANES, D_SCONV), bonly3),
                   pl.BlockSpec((nb, 2 * D_STATE), bonly2)],
        out_shape=[jax.ShapeDtypeStruct((bsz, s, D_QKV), BF16),
                   jax.ShapeDtypeStruct(kt_all.shape, F32),
                   jax.ShapeDtypeStruct(vt_all.shape, F32),
                   jax.ShapeDtypeStruct((bsz, s, D_SSM), BF16),
                   jax.ShapeDtypeStruct((bsz, s, D_SCONV), BF16),
                   jax.ShapeDtypeStruct((bsz, SUBLANES, D_SCONV), F32),
                   jax.ShapeDtypeStruct((bsz, 2 * D_STATE), F32)],
        scratch_shapes=[pltpu.VMEM((nb, SUBLANES, D_SCONV), F32),
                        pltpu.VMEM((nb, 2 * D_STATE), F32),
                        pltpu.VMEM((nb * tt, 2 * D_STATE), F32)],
        input_output_aliases={n_in: 1, n_in + 1: 2},
        compiler_params=_cparams(("parallel", "arbitrary")),
        name="in_ssm",
    )(x, g_in, b_in, w_in, scw, scb, sc0, h0, lbar, bblk, cblk, d, wglu, kt_all, vt_all)


HEADS_PER_GROUP = 4
GROUP_W = HEADS_PER_GROUP * HEAD_DIM
N_GROUPS = N_HEADS // HEADS_PER_GROUP
Q_STEP = 4 * CHUNK
K_WIN = PAST + Q_STEP


def _softmax_unnormalised(sc):
    m = jnp.max(sc, axis=-1, keepdims=True)
    e = jnp.exp(sc - m)
    return e.astype(BF16), jnp.sum(e, axis=-1, keepdims=True)


def _group_head_of_lane(shape):
    lane = lax.broadcasted_iota(jnp.int32, shape, len(shape) - 1)
    return (lane // HEAD_DIM) % HEADS_PER_GROUP


def _softmax_banded(sc, bias):
    half, skip = Q_STEP // 2, 2 * CHUNK
    zeros = jnp.zeros((half, skip), BF16)
    e_top, l_top = _softmax_unnormalised(sc[0:half, 0:K_WIN - skip] + bias[0:half, 0:K_WIN - skip])
    e_bot, l_bot = _softmax_unnormalised(sc[half:Q_STEP, skip:K_WIN] + bias[half:Q_STEP, skip:K_WIN])
    e = jnp.concatenate([jnp.concatenate([e_top, zeros], axis=1),
                         jnp.concatenate([zeros, e_bot], axis=1)], axis=0)
    return e, jnp.concatenate([l_top, l_bot], axis=0)


def _attn_group(q4, k4, v_of_head, bias_of_head):
    head = _group_head_of_lane(q4.shape)
    full_window = q4.shape[0] == Q_STEP and k4.shape[0] == K_WIN
    acc = None
    scale = None
    for h in range(HEADS_PER_GROUP):
        qm = jnp.where(head == h, q4, jnp.zeros_like(q4))
        sc = lax.dot_general(qm, k4, (((1,), (1,)), ((), ())), preferred_element_type=F32)
        if full_window:
            e, l = _softmax_banded(sc, bias_of_head(h))
        else:
            e, l = _softmax_unnormalised(sc + bias_of_head(h))
        pv = _dot(e, v_of_head(h))
        acc = pv if acc is None else acc + pv
        inv = 1.0 / l
        scale = inv if scale is None else jnp.where(head == h, inv, scale)
    return acc * scale


def _attn_prompt_kernel(qkv_ref, bias_ref, o_ref, vmask_ref, *, s):
    v = qkv_ref[:, 2 * D_ATT:3 * D_ATT]
    head = _group_head_of_lane(v.shape)
    for h in range(HEADS_PER_GROUP):
        vmask_ref[h] = jnp.where(head == h, v, jnp.zeros_like(v))

    def step(r0, k0, nk):
        for g in range(N_GROUPS):
            cols = slice(g * GROUP_W, (g + 1) * GROUP_W)
            kcols = slice(D_ATT + g * GROUP_W, D_ATT + (g + 1) * GROUP_W)
            out = _attn_group(
                qkv_ref[pl.ds(r0, Q_STEP), cols],
                qkv_ref[pl.ds(k0, nk), kcols],
                lambda h: vmask_ref[h, pl.ds(k0, nk), cols],
                lambda h: bias_ref[g * HEADS_PER_GROUP + h, :, K_WIN - nk:K_WIN])
            o_ref[pl.ds(r0, Q_STEP), cols] = out.astype(BF16)

    n_steps = s // Q_STEP
    n_head_steps = min(PAST // Q_STEP, n_steps)
    for j in range(n_head_steps):
        step(j * Q_STEP, 0, (j + 1) * Q_STEP)

    def body(j, carry):
        r0 = pl.multiple_of(j * Q_STEP, Q_STEP)
        step(r0, pl.multiple_of(r0 - PAST, Q_STEP), K_WIN)
        return carry

    if n_steps > n_head_steps:
        lax.fori_loop(n_head_steps, n_steps, body, 0)


def _attn_prompt_call(qkv, bias):
    bsz, s, _ = qkv.shape
    assert s % Q_STEP == 0
    bsel = lambda b: (b, 0, 0)
    return pl.pallas_call(
        functools.partial(_attn_prompt_kernel, s=s),
        grid=(bsz,),
        in_specs=[pl.BlockSpec((None, s, D_QKV), bsel),
                  pl.BlockSpec((N_HEADS, Q_STEP, K_WIN), lambda b: (0, 0, 0))],
        out_specs=pl.BlockSpec((None, s, D_ATT), bsel),
        out_shape=jax.ShapeDtypeStruct((bsz, s, D_ATT), BF16),
        scratch_shapes=[pltpu.VMEM((HEADS_PER_GROUP, s, D_ATT), BF16)],
        compiler_params=_cparams(("parallel",)),
        name="attention_prompt",
    )(qkv, bias)


def _attn_sample_kernel(qkv_ref, pkt_ref, pvt_ref, bias_ref, o_ref, *, s):
    contract_last = (((1,), (1,)), ((), ()))
    for g in range(N_GROUPS):
        cols = slice(g * GROUP_W, (g + 1) * GROUP_W)
        q4 = qkv_ref[:, cols]
        k_new = qkv_ref[:, D_ATT + g * GROUP_W:D_ATT + (g + 1) * GROUP_W]
        v_new = qkv_ref[:, 2 * D_ATT + g * GROUP_W:2 * D_ATT + (g + 1) * GROUP_W]
        kt_past = pkt_ref[cols, :].astype(BF16)
        vt_past = pvt_ref[cols, :].astype(BF16)
        head = _group_head_of_lane(q4.shape)
        out = None
        scale = None
        for h in range(HEADS_PER_GROUP):
            qm = jnp.where(head == h, q4, jnp.zeros_like(q4))
            bias = bias_ref[g * HEADS_PER_GROUP + h]
            sc_past = _dot(qm, kt_past) + bias[:, 0:PAST]
            sc_new = lax.dot_general(qm, k_new, contract_last, preferred_element_type=F32) + bias[:, PAST:]
            m = jnp.maximum(jnp.max(sc_past, axis=-1, keepdims=True), jnp.max(sc_new, axis=-1, keepdims=True))
            e_past = jnp.exp(sc_past - m)
            e_new = jnp.exp(sc_new - m)
            l = jnp.sum(e_past, axis=-1, keepdims=True) + jnp.sum(e_new, axis=-1, keepdims=True)
            pv = (lax.dot_general(e_past.astype(BF16), vt_past, contract_last, preferred_element_type=F32)
                  + _dot(e_new.astype(BF16), v_new))
            inv = 1.0 / l
            out = pv if out is None else jnp.where(head == h, pv, out)
            scale = inv if scale is None else jnp.where(head == h, inv, scale)
        o_ref[:, cols] = (out * scale).astype(BF16)


def _attn_sample_call(qkv, layer, pkt, pvt, bias):
    bsz, s, _ = qkv.shape
    band = PAST + s
    bsel = lambda b: (b, 0, 0)
    return pl.pallas_call(
        functools.partial(_attn_sample_kernel, s=s),
        grid=(bsz,),
        in_specs=[pl.BlockSpec((None, s, D_QKV), bsel),
                  pl.BlockSpec((None, None, D_ATT, PAST), lambda b: (layer, b, 0, 0)),
                  pl.BlockSpec((None, None, D_ATT, PAST), lambda b: (layer, b, 0, 0)),
                  pl.BlockSpec((N_HEADS, s, band), lambda b: (0, 0, 0))],
        out_specs=pl.BlockSpec((None, s, D_ATT), bsel),
        out_shape=jax.ShapeDtypeStruct((bsz, s, D_ATT), BF16),
        compiler_params=_cparams(("parallel",)),
        name="attention_sample",
    )(qkv, pkt, pvt, bias)


def _mix_ffn_kernel(att_ref, ssm_ref, sc_ref, x_ref, gin_ref, bin_ref, wo_ref, g1_ref, b1_ref,
                    w1_ref, cw_ref, cb_ref, c0_ref, w2_ref, g2_ref, b2_ref,
                    o_ref, cn_ref, hist_ref, *, nb, tt, ln_in):
    ti = pl.program_id(1)
    rows = nb * tt

    @pl.when(ti == 0)
    def _():
        hist_ref[...] = c0_ref[...]

    x = x_ref[...].reshape(rows, D_MODEL)
    if ln_in:
        x = _layer_norm(x, gin_ref[...], bin_ref[...])
    mix = _dot(att_ref[...].reshape(rows, D_ATT), wo_ref[0:D_ATT, :])
    mix = mix + _dot(ssm_ref[...].reshape(rows, D_SSM), wo_ref[D_ATT:D_ATT + D_SSM, :])
    mix = mix + _dot(sc_ref[...].reshape(rows, D_SCONV), wo_ref[D_ATT + D_SSM:D_MODEL, :])
    x1 = _layer_norm(ALPHA * x + mix, g1_ref[...], b1_ref[...])

    up = _dot(x1.astype(BF16), w1_ref[...])
    conv = _causal_conv3(up, hist_ref, cw_ref, cb_ref, nb, tt)
    cn_ref[...] = hist_ref[...]
    gate = conv[:, 0:D_FF]
    val = conv[:, D_FF:2 * D_FF]
    h = (gate * jax.nn.sigmoid(gate) * val).astype(BF16)
    y = _dot(h, w2_ref[...])
    o_ref[...] = _layer_norm(ALPHA * x1 + y, g2_ref[...], b2_ref[...]).reshape(nb, tt, D_MODEL)


def _mix_ffn_call(att, ssm, sc, x, ln_in, layer, w_out, g1, b1, w1, cw, cb, c0, w2, g2, b2, nb, tt):
    apply_ln, g_in, b_in = ln_in
    bsz, s, _ = x.shape
    blk = lambda b, t: (b, t, 0)
    bonly = lambda b, t: (b, 0, 0)
    const2 = lambda b, t: (0, 0)
    vec = pl.BlockSpec((1, D_MODEL), const2)
    return pl.pallas_call(
        functools.partial(_mix_ffn_kernel, nb=nb, tt=tt, ln_in=apply_ln),
        grid=(bsz // nb, s // tt),
        in_specs=[pl.BlockSpec((nb, tt, D_ATT), blk),
                  pl.BlockSpec((nb, tt, D_SSM), blk),
                  pl.BlockSpec((nb, tt, D_SCONV), blk),
                  pl.BlockSpec((nb, tt, D_MODEL), blk), vec, vec,
                  _layer_spec((D_MODEL, D_MODEL), layer), vec, vec,
                  _layer_spec((D_MODEL, 2 * D_FF), layer),
                  pl.BlockSpec((CONV_K, 2 * D_FF), const2),
                  pl.BlockSpec((1, 2 * D_FF), const2),
                  pl.BlockSpec((nb, SUBLANES, 2 * D_FF), bonly),
                  _layer_spec((D_FF, D_MODEL), layer), vec, vec],
        out_specs=[pl.BlockSpec((nb, tt, D_MODEL), blk),
                   pl.BlockSpec((nb, SUBLANES, 2 * D_FF), bonly)],
        out_shape=[jax.ShapeDtypeStruct((bsz, s, D_MODEL), F32),
                   jax.ShapeDtypeStruct((bsz, SUBLANES, 2 * D_FF), F32)],
        scratch_shapes=[pltpu.VMEM((nb, SUBLANES, 2 * D_FF), F32)],
        compiler_params=_cparams(("parallel", "arbitrary")),
        name="mix_ffn",
    )(att, ssm, sc, x, g_in, b_in, w_out, g1, b1, w1, cw, cb, c0, w2, g2, b2)


def _rel_bias_table(rel_bias, tq, chunk):
    nk = PAST + tq
    period = nk + tq
    dist = (nk - 1) - jnp.arange(period)
    by_lag = rel_bias[:, jnp.clip(dist, -(CHUNK - 1), REL_MAX) + (CHUNK - 1)].astype(F32)
    rolled = jnp.roll(by_lag, -(tq - 1), axis=-1)
    n_heads = rel_bias.shape[0]
    bias = jnp.tile(rolled, (1, tq))[:, :tq * (period - 1)].reshape(n_heads, tq, period - 1)[:, :, :nk]
    if tq == chunk:
        return bias
    t_idx = np.arange(tq)[:, None]
    s_idx = np.arange(nk)[None, :]
    key_chunk = s_idx // chunk - t_idx // chunk
    in_band = (key_chunk >= 0) & (key_chunk <= N_PAST_CHUNKS)
    return jnp.where(in_band[None], bias, -jnp.inf)


def _ssm_params(lam_re, lam_im, log_dt, b_re, b_im, c_re, c_im):
    lam = lax.complex(lam_re.astype(F32), lam_im.astype(F32))
    dt = jnp.exp(log_dt.astype(F32))[:, None]
    lbar = jnp.exp(lam * dt)
    bbar = ((lbar - 1.0) / lam)[:, :, None] * lax.complex(b_re.astype(F32), b_im.astype(F32))
    eye = jnp.eye(N_SSM_GROUPS, dtype=F32)

    def in_map(m):
        return jnp.einsum('gpc,gh->gchp', m, eye).reshape(D_SSM, D_STATE)

    def out_map(m):
        return jnp.einsum('gcp,gh->gphc', m, eye).reshape(D_STATE, D_SSM)

    bblk = jnp.concatenate([in_map(bbar.real), in_map(bbar.imag)], axis=1).astype(BF16)
    cblk = jnp.concatenate([out_map(c_re.astype(F32)), out_map(-c_im.astype(F32))], axis=0).astype(BF16)
    lbar2 = jnp.stack([lbar.real.reshape(D_STATE), lbar.imag.reshape(D_STATE)], axis=0)
    return lbar2, bblk, cblk


def _pad_rows(buf):
    return jnp.pad(buf.astype(F32), ((0, 0), (SUBLANES - (CONV_K - 1), 0), (0, 0)))


def _trunk_layer(x, ln_in, layer, lp, past, kv_tails, tiles):
    bsz, s, _ = x.shape
    tt_in, nb_ffn, tt_ffn = tiles
    if past is None:
        h0 = jnp.zeros((bsz, 2 * D_STATE), F32)
        sc0 = jnp.zeros((bsz, SUBLANES, D_SCONV), F32)
        ff0 = jnp.zeros((bsz, SUBLANES, 2 * D_FF), F32)
    else:
        pk, pv, h_re0, h_im0, sconv_buf, ffn_buf = past
        h0 =jnp.concatenate([h_re0.reshape(bsz, D_STATE), h_im0.reshape(bsz, D_STATE)], axis=1).astype(F32)
        sc0 = _pad_rows(sconv_buf)
        ff0 = _pad_rows(ffn_buf)

    qkv, kt_all, vt_all, ssm_out, sconv_out, sc_new, h_new = _in_ssm_call(
        x, ln_in, layer, lp['w_in'], lp['sconv_w'], lp['sconv_b'], sc0, h0,
        lp['lbar'], lp['bblk'], lp['cblk'], lp['d'], lp['w_glu'], *kv_tails, tt_in)
    if past is None:
        att = _attn_prompt_call(qkv, _rel_bias_table(lp['rel_bias'], Q_STEP, CHUNK))
    else:
        att = _attn_sample_call(qkv, layer, pk, pv, _rel_bias_table(lp['rel_bias'], s, s))
    x2, ff_new = _mix_ffn_call(att, ssm_out, sconv_out, x, ln_in, layer, lp['w_out'], lp['ln1_g'], lp['ln1_b'],
                               lp['w_ff_in'], lp['ffn_conv_w'], lp['ffn_conv_b'], ff0,
                               lp['w_ff_out'], lp['ln2_g'], lp['ln2_b'], nb_ffn, tt_ffn)
    new = (h_new[:, 0:D_STATE].reshape(bsz, N_SSM_GROUPS, SSM_STATE),
           h_new[:, D_STATE:].reshape(bsz, N_SSM_GROUPS, SSM_STATE),
           sc_new[:, SUBLANES - (CONV_K - 1):], ff_new[:, SUBLANES - (CONV_K - 1):])
    return x2, (kt_all, vt_all), new


ROW_TILE = 512
IN_SSM_POSITIONS = 128


def _tiles(bsz, s):
    tt_in = min(IN_SSM_POSITIONS, s)
    tt_ffn = min(ROW_TILE, s)
    nb_ffn = min(bsz, ROW_TILE // tt_ffn)
    return tt_in, nb_ffn, tt_ffn


def kernel(x_prompt, x_sample, cache_k, cache_v, state_ssm_re, state_ssm_im, cache_sconv, cache_ffn_conv, ln_in_g, ln_in_b, w_in, rel_bias, ssm_lam_re, ssm_lam_im, ssm_log_dt, ssm_b_re, ssm_b_im, ssm_c_re, ssm_c_im, ssm_d, w_glu, sconv_w, sconv_b, w_out, ln1_g, ln1_b, w_ff_in, ffn_conv_w, ffn_conv_b, w_ff_out, ln2_g, ln2_b):
    bp, sp, _ = x_prompt.shape
    bs, ss, _ = x_sample.shape
    g_in = ln_in_g.reshape(1, D_MODEL)
    b_in = ln_in_b.reshape(1, D_MODEL)
    xp, xs = x_prompt, x_sample
    to_feature_major = lambda c: jnp.transpose(c, (0, 1, 3, 4, 2)).reshape(DEPTH, bs, D_ATT, PAST)
    pk_all = to_feature_major(cache_k)
    pv_all = to_feature_major(cache_v)
    w_in_b, w_out_b = w_in.astype(BF16), w_out.astype(BF16)
    w_ff_in_b, w_ff_out_b = w_ff_in.astype(BF16), w_ff_out.astype(BF16)
    kv_p = tuple(jnp.zeros((DEPTH, bp, min(PAST, sp), D_ATT), F32) for _ in range(2))
    kv_s = tuple(jnp.zeros((DEPTH, bs, ss, D_ATT), F32) for _ in range(2))
    st_p = [[] for _ in range(4)]
    st_s = [[] for _ in range(4)]
    for l in range(DEPTH):
        lbar, bblk, cblk = _ssm_params(ssm_lam_re[l], ssm_lam_im[l], ssm_log_dt[l], ssm_b_re[l], ssm_b_im[l],
                                       ssm_c_re[l], ssm_c_im[l])
        lp = {'w_in': w_in_b, 'rel_bias': rel_bias[l],
              'lbar': lbar, 'bblk': bblk, 'cblk': cblk,
              'd': ssm_d[l].reshape(1, D_SSM), 'w_glu': w_glu[l].astype(BF16),
              'sconv_w': sconv_w[l], 'sconv_b': sconv_b[l].reshape(1, D_SCONV),
              'w_out': w_out_b,
              'ln1_g': ln1_g[l].reshape(1, D_MODEL), 'ln1_b': ln1_b[l].reshape(1, D_MODEL),
              'w_ff_in': w_ff_in_b, 'ffn_conv_w': ffn_conv_w[l],
              'ffn_conv_b': ffn_conv_b[l].reshape(1, 2 * D_FF),
              'w_ff_out': w_ff_out_b,
              'ln2_g': ln2_g[l].reshape(1, D_MODEL), 'ln2_b': ln2_b[l].reshape(1, D_MODEL)}
        ln_in = (l == 0, g_in, b_in)
        xp, kv_p, new_p = _trunk_layer(xp, ln_in, l, lp, None, kv_p, _tiles(bp, sp))
        past = (pk_all, pv_all, state_ssm_re[l], state_ssm_im[l], cache_sconv[l], cache_ffn_conv[l])
        xs, kv_s, new_s = _trunk_layer(xs, ln_in, l, lp, past, kv_s, _tiles(bs, ss))
        for i in range(4):
            st_p[i].append(new_p[i])
            st_s[i].append(new_s[i])
    hre_p, him_p, sc_p, ff_p = [jnp.stack(a, axis=0) for a in st_p]
    hre_s, him_s, sc_s, ff_s = [jnp.stack(a, axis=0) for a in st_s]
    heads = lambda a: a.reshape(a.shape[:-1] + (N_HEADS, HEAD_DIM))
    return (xp, xs, heads(kv_p[0]), heads(kv_p[1]), heads(kv_s[0]), heads(kv_s[1]),
            hre_p, him_p, hre_s, him_s, sc_p, sc_s, ff_p, ff_s)
```

```python
import functools
import math

import jax
import jax.numpy as jnp
import numpy as np
from jax import lax
from jax.experimental import pallas as pl
from jax.experimental.pallas import tpu as pltpu

D_MODEL = 1024
DEPTH = 4
CHUNK = 64
N_HEADS = 8
HEAD_DIM = 64
D_ATT = N_HEADS * HEAD_DIM
N_PAST_CHUNKS = 8
PAST = N_PAST_CHUNKS * CHUNK
REL_MAX = 128
SSM_GROUP = 16
N_SSM_GROUPS = 16
D_SSM = SSM_GROUP * N_SSM_GROUPS
SSM_STATE = 64
D_STATE = N_SSM_GROUPS * SSM_STATE
D_SCONV = 256
CONV_K = 3
D_FF = 2048
D_QKV = 3 * D_ATT
D_REST = D_SSM + 3 * D_SCONV
ALPHA = (2 * DEPTH) ** 0.25
LN_EPS = 1e-5

SUBLANES = 8
VMEM_LIMIT = 56 * 1024 * 1024

F32 = jnp.float32
BF16 = jnp.bfloat16


def _cparams(sem):
    return pltpu.CompilerParams(dimension_semantics=sem, vmem_limit_bytes=VMEM_LIMIT)


def _layer_norm(x, g, b):
    mu = jnp.mean(x, axis=-1, keepdims=True)
    xc = x - mu
    var = jnp.mean(xc * xc, axis=-1, keepdims=True)
    return xc * lax.rsqrt(var + LN_EPS) * g + b


def _dot(a, b):
    return jnp.dot(a, b, preferred_element_type=F32)


def _layer_spec(shape, layer):
    return pl.BlockSpec((None,) + shape, lambda *_: (layer,) + (0,) * len(shape))


def _causal_conv3(cur, hist_ref, w_ref, b_ref, nb, tt):
    outs = []
    for s in range(nb):
        v = cur[s * tt:(s + 1) * tt]
        ext = jnp.concatenate([hist_ref[s], v], axis=0)
        m1 = pltpu.roll(ext, 1, 0)[SUBLANES:]
        m2 = pltpu.roll(ext, 2, 0)[SUBLANES:]
        hist_ref[s] = v[tt - SUBLANES:tt]
        outs.append(w_ref[0:1, :] * m2 + w_ref[1:2, :] * m1 + w_ref[2:3, :] * v + b_ref[...])
    return outs[0] if nb == 1 else jnp.concatenate(outs, axis=0)


def _gelu_tanh(x):
    c = math.sqrt(2.0 / math.pi)
    return 0.5 * x * (1.0 + jnp.tanh(c * (x + 0.044715 * (x * x * x))))


def _in_ssm_kernel(x_ref, gin_ref, bin_ref, w_ref, scw_ref, scb_ref, sc0_ref, h0_ref,
                   lbar_ref, bblk_ref, cblk_ref, d_ref, wglu_ref, kt_all_ref, vt_all_ref,
                   qkv_ref, kt_ref, vt_ref, ssm_ref, sc_ref, scn_ref, hn_ref,
                   hist_ref, h_ref, st_ref, *, tt, ln_in):
    del kt_all_ref, vt_all_ref
    ti = pl.program_id(1)
    nb = SUBLANES
    rows = nb * tt

    @pl.when(ti == 0)
    def _():
        hist_ref[...] = sc0_ref[...]
        h_ref[...] = h0_ref[...]

    x = x_ref[...].reshape(rows, D_MODEL)
    if ln_in:
        x = _layer_norm(x, gin_ref[...], bin_ref[...])
    xb = x.astype(BF16)
    u = _dot(xb, w_ref[:, D_QKV:D_QKV + D_SSM])
    ut = jnp.swapaxes(u.reshape(nb, tt, D_SSM), 0, 1).reshape(rows, D_SSM)
    bu = _dot(ut.astype(BF16), bblk_ref[...])

    q = _dot(xb, w_ref[:, 0:D_ATT]) * (HEAD_DIM ** -0.5)
    qkv_ref[:, :, 0:D_ATT] = q.astype(BF16).reshape(nb, tt, D_ATT)
    for tail_ref, lo in ((kt_ref, D_ATT), (vt_ref, 2 * D_ATT)):
        kv = _dot(xb, w_ref[:, lo:lo + D_ATT])
        qkv_ref[:, :, lo:lo + D_ATT] = kv.astype(BF16).reshape(nb, tt, D_ATT)
        tail_ref[...] = kv.reshape(nb, tt, D_ATT)

    g = _dot(xb, w_ref[:, D_QKV + D_SSM:D_QKV + D_REST])
    gate_b = g[:, 0:D_SCONV]
    gate_c = g[:, D_SCONV:2 * D_SCONV]
    xv = g[:, 2 * D_SCONV:3 * D_SCONV]
    conv = _causal_conv3(gate_c * xv, hist_ref, scw_ref, scb_ref, nb, tt)
    sc_ref[...] = (gate_b * conv).astype(BF16).reshape(nb, tt, D_SCONV)
    scn_ref[...] = hist_ref[...]

    lr = jnp.broadcast_to(lbar_ref[0:1, :], (SUBLANES, D_STATE))
    li = jnp.broadcast_to(lbar_ref[1:2, :], (SUBLANES, D_STATE))
    hr = h_ref[:, 0:D_STATE]
    hi = h_ref[:, D_STATE:2 * D_STATE]
    for t in range(tt):
        r = slice(t * SUBLANES, (t + 1) * SUBLANES)
        hr, hi = (lr * hr - li * hi + bu[r, 0:D_STATE],
                  lr * hi + li * hr + bu[r, D_STATE:2 * D_STATE])
        st_ref[r, 0:D_STATE] = hr
        st_ref[r, D_STATE:2 * D_STATE] = hi
    y = _dot(st_ref[...].astype(BF16), cblk_ref[...]) + d_ref[...] * ut
    z = _gelu_tanh(y)
    gate = jax.nn.sigmoid(_dot(z.astype(BF16), wglu_ref[...]))
    h_ref[:, 0:D_STATE] = hr
    h_ref[:, D_STATE:2 * D_STATE] = hi
    hn_ref[:, 0:D_STATE] = hr
    hn_ref[:, D_STATE:2 * D_STATE] = hi
    out = (z * gate).reshape(tt, nb, D_SSM)
    ssm_ref[...] = jnp.swapaxes(out, 0, 1).astype(BF16)


def _in_ssm_call(x, ln_in, layer, init_layer, w_in, scw, scb, sc0, h0, lbar, bblk, cblk, d, wglu,
                 kt_all, vt_all, tt):
    apply_ln, g_in, b_in = ln_in
    bsz, s, _ = x.shape
    keep = kt_all.shape[2]
    nb = SUBLANES
    nt = s // tt
    tail_first = nt - keep // tt
    blk = lambda b, t: (b, t, 0)
    tail = lambda b, t: (layer, b, jnp.maximum(t - tail_first, 0), 0)
    n_in = 13
    bonly3 = lambda b, t: (b, 0, 0)
    bonly2 = lambda b, t: (b, 0)
    const2 = lambda b, t: (0, 0)
    init3 = lambda b, t: (init_layer, b, 0, 0)
    init2 = lambda b, t: (init_layer, b, 0)
    return pl.pallas_call(
        functools.partial(_in_ssm_kernel, tt=tt, ln_in=apply_ln),
        grid=(bsz // nb, nt),
        in_specs=[pl.BlockSpec((nb, tt, D_MODEL), blk),
                  pl.BlockSpec((1, D_MODEL), const2),
                  pl.BlockSpec((1, D_MODEL), const2),
                  _layer_spec((D_MODEL, D_QKV + D_REST), layer),
                  _layer_spec((CONV_K, D_SCONV), layer),
                  _layer_spec((1, D_SCONV), layer),
                  pl.BlockSpec((None, nb, SUBLANES, D_SCONV), init3),
                  pl.BlockSpec((None, nb, 2 * D_STATE), init2),
                  _layer_spec((2, D_STATE), layer),
                  _layer_spec((D_SSM, 2 * D_STATE), layer),
                  _layer_spec((2 * D_STATE, D_SSM), layer),
                  _layer_spec((1, D_SSM), layer),
                  _layer_spec((D_SSM, D_SSM), layer),
                  pl.BlockSpec(memory_space=pl.ANY),
                  pl.BlockSpec(memory_space=pl.ANY)],
        out_specs=[pl.BlockSpec((nb, tt, D_QKV), blk),
                   pl.BlockSpec((None, nb, tt, D_ATT), tail),
                   pl.BlockSpec((None, nb, tt, D_ATT), tail),
                   pl.BlockSpec((nb, tt, D_SSM), blk),
                   pl.BlockSpec((nb, tt, D_SCONV), blk),
                   pl.BlockSpec((nb, SUBLANES, D_SCONV), bonly3),
                   pl.BlockSpec((nb, 2 * D_STATE), bonly2)],
        out_shape=[jax.ShapeDtypeStruct((bsz, s, D_QKV), BF16),
                   jax.ShapeDtypeStruct(kt_all.shape, F32),
                   jax.ShapeDtypeStruct(vt_all.shape, F32),
                   jax.ShapeDtypeStruct((bsz, s, D_SSM), BF16),
                   jax.ShapeDtypeStruct((bsz, s, D_SCONV), BF16),
                   jax.ShapeDtypeStruct((bsz, SUBLANES, D_SCONV), F32),
                   jax.ShapeDtypeStruct((bsz, 2 * D_STATE), F32)],
        scratch_shapes=[pltpu.VMEM((nb, SUBLANES, D_SCONV), F32),
                        pltpu.VMEM((nb, 2 * D_STATE), F32),
                        pltpu.VMEM((nb * tt, 2 * D_STATE), F32)],
        input_output_aliases={n_in: 1, n_in + 1: 2},
        compiler_params=_cparams(("parallel", "arbitrary")),
        name="in_ssm",
    )(x, g_in, b_in, w_in, scw, scb, sc0, h0, lbar, bblk, cblk, d, wglu, kt_all, vt_all)


HEADS_PER_GROUP = 4
GROUP_W = HEADS_PER_GROUP * HEAD_DIM
N_GROUPS = N_HEADS // HEADS_PER_GROUP
Q_STEP = 4 * CHUNK
K_WIN = PAST + Q_STEP


def _softmax_unnormalised(sc):
    m = jnp.max(sc, axis=-1, keepdims=True)
    e = jnp.exp(sc - m)
    return e.astype(BF16), jnp.sum(e, axis=-1, keepdims=True)


def _group_head_of_lane(shape):
    lane = lax.broadcasted_iota(jnp.int32, shape, len(shape) - 1)
    return (lane // HEAD_DIM) % HEADS_PER_GROUP


def _softmax_banded(sc, bias):
    half, skip = Q_STEP // 2, 2 * CHUNK
    zeros = jnp.zeros((half, skip), BF16)
    e_top, l_top = _softmax_unnormalised(sc[0:half, 0:K_WIN - skip] + bias[0:half, 0:K_WIN - skip])
    e_bot, l_bot = _softmax_unnormalised(sc[half:Q_STEP, skip:K_WIN] + bias[half:Q_STEP, skip:K_WIN])
    e = jnp.concatenate([jnp.concatenate([e_top, zeros], axis=1),
                         jnp.concatenate([zeros, e_bot], axis=1)], axis=0)
    return e, jnp.concatenate([l_top, l_bot], axis=0)


def _attn_group(q4, k4, v_of_head, bias_of_head):
    head = _group_head_of_lane(q4.shape)
    full_window = q4.shape[0] == Q_STEP and k4.shape[0] == K_WIN
    acc = None
    scale = None
    for h in range(HEADS_PER_GROUP):
        qm = jnp.where(head == h, q4, jnp.zeros_like(q4))
        sc = lax.dot_general(qm, k4, (((1,), (1,)), ((), ())), preferred_element_type=F32)
        if full_window:
            e, l = _softmax_banded(sc, bias_of_head(h))
        else:
            e, l = _softmax_unnormalised(sc + bias_of_head(h))
        pv = _dot(e, v_of_head(h))
        acc = pv if acc is None else acc + pv
        inv = 1.0 / l
        scale = inv if scale is None else jnp.where(head == h, inv, scale)
    return acc * scale


def _attn_prompt_kernel(qkv_ref, bias_ref, o_ref, vmask_ref, *, s):
    v = qkv_ref[:, 2 * D_ATT:3 * D_ATT]
    head = _group_head_of_lane(v.shape)
    for h in range(HEADS_PER_GROUP):
        vmask_ref[h] = jnp.where(head == h, v, jnp.zeros_like(v))

    def step(r0, k0, nk):
        for g in range(N_GROUPS):
            cols = slice(g * GROUP_W, (g + 1) * GROUP_W)
            kcols = slice(D_ATT + g * GROUP_W, D_ATT + (g + 1) * GROUP_W)
            out = _attn_group(
                qkv_ref[pl.ds(r0, Q_STEP), cols],
                qkv_ref[pl.ds(k0, nk), kcols],
                lambda h: vmask_ref[h, pl.ds(k0, nk), cols],
                lambda h: bias_ref[g * HEADS_PER_GROUP + h, :, K_WIN - nk:K_WIN])
            o_ref[pl.ds(r0, Q_STEP), cols] = out.astype(BF16)

    n_steps = s // Q_STEP
    n_head_steps = min(PAST // Q_STEP, n_steps)
    for j in range(n_head_steps):
        step(j * Q_STEP, 0, (j + 1) * Q_STEP)

    def body(j, carry):
        r0 = pl.multiple_of(j * Q_STEP, Q_STEP)
        step(r0, pl.multiple_of(r0 - PAST, Q_STEP), K_WIN)
        return carry

    if n_steps > n_head_steps:
        lax.fori_loop(n_head_steps, n_steps, body, 0)


def _attn_prompt_call(qkv, layer, bias):
    bsz, s, _ = qkv.shape
    assert s % Q_STEP == 0
    bsel = lambda b: (b, 0, 0)
    return pl.pallas_call(
        functools.partial(_attn_prompt_kernel, s=s),
        grid=(bsz,),
        in_specs=[pl.BlockSpec((None, s, D_QKV), bsel),
                  _layer_spec((N_HEADS, Q_STEP, K_WIN), layer)],
        out_specs=pl.BlockSpec((None, s, D_ATT), bsel),
        out_shape=jax.ShapeDtypeStruct((bsz, s, D_ATT), BF16),
        scratch_shapes=[pltpu.VMEM((HEADS_PER_GROUP, s, D_ATT), BF16)],
        compiler_params=_cparams(("parallel",)),
        name="attention_prompt",
    )(qkv, bias)


def _attn_sample_kernel(qkv_ref, pkt_ref, pvt_ref, bias_ref, o_ref, *, s):
    contract_last = (((1,), (1,)), ((), ()))
    for g in range(N_GROUPS):
        cols = slice(g * GROUP_W, (g + 1) * GROUP_W)
        q4 = qkv_ref[:, cols]
        k_new = qkv_ref[:, D_ATT + g * GROUP_W:D_ATT + (g + 1) * GROUP_W]
        v_new = qkv_ref[:, 2 * D_ATT + g * GROUP_W:2 * D_ATT + (g + 1) * GROUP_W]
        kt_past = pkt_ref[cols, :].astype(BF16)
        vt_past = pvt_ref[cols, :].astype(BF16)
        head = _group_head_of_lane(q4.shape)
        out = None
        scale = None
        for h in range(HEADS_PER_GROUP):
            qm = jnp.where(head == h, q4, jnp.zeros_like(q4))
            bias = bias_ref[g * HEADS_PER_GROUP + h]
            sc_past = _dot(qm, kt_past) + bias[:, 0:PAST]
            sc_new = lax.dot_general(qm, k_new, contract_last, preferred_element_type=F32) + bias[:, PAST:]
            m = jnp.maximum(jnp.max(sc_past, axis=-1, keepdims=True), jnp.max(sc_new, axis=-1, keepdims=True))
            e_past = jnp.exp(sc_past - m)
            e_new = jnp.exp(sc_new - m)
            l = jnp.sum(e_past, axis=-1, keepdims=True) + jnp.sum(e_new, axis=-1, keepdims=True)
            pv = (lax.dot_general(e_past.astype(BF16), vt_past, contract_last, preferred_element_type=F32)
                  + _dot(e_new.astype(BF16), v_new))
            inv = 1.0 / l
            out = pv if out is None else jnp.where(head == h, pv, out)
            scale = inv if scale is None else jnp.where(head == h, inv, scale)
        o_ref[:, cols] = (out * scale).astype(BF16)


def _attn_sample_call(qkv, layer, pkt, pvt, bias):
    bsz, s, _ = qkv.shape
    band = PAST + s
    bsel = lambda b: (b, 0, 0)
    return pl.pallas_call(
        functools.partial(_attn_sample_kernel, s=s),
        grid=(bsz,),
        in_specs=[pl.BlockSpec((None, s, D_QKV), bsel),
                  pl.BlockSpec((None, None, D_ATT, PAST), lambda b: (layer, b, 0, 0)),
                  pl.BlockSpec((None, None, D_ATT, PAST), lambda b: (layer, b, 0, 0)),
                  _layer_spec((N_HEADS, s, band), layer)],
        out_specs=pl.BlockSpec((None, s, D_ATT), bsel),
        out_shape=jax.ShapeDtypeStruct((bsz, s, D_ATT), BF16),
        compiler_params=_cparams(("parallel",)),
        name="attention_sample",
    )(qkv, pkt, pvt, bias)


def _mix_ffn_kernel(att_ref, ssm_ref, sc_ref, x_ref, gin_ref, bin_ref, wo_ref, g1_ref, b1_ref,
                    w1_ref, cw_ref, cb_ref, c0_ref, w2_ref, g2_ref, b2_ref,
                    o_ref, cn_ref, hist_ref, *, nb, tt, ln_in):
    ti = pl.program_id(1)
    rows = nb * tt

    @pl.when(ti == 0)
    def _():
        hist_ref[...] = c0_ref[...]

    x = x_ref[...].reshape(rows, D_MODEL)
    if ln_in:
        x = _layer_norm(x, gin_ref[...], bin_ref[...])
    mix = _dot(att_ref[...].reshape(rows, D_ATT), wo_ref[0:D_ATT, :])
    mix = mix + _dot(ssm_ref[...].reshape(rows, D_SSM), wo_ref[D_ATT:D_ATT + D_SSM, :])
    mix = mix + _dot(sc_ref[...].reshape(rows, D_SCONV), wo_ref[D_ATT + D_SSM:D_MODEL, :])
    x1 = _layer_norm(ALPHA * x + mix, g1_ref[...], b1_ref[...])

    up = _dot(x1.astype(BF16), w1_ref[...])
    conv = _causal_conv3(up, hist_ref, cw_ref, cb_ref, nb, tt)
    cn_ref[...] = hist_ref[...]
    gate = conv[:, 0:D_FF]
    val = conv[:, D_FF:2 * D_FF]
    h = (gate * jax.nn.sigmoid(gate) * val).astype(BF16)
    y = _dot(h, w2_ref[...])
    o_ref[...] = _layer_norm(ALPHA * x1 + y, g2_ref[...], b2_ref[...]).reshape(nb, tt, D_MODEL)


def _mix_ffn_call(att, ssm, sc, x, ln_in, layer, init_layer, w_out, g1, b1, w1, cw, cb, c0, w2, g2, b2,
                  nb, tt):
    apply_ln, g_in, b_in = ln_in
    bsz, s, _ = x.shape
    blk = lambda b, t: (b, t, 0)
    bonly = lambda b, t: (b, 0, 0)
    const2 = lambda b, t: (0, 0)
    vec = pl.BlockSpec((1, D_MODEL), const2)
    lvec = _layer_spec((1, D_MODEL), layer)
    return pl.pallas_call(
        functools.partial(_mix_ffn_kernel, nb=nb, tt=tt, ln_in=apply_ln),
        grid=(bsz // nb, s // tt),
        in_specs=[pl.BlockSpec((nb, tt, D_ATT), blk),
                  pl.BlockSpec((nb, tt, D_SSM), blk),
                  pl.BlockSpec((nb, tt, D_SCONV), blk),
                  pl.BlockSpec((nb, tt, D_MODEL), blk), vec, vec,
                  _layer_spec((D_MODEL, D_MODEL), layer), lvec, lvec,
                  _layer_spec((D_MODEL, 2 * D_FF), layer),
                  _layer_spec((CONV_K, 2 * D_FF), layer),
                  _layer_spec((1, 2 * D_FF), layer),
                  pl.BlockSpec((None, nb, SUBLANES, 2 * D_FF), lambda b, t: (init_layer, b, 0, 0)),
                  _layer_spec((D_FF, D_MODEL), layer), lvec, lvec],
        out_specs=[pl.BlockSpec((nb, tt, D_MODEL), blk),
                   pl.BlockSpec((nb, SUBLANES, 2 * D_FF), bonly)],
        out_shape=[jax.ShapeDtypeStruct((bsz, s, D_MODEL), F32),
                   jax.ShapeDtypeStruct((bsz, SUBLANES, 2 * D_FF), F32)],
        scratch_shapes=[pltpu.VMEM((nb, SUBLANES, 2 * D_FF), F32)],
        compiler_params=_cparams(("parallel", "arbitrary")),
        name="mix_ffn",
    )(att, ssm, sc, x, g_in, b_in, w_out, g1, b1, w1, cw, cb, c0, w2, g2, b2)


def _rel_bias_table(rel_bias, tq, chunk):
    depth, n_heads, _ = rel_bias.shape
    nk = PAST + tq
    period = nk + tq
    dist = (nk - 1) - jnp.arange(period)
    by_lag = rel_bias[:, :, jnp.clip(dist, -(CHUNK - 1), REL_MAX) + (CHUNK - 1)].astype(F32)
    rolled = jnp.roll(by_lag, -(tq - 1), axis=-1)
    bias = jnp.tile(rolled, (1, 1, tq))[:, :, :tq * (period - 1)]
    bias = bias.reshape(depth, n_heads, tq, period - 1)[:, :, :, :nk]
    if tq == chunk:
        return bias
    t_idx = np.arange(tq)[:, None]
    s_idx = np.arange(nk)[None, :]
    key_chunk = s_idx // chunk - t_idx // chunk
    in_band = (key_chunk >= 0) & (key_chunk <= N_PAST_CHUNKS)
    return jnp.where(in_band[None, None], bias, -jnp.inf)


def _ssm_params(lam_re, lam_im, log_dt, b_re, b_im, c_re, c_im):
    depth = lam_re.shape[0]
    lam = lax.complex(lam_re.astype(F32), lam_im.astype(F32))
    dt = jnp.exp(log_dt.astype(F32))[:, :, None]
    lbar = jnp.exp(lam * dt)
    bbar = ((lbar - 1.0) / lam)[..., None] * lax.complex(b_re.astype(F32), b_im.astype(F32))
    eye = jnp.eye(N_SSM_GROUPS, dtype=F32)

    def in_map(m):
        return jnp.einsum('lgpc,gh->lgchp', m, eye).reshape(depth, D_SSM, D_STATE)

    def out_map(m):
        return jnp.einsum('lgcp,gh->lgphc', m, eye).reshape(depth, D_STATE, D_SSM)

    bblk = jnp.concatenate([in_map(bbar.real), in_map(bbar.imag)], axis=2).astype(BF16)
    cblk = jnp.concatenate([out_map(c_re.astype(F32)), out_map(-c_im.astype(F32))], axis=1).astype(BF16)
    lbar2 = jnp.stack([lbar.real.reshape(depth, D_STATE), lbar.imag.reshape(depth, D_STATE)], axis=1)
    return lbar2, bblk, cblk


def _pad_rows(buf):
    return jnp.pad(buf.astype(F32), ((0, 0), (0, 0), (SUBLANES - (CONV_K - 1), 0), (0, 0)))


def _trunk_layer(x, ln_in, layer, lp, group, kv_tails, tiles):
    bsz, s, _ = x.shape
    tt_in, nb_ffn, tt_ffn = tiles
    init_layer = group['init_layer'](layer)

    qkv, kt_all, vt_all, ssm_out, sconv_out, sc_new, h_new = _in_ssm_call(
        x, ln_in, layer, init_layer, lp['w_in'], lp['sconv_w'], lp['sconv_b'], group['sc0'], group['h0'],
        lp['lbar'], lp['bblk'], lp['cblk'], lp['d'], lp['w_glu'], *kv_tails, tt_in)
    if 'pk' in group:
        att = _attn_sample_call(qkv, layer, group['pk'], group['pv'], group['bias'])
    else:
        att = _attn_prompt_call(qkv, layer, group['bias'])
    x2, ff_new = _mix_ffn_call(att, ssm_out, sconv_out, x, ln_in, layer, init_layer,
                               lp['w_out'], lp['ln1_g'], lp['ln1_b'],
                               lp['w_ff_in'], lp['ffn_conv_w'], lp['ffn_conv_b'], group['ff0'],
                               lp['w_ff_out'], lp['ln2_g'], lp['ln2_b'], nb_ffn, tt_ffn)
    new = (h_new[:, 0:D_STATE].reshape(bsz, N_SSM_GROUPS, SSM_STATE),
           h_new[:, D_STATE:].reshape(bsz, N_SSM_GROUPS, SSM_STATE),
           sc_new[:, SUBLANES - (CONV_K - 1):], ff_new[:, SUBLANES - (CONV_K - 1):])
    return x2, (kt_all, vt_all), new


ROW_TILE = 512
IN_SSM_POSITIONS = 128


def _tiles(bsz, s):
    tt_in = min(IN_SSM_POSITIONS, s)
    tt_ffn = min(ROW_TILE, s)
    nb_ffn = min(bsz, ROW_TILE // tt_ffn)
    return tt_in, nb_ffn, tt_ffn


def kernel(x_prompt, x_sample, cache_k, cache_v, state_ssm_re, state_ssm_im, cache_sconv, cache_ffn_conv, ln_in_g, ln_in_b, w_in, rel_bias, ssm_lam_re, ssm_lam_im, ssm_log_dt, ssm_b_re, ssm_b_im, ssm_c_re, ssm_c_im, ssm_d, w_glu, sconv_w, sconv_b, w_out, ln1_g, ln1_b, w_ff_in, ffn_conv_w, ffn_conv_b, w_ff_out, ln2_g, ln2_b):
    bp, sp, _ = x_prompt.shape
    bs, ss, _ = x_sample.shape
    g_in = ln_in_g.reshape(1, D_MODEL)
    b_in = ln_in_b.reshape(1, D_MODEL)
    xp, xs = x_prompt, x_sample
    to_feature_major = lambda c: jnp.transpose(c, (0, 1, 3, 4, 2)).reshape(DEPTH, bs, D_ATT, PAST)
    pk_all = to_feature_major(cache_k)
    pv_all = to_feature_major(cache_v)
    lbar, bblk, cblk = _ssm_params(ssm_lam_re, ssm_lam_im, ssm_log_dt, ssm_b_re, ssm_b_im, ssm_c_re, ssm_c_im)
    row = lambda a: a.reshape(DEPTH, 1, a.shape[-1])
    lp = {'w_in': w_in.astype(BF16), 'lbar': lbar, 'bblk': bblk, 'cblk': cblk,
          'd': row(ssm_d), 'w_glu': w_glu.astype(BF16),
          'sconv_w': sconv_w, 'sconv_b': row(sconv_b),
          'w_out': w_out.astype(BF16), 'ln1_g': row(ln1_g), 'ln1_b': row(ln1_b),
          'w_ff_in': w_ff_in.astype(BF16), 'ffn_conv_w': ffn_conv_w, 'ffn_conv_b': row(ffn_conv_b),
          'w_ff_out': w_ff_out.astype(BF16), 'ln2_g': row(ln2_g), 'ln2_b': row(ln2_b)}
    prompt = {'init_layer': lambda l: 0,
              'h0': jnp.zeros((1, bp, 2 * D_STATE), F32),
              'sc0': jnp.zeros((1, bp, SUBLANES, D_SCONV), F32),
              'ff0': jnp.zeros((1, bp, SUBLANES, 2 * D_FF), F32),
              'bias': _rel_bias_table(rel_bias, Q_STEP, CHUNK)}
    sample = {'init_layer': lambda l: l,
              'h0': jnp.concatenate([state_ssm_re.reshape(DEPTH, bs, D_STATE),
                                     state_ssm_im.reshape(DEPTH, bs, D_STATE)], axis=2).astype(F32),
              'sc0': _pad_rows(cache_sconv), 'ff0': _pad_rows(cache_ffn_conv),
              'bias': _rel_bias_table(rel_bias, ss, ss), 'pk': pk_all, 'pv': pv_all}
    kv_p = tuple(jnp.zeros((DEPTH, bp, min(PAST, sp), D_ATT), F32) for _ in range(2))
    kv_s = tuple(jnp.zeros((DEPTH, bs, ss, D_ATT), F32) for _ in range(2))
    st_p = [[] for _ in range(4)]
    st_s = [[] for _ in range(4)]
    for l in range(DEPTH):
        ln_in = (l == 0, g_in, b_in)
        xp, kv_p, new_p = _trunk_layer(xp, ln_in, l, lp, prompt, kv_p, _tiles(bp, sp))
        xs, kv_s, new_s = _trunk_layer(xs, ln_in, l, lp, sample, kv_s, _tiles(bs, ss))
        for i in range(4):
            st_p[i].append(new_p[i])
            st_s[i].append(new_s[i])
    hre_p, him_p, sc_p, ff_p = [jnp.stack(a, axis=0) for a in st_p]
    hre_s, him_s, sc_s, ff_s = [jnp.stack(a, axis=0) for a in st_s]
    heads = lambda a: a.reshape(a.shape[:-1] + (N_HEADS, HEAD_DIM))
    return (xp, xs, heads(kv_p[0]), heads(kv_p[1]), heads(kv_s[0]), heads(kv_s[1]),
            hre_p, him_p, hre_s, him_s, sc_p, sc_s, ff_p, ff_s)
```

```python
import functools
import math

import jax
import jax.numpy as jnp
import numpy as np
from jax import lax
from jax.experimental import pallas as pl
from jax.experimental.pallas import tpu as pltpu

D_MODEL = 1024
DEPTH = 4
CHUNK = 64
N_HEADS = 8
HEAD_DIM = 64
D_ATT = N_HEADS * HEAD_DIM
N_PAST_CHUNKS = 8
PAST = N_PAST_CHUNKS * CHUNK
REL_MAX = 128
SSM_GROUP = 16
N_SSM_GROUPS = 16
D_SSM = SSM_GROUP * N_SSM_GROUPS
SSM_STATE = 64
D_STATE = N_SSM_GROUPS * SSM_STATE
D_SCONV = 256
CONV_K = 3
D_FF = 2048
D_QKV = 3 * D_ATT
D_REST = D_SSM + 3 * D_SCONV
ALPHA = (2 * DEPTH) ** 0.25
LN_EPS = 1e-5

SUBLANES = 8
VMEM_LIMIT = 56 * 1024 * 1024

F32 = jnp.float32
BF16 = jnp.bfloat16


def _cparams(sem):
    return pltpu.CompilerParams(dimension_semantics=sem, vmem_limit_bytes=VMEM_LIMIT)


def _layer_norm(x, g, b):
    mu = jnp.mean(x, axis=-1, keepdims=True)
    xc = x - mu
    var = jnp.mean(xc * xc, axis=-1, keepdims=True)
    return xc * lax.rsqrt(var + LN_EPS) * g + b


def _dot(a, b):
    return jnp.dot(a, b, preferred_element_type=F32)


def _layer_spec(shape, layer):
    return pl.BlockSpec((None,) + shape, lambda *_: (layer,) + (0,) * len(shape))


def _causal_conv3(cur, hist_ref, w_ref, b_ref, nb, tt):
    outs = []
    for s in range(nb):
        v = cur[s * tt:(s + 1) * tt]
        ext = jnp.concatenate([hist_ref[s], v], axis=0)
        m1 = pltpu.roll(ext, 1, 0)[SUBLANES:]
        m2 = pltpu.roll(ext, 2, 0)[SUBLANES:]
        hist_ref[s] = v[tt - SUBLANES:tt]
        outs.append(w_ref[0:1, :] * m2 + w_ref[1:2, :] * m1 + w_ref[2:3, :] * v + b_ref[...])
    return outs[0] if nb == 1 else jnp.concatenate(outs, axis=0)


def _gelu_tanh(x):
    c = math.sqrt(2.0 / math.pi)
    return 0.5 * x * (1.0 + jnp.tanh(c * (x + 0.044715 * (x * x * x))))


def _in_ssm_kernel(x_ref, gin_ref, bin_ref, w_ref, scw_ref, scb_ref, sc0_ref, h0_ref,
                   lbar_ref, bblk_ref, cblk_ref, d_ref, wglu_ref, kt_all_ref, vt_all_ref,
                   qkv_ref, kt_ref, vt_ref, ssm_ref, sc_ref, scn_ref, hn_ref,
                   hist_ref, h_ref, st_ref, *, tt, ln_in):
    del kt_all_ref, vt_all_ref
    ti = pl.program_id(1)
    nb = SUBLANES
    rows = nb * tt

    @pl.when(ti == 0)
    def _():
        hist_ref[...] = sc0_ref[...]
        h_ref[...] = h0_ref[...]

    x = x_ref[...].reshape(rows, D_MODEL)
    if ln_in:
        x = _layer_norm(x, gin_ref[...], bin_ref[...])
    xb = x.astype(BF16)
    u = _dot(xb, w_ref[:, D_QKV:D_QKV + D_SSM])
    ut = jnp.swapaxes(u.reshape(nb, tt, D_SSM), 0, 1).reshape(rows, D_SSM)
    bu = _dot(ut.astype(BF16), bblk_ref[...])

    q = _dot(xb, w_ref[:, 0:D_ATT]) * (HEAD_DIM ** -0.5)
    qkv_ref[:, :, 0:D_ATT] = q.astype(BF16).reshape(nb, tt, D_ATT)
    for tail_ref, lo in ((kt_ref, D_ATT), (vt_ref, 2 * D_ATT)):
        kv = _dot(xb, w_ref[:, lo:lo + D_ATT])
        qkv_ref[:, :, lo:lo + D_ATT] = kv.astype(BF16).reshape(nb, tt, D_ATT)
        tail_ref[...] = kv.reshape(nb, tt, D_ATT)

    g = _dot(xb, w_ref[:, D_QKV + D_SSM:D_QKV + D_REST])
    gate_b = g[:, 0:D_SCONV]
    gate_c = g[:, D_SCONV:2 * D_SCONV]
    xv = g[:, 2 * D_SCONV:3 * D_SCONV]
    conv = _causal_conv3(gate_c * xv, hist_ref, scw_ref, scb_ref, nb, tt)
    sc_ref[...] = (gate_b * conv).astype(BF16).reshape(nb, tt, D_SCONV)
    scn_ref[...] = hist_ref[...]

    lr = jnp.broadcast_to(lbar_ref[0:1, :], (SUBLANES, D_STATE))
    li = jnp.broadcast_to(lbar_ref[1:2, :], (SUBLANES, D_STATE))
    hr = h_ref[:, 0:D_STATE]
    hi = h_ref[:, D_STATE:2 * D_STATE]
    for t in range(tt):
        r = slice(t * SUBLANES, (t + 1) * SUBLANES)
        hr, hi = (lr * hr - li * hi + bu[r, 0:D_STATE],
                  lr * hi + li * hr + bu[r, D_STATE:2 * D_STATE])
        st_ref[r, 0:D_STATE] = hr
        st_ref[r, D_STATE:2 * D_STATE] = hi
    y = _dot(st_ref[...].astype(BF16), cblk_ref[...]) + d_ref[...] * ut
    z = _gelu_tanh(y)
    gate = jax.nn.sigmoid(_dot(z.astype(BF16), wglu_ref[...]))
    h_ref[:, 0:D_STATE] = hr
    h_ref[:, D_STATE:2 * D_STATE] = hi
    hn_ref[:, 0:D_STATE] = hr
    hn_ref[:, D_STATE:2 * D_STATE] = hi
    out = (z * gate).reshape(tt, nb, D_SSM)
    ssm_ref[...] = jnp.swapaxes(out, 0, 1).astype(BF16)


def _in_ssm_call(x, ln_in, layer, init_layer, w_in, scw, scb, sc0, h0, lbar, bblk, cblk, d, wglu,
                 kt_all, vt_all, tt):
    apply_ln, g_in, b_in = ln_in
    bsz, s, _ = x.shape
    keep = kt_all.shape[2]
    nb = SUBLANES
    nt = s // tt
    tail_first = nt - keep // tt
    blk = lambda b, t: (b, t, 0)
    tail = lambda b, t: (layer, b, jnp.maximum(t - tail_first, 0), 0)
    bonly3 = lambda b, t: (b, 0, 0)
    bonly2 = lambda b, t: (b, 0)
    const2 = lambda b, t: (0, 0)
    init3 = lambda b, t: (init_layer, b, 0, 0)
    init2 = lambda b, t: (init_layer, b, 0)
    operands = (x, g_in, b_in, w_in, scw, scb, sc0, h0, lbar, bblk, cblk, d, wglu, kt_all, vt_all)
    return pl.pallas_call(
        functools.partial(_in_ssm_kernel, tt=tt, ln_in=apply_ln),
        grid=(bsz // nb, nt),
        in_specs=[pl.BlockSpec((nb, tt, D_MODEL), blk),
                  pl.BlockSpec((1, D_MODEL), const2),
                  pl.BlockSpec((1, D_MODEL), const2),
                  _layer_spec((D_MODEL, D_QKV + D_REST), layer),
                  _layer_spec((CONV_K, D_SCONV), layer),
                  _layer_spec((1, D_SCONV), layer),
                  pl.BlockSpec((None, nb, SUBLANES, D_SCONV), init3),
                  pl.BlockSpec((None, nb, 2 * D_STATE), init2),
                  _layer_spec((2, D_STATE), layer),
                  _layer_spec((D_SSM, 2 * D_STATE), layer),
                  _layer_spec((2 * D_STATE, D_SSM), layer),
                  _layer_spec((1, D_SSM), layer),
                  _layer_spec((D_SSM, D_SSM), layer),
                  pl.BlockSpec(memory_space=pl.ANY),
                  pl.BlockSpec(memory_space=pl.ANY)],
        out_specs=[pl.BlockSpec((nb, tt, D_QKV), blk),
                   pl.BlockSpec((None, nb, tt, D_ATT), tail),
                   pl.BlockSpec((None, nb, tt, D_ATT), tail),
                   pl.BlockSpec((nb, tt, D_SSM), blk),
                   pl.BlockSpec((nb, tt, D_SCONV), blk),
                   pl.BlockSpec((nb, SUBLANES, D_SCONV), bonly3),
                   pl.BlockSpec((nb, 2 * D_STATE), bonly2)],
        out_shape=[jax.ShapeDtypeStruct((bsz, s, D_QKV), BF16),
                   jax.ShapeDtypeStruct(kt_all.shape, F32),
                   jax.ShapeDtypeStruct(vt_all.shape, F32),
                   jax.ShapeDtypeStruct((bsz, s, D_SSM), BF16),
                   jax.ShapeDtypeStruct((bsz, s, D_SCONV), BF16),
                   jax.ShapeDtypeStruct((bsz, SUBLANES, D_SCONV), F32),
                   jax.ShapeDtypeStruct((bsz, 2 * D_STATE), F32)],
        scratch_shapes=[pltpu.VMEM((nb, SUBLANES, D_SCONV), F32),
                        pltpu.VMEM((nb, 2 * D_STATE), F32),
                        pltpu.VMEM((nb * tt, 2 * D_STATE), F32)],
        input_output_aliases={len(operands) - 2: 1, len(operands) - 1: 2},
        compiler_params=_cparams(("parallel", "arbitrary")),
        name="in_ssm",
    )(*operands)


HEADS_PER_GROUP = 4
GROUP_W = HEADS_PER_GROUP * HEAD_DIM
N_GROUPS = N_HEADS // HEADS_PER_GROUP
Q_STEP = 4 * CHUNK
K_WIN = PAST + Q_STEP


def _softmax_unnormalised(sc):
    m = jnp.max(sc, axis=-1, keepdims=True)
    e = jnp.exp(sc - m)
    return e.astype(BF16), jnp.sum(e, axis=-1, keepdims=True)


def _group_head_of_lane(shape):
    lane = lax.broadcasted_iota(jnp.int32, shape, len(shape) - 1)
    return (lane // HEAD_DIM) % HEADS_PER_GROUP


def _softmax_banded(sc, bias):
    half, skip = Q_STEP // 2, 2 * CHUNK
    zeros = jnp.zeros((half, skip), BF16)
    e_top, l_top = _softmax_unnormalised(sc[0:half, 0:K_WIN - skip] + bias[0:half, 0:K_WIN - skip])
    e_bot, l_bot = _softmax_unnormalised(sc[half:Q_STEP, skip:K_WIN] + bias[half:Q_STEP, skip:K_WIN])
    e = jnp.concatenate([jnp.concatenate([e_top, zeros], axis=1),
                         jnp.concatenate([zeros, e_bot], axis=1)], axis=0)
    return e, jnp.concatenate([l_top, l_bot], axis=0)


def _attn_group(q4, k4, v_of_head, bias_of_head):
    head = _group_head_of_lane(q4.shape)
    full_window = q4.shape[0] == Q_STEP and k4.shape[0] == K_WIN
    acc = None
    scale = None
    for h in range(HEADS_PER_GROUP):
        qm = jnp.where(head == h, q4, jnp.zeros_like(q4))
        sc = lax.dot_general(qm, k4, (((1,), (1,)), ((), ())), preferred_element_type=F32)
        if full_window:
            e, l = _softmax_banded(sc, bias_of_head(h))
        else:
            e, l = _softmax_unnormalised(sc + bias_of_head(h))
        pv = _dot(e, v_of_head(h))
        acc = pv if acc is None else acc + pv
        inv = 1.0 / l
        scale = inv if scale is None else jnp.where(head == h, inv, scale)
    return acc * scale


def _attn_prompt_kernel(qkv_ref, bias_ref, o_ref, vmask_ref, *, s):
    v = qkv_ref[:, 2 * D_ATT:3 * D_ATT]
    head = _group_head_of_lane(v.shape)
    for h in range(HEADS_PER_GROUP):
        vmask_ref[h] = jnp.where(head == h, v, jnp.zeros_like(v))

    def step(r0, k0, nk):
        for g in range(N_GROUPS):
            cols = slice(g * GROUP_W, (g + 1) * GROUP_W)
            kcols = slice(D_ATT + g * GROUP_W, D_ATT + (g + 1) * GROUP_W)
            out = _attn_group(
                qkv_ref[pl.ds(r0, Q_STEP), cols],
                qkv_ref[pl.ds(k0, nk), kcols],
                lambda h: vmask_ref[h, pl.ds(k0, nk), cols],
                lambda h: bias_ref[g * HEADS_PER_GROUP + h, :, K_WIN - nk:K_WIN])
            o_ref[pl.ds(r0, Q_STEP), cols] = out.astype(BF16)

    n_steps = s // Q_STEP
    n_head_steps = min(PAST // Q_STEP, n_steps)
    for j in range(n_head_steps):
        step(j * Q_STEP, 0, (j + 1) * Q_STEP)

    def body(j, carry):
        r0 = pl.multiple_of(j * Q_STEP, Q_STEP)
        step(r0, pl.multiple_of(r0 - PAST, Q_STEP), K_WIN)
        return carry

    if n_steps > n_head_steps:
        lax.fori_loop(n_head_steps, n_steps, body, 0)


def _attn_prompt_call(qkv, layer, bias):
    bsz, s, _ = qkv.shape
    assert s % Q_STEP == 0
    bsel = lambda b: (b, 0, 0)
    return pl.pallas_call(
        functools.partial(_attn_prompt_kernel, s=s),
        grid=(bsz,),
        in_specs=[pl.BlockSpec((None, s, D_QKV), bsel),
                  _layer_spec((N_HEADS, Q_STEP, K_WIN), layer)],
        out_specs=pl.BlockSpec((None, s, D_ATT), bsel),
        out_shape=jax.ShapeDtypeStruct((bsz, s, D_ATT), BF16),
        scratch_shapes=[pltpu.VMEM((HEADS_PER_GROUP, s, D_ATT), BF16)],
        compiler_params=_cparams(("parallel",)),
        name="attention_prompt",
    )(qkv, bias)


def _attn_sample_kernel(qkv_ref, pkt_ref, pvt_ref, bias_ref, o_ref, *, s):
    contract_last = (((1,), (1,)), ((), ()))
    for g in range(N_GROUPS):
        cols = slice(g * GROUP_W, (g + 1) * GROUP_W)
        q4 = qkv_ref[:, cols]
        k_new = qkv_ref[:, D_ATT + g * GROUP_W:D_ATT + (g + 1) * GROUP_W]
        v_new = qkv_ref[:, 2 * D_ATT + g * GROUP_W:2 * D_ATT + (g + 1) * GROUP_W]
        kt_past = pkt_ref[cols, :].astype(BF16)
        vt_past = pvt_ref[cols, :].astype(BF16)
        head = _group_head_of_lane(q4.shape)
        out = None
        scale = None
        for h in range(HEADS_PER_GROUP):
            qm = jnp.where(head == h, q4, jnp.zeros_like(q4))
            bias = bias_ref[g * HEADS_PER_GROUP + h]
            sc_past = _dot(qm, kt_past) + bias[:, 0:PAST]
            sc_new = lax.dot_general(qm, k_new, contract_last, preferred_element_type=F32) + bias[:, PAST:]
            m = jnp.maximum(jnp.max(sc_past, axis=-1, keepdims=True), jnp.max(sc_new, axis=-1, keepdims=True))
            e_past = jnp.exp(sc_past - m)
            e_new = jnp.exp(sc_new - m)
            l = jnp.sum(e_past, axis=-1, keepdims=True) + jnp.sum(e_new, axis=-1, keepdims=True)
            pv = (lax.dot_general(e_past.astype(BF16), vt_past, contract_last, preferred_element_type=F32)
                  + _dot(e_new.astype(BF16), v_new))
            inv = 1.0 / l
            out = pv if out is None else jnp.where(head == h, pv, out)
            scale = inv if scale is None else jnp.where(head == h, inv, scale)
        o_ref[:, cols] = (out * scale).astype(BF16)


def _attn_sample_call(qkv, layer, pkt, pvt, bias):
    bsz, s, _ = qkv.shape
    band = PAST + s
    bsel = lambda b: (b, 0, 0)
    return pl.pallas_call(
        functools.partial(_attn_sample_kernel, s=s),
        grid=(bsz,),
        in_specs=[pl.BlockSpec((None, s, D_QKV), bsel),
                  pl.BlockSpec((None, None, D_ATT, PAST), lambda b: (layer, b, 0, 0)),
                  pl.BlockSpec((None, None, D_ATT, PAST), lambda b: (layer, b, 0, 0)),
                  _layer_spec((N_HEADS, s, band), layer)],
        out_specs=pl.BlockSpec((None, s, D_ATT), bsel),
        out_shape=jax.ShapeDtypeStruct((bsz, s, D_ATT), BF16),
        compiler_params=_cparams(("parallel",)),
        name="attention_sample",
    )(qkv, pkt, pvt, bias)


def _mix_ffn_kernel(att_ref, ssm_ref, sc_ref, x_ref, gin_ref, bin_ref, wo_ref, g1_ref, b1_ref,
                    w1_ref, cw_ref, cb_ref, c0_ref, w2_ref, g2_ref, b2_ref,
                    o_ref, cn_ref, hist_ref, *, nb, tt, ln_in):
    ti = pl.program_id(1)
    rows = nb * tt

    @pl.when(ti == 0)
    def _():
        hist_ref[...] = c0_ref[...]

    x = x_ref[...].reshape(rows, D_MODEL)
    if ln_in:
        x = _layer_norm(x, gin_ref[...], bin_ref[...])
    mix = _dot(att_ref[...].reshape(rows, D_ATT), wo_ref[0:D_ATT, :])
    mix = mix + _dot(ssm_ref[...].reshape(rows, D_SSM), wo_ref[D_ATT:D_ATT + D_SSM, :])
    mix = mix + _dot(sc_ref[...].reshape(rows, D_SCONV), wo_ref[D_ATT + D_SSM:D_MODEL, :])
    x1 = _layer_norm(ALPHA * x + mix, g1_ref[...], b1_ref[...])

    up = _dot(x1.astype(BF16), w1_ref[...])
    conv = _causal_conv3(up, hist_ref, cw_ref, cb_ref, nb, tt)
    cn_ref[...] = hist_ref[...]
    gate = conv[:, 0:D_FF]
    val = conv[:, D_FF:2 * D_FF]
    h = (gate * jax.nn.sigmoid(gate) * val).astype(BF16)
    y = _dot(h, w2_ref[...])
    o_ref[...] = _layer_norm(ALPHA * x1 + y, g2_ref[...], b2_ref[...]).reshape(nb, tt, D_MODEL)


def _mix_ffn_call(att, ssm, sc, x, ln_in, layer, init_layer, w_out, g1, b1, w1, cw, cb, c0, w2, g2, b2,
                  nb, tt):
    apply_ln, g_in, b_in = ln_in
    bsz, s, _ = x.shape
    blk = lambda b, t: (b, t, 0)
    bonly = lambda b, t: (b, 0, 0)
    const2 = lambda b, t: (0, 0)
    vec = pl.BlockSpec((1, D_MODEL), const2)
    lvec = _layer_spec((1, D_MODEL), layer)
    return pl.pallas_call(
        functools.partial(_mix_ffn_kernel, nb=nb, tt=tt, ln_in=apply_ln),
        grid=(bsz // nb, s // tt),
        in_specs=[pl.BlockSpec((nb, tt, D_ATT), blk),
                  pl.BlockSpec((nb, tt, D_SSM), blk),
                  pl.BlockSpec((nb, tt, D_SCONV), blk),
                  pl.BlockSpec((nb, tt, D_MODEL), blk), vec, vec,
                  _layer_spec((D_MODEL, D_MODEL), layer), lvec, lvec,
                  _layer_spec((D_MODEL, 2 * D_FF), layer),
                  _layer_spec((CONV_K, 2 * D_FF), layer),
                  _layer_spec((1, 2 * D_FF), layer),
                  pl.BlockSpec((None, nb, SUBLANES, 2 * D_FF), lambda b, t: (init_layer, b, 0, 0)),
                  _layer_spec((D_FF, D_MODEL), layer), lvec, lvec],
        out_specs=[pl.BlockSpec((nb, tt, D_MODEL), blk),
                   pl.BlockSpec((nb, SUBLANES, 2 * D_FF), bonly)],
        out_shape=[jax.ShapeDtypeStruct((bsz, s, D_MODEL), F32),
                   jax.ShapeDtypeStruct((bsz, SUBLANES, 2 * D_FF), F32)],
        scratch_shapes=[pltpu.VMEM((nb, SUBLANES, 2 * D_FF), F32)],
        compiler_params=_cparams(("parallel", "arbitrary")),
        name="mix_ffn",
    )(att, ssm, sc, x, g_in, b_in, w_out, g1, b1, w1, cw, cb, c0, w2, g2, b2)


def _rel_bias_table(rel_bias, tq, chunk):
    depth, n_heads, _ = rel_bias.shape
    nk = PAST + tq
    period = nk + tq
    dist = (nk - 1) - jnp.arange(period)
    by_lag = rel_bias[:, :, jnp.clip(dist, -(CHUNK - 1), REL_MAX) + (CHUNK - 1)].astype(F32)
    rolled = jnp.roll(by_lag, -(tq - 1), axis=-1)
    bias = jnp.tile(rolled, (1, 1, tq))[:, :, :tq * (period - 1)]
    bias = bias.reshape(depth, n_heads, tq, period - 1)[:, :, :, :nk]
    if tq == chunk:
        return bias
    t_idx = np.arange(tq)[:, None]
    s_idx = np.arange(nk)[None, :]
    key_chunk = s_idx // chunk - t_idx // chunk
    in_band = (key_chunk >= 0) & (key_chunk <= N_PAST_CHUNKS)
    return jnp.where(in_band[None, None], bias, -jnp.inf)


def _ssm_params(lam_re, lam_im, log_dt, b_re, b_im, c_re, c_im):
    depth = lam_re.shape[0]
    lam = lax.complex(lam_re.astype(F32), lam_im.astype(F32))
    dt = jnp.exp(log_dt.astype(F32))[:, :, None]
    lbar = jnp.exp(lam * dt)
    bbar = ((lbar - 1.0) / lam)[..., None] * lax.complex(b_re.astype(F32), b_im.astype(F32))
    eye = jnp.eye(N_SSM_GROUPS, dtype=F32)

    def in_map(m):
        return jnp.einsum('lgpc,gh->lgchp', m, eye).reshape(depth, D_SSM, D_STATE)

    def out_map(m):
        return jnp.einsum('lgcp,gh->lgphc', m, eye).reshape(depth, D_STATE, D_SSM)

    bblk = jnp.concatenate([in_map(bbar.real), in_map(bbar.imag)], axis=2).astype(BF16)
    cblk = jnp.concatenate([out_map(c_re.astype(F32)), out_map(-c_im.astype(F32))], axis=1).astype(BF16)
    lbar2 = jnp.stack([lbar.real.reshape(depth, D_STATE), lbar.imag.reshape(depth, D_STATE)], axis=1)
    return lbar2, bblk, cblk


def _pad_rows(buf):
    return jnp.pad(buf.astype(F32), ((0, 0), (0, 0), (SUBLANES - (CONV_K - 1), 0), (0, 0)))


def _trunk_layer(x, ln_in, layer, lp, group, kv_tails, tiles):
    bsz, s, _ = x.shape
    tt_in, nb_ffn, tt_ffn = tiles
    init_layer = group['init_layer'](layer)

    qkv, kt_all, vt_all, ssm_out, sconv_out, sc_new, h_new = _in_ssm_call(
        x, ln_in, layer, init_layer, lp['w_in'], lp['sconv_w'], lp['sconv_b'], group['sc0'], group['h0'],
        lp['lbar'], lp['bblk'], lp['cblk'], lp['d'], lp['w_glu'], *kv_tails, tt_in)
    if 'pk' in group:
        att = _attn_sample_call(qkv, layer, group['pk'], group['pv'], group['bias'])
    else:
        att = _attn_prompt_call(qkv, layer, group['bias'])
    x2, ff_new = _mix_ffn_call(att, ssm_out, sconv_out, x, ln_in, layer, init_layer,
                               lp['w_out'], lp['ln1_g'], lp['ln1_b'],
                               lp['w_ff_in'], lp['ffn_conv_w'], lp['ffn_conv_b'], group['ff0'],
                               lp['w_ff_out'], lp['ln2_g'], lp['ln2_b'], nb_ffn, tt_ffn)
    new = (h_new[:, 0:D_STATE].reshape(bsz, N_SSM_GROUPS, SSM_STATE),
           h_new[:, D_STATE:].reshape(bsz, N_SSM_GROUPS, SSM_STATE),
           sc_new[:, SUBLANES - (CONV_K - 1):], ff_new[:, SUBLANES - (CONV_K - 1):])
    return x2, (kt_all, vt_all), new


ROW_TILE = 512
IN_SSM_POSITIONS = 128


def _tiles(bsz, s):
    tt_in = min(IN_SSM_POSITIONS, s)
    tt_ffn = min(ROW_TILE, s)
    nb_ffn = min(bsz, ROW_TILE // tt_ffn)
    return tt_in, nb_ffn, tt_ffn


def kernel(x_prompt, x_sample, cache_k, cache_v, state_ssm_re, state_ssm_im, cache_sconv, cache_ffn_conv, ln_in_g, ln_in_b, w_in, rel_bias, ssm_lam_re, ssm_lam_im, ssm_log_dt, ssm_b_re, ssm_b_im, ssm_c_re, ssm_c_im, ssm_d, w_glu, sconv_w, sconv_b, w_out, ln1_g, ln1_b, w_ff_in, ffn_conv_w, ffn_conv_b, w_ff_out, ln2_g, ln2_b):
    bp, sp, _ = x_prompt.shape
    bs, ss, _ = x_sample.shape
    g_in = ln_in_g.reshape(1, D_MODEL)
    b_in = ln_in_b.reshape(1, D_MODEL)
    xp, xs = x_prompt, x_sample
    to_feature_major = lambda c: jnp.transpose(c, (0, 1, 3, 4, 2)).reshape(DEPTH, bs, D_ATT, PAST)
    pk_all = to_feature_major(cache_k)
    pv_all = to_feature_major(cache_v)
    lbar, bblk, cblk = _ssm_params(ssm_lam_re, ssm_lam_im, ssm_log_dt, ssm_b_re, ssm_b_im, ssm_c_re, ssm_c_im)
    row = lambda a: a.reshape(DEPTH, 1, a.shape[-1])
    lp = {'w_in': w_in.astype(BF16), 'lbar': lbar, 'bblk': bblk, 'cblk': cblk,
          'd': row(ssm_d), 'w_glu': w_glu.astype(BF16),
          'sconv_w': sconv_w, 'sconv_b': row(sconv_b),
          'w_out': w_out.astype(BF16), 'ln1_g': row(ln1_g), 'ln1_b': row(ln1_b),
          'w_ff_in': w_ff_in.astype(BF16), 'ffn_conv_w': ffn_conv_w, 'ffn_conv_b': row(ffn_conv_b),
          'w_ff_out': w_ff_out.astype(BF16), 'ln2_g': row(ln2_g), 'ln2_b': row(ln2_b)}
    prompt = {'init_layer': lambda l: 0,
              'h0': jnp.zeros((1, bp, 2 * D_STATE), F32),
              'sc0': jnp.zeros((1, bp, SUBLANES, D_SCONV), F32),
              'ff0': jnp.zeros((1, bp, SUBLANES, 2 * D_FF), F32),
              'bias': _rel_bias_table(rel_bias, Q_STEP, CHUNK)}
    sample = {'init_layer': lambda l: l,
              'h0': jnp.concatenate([state_ssm_re.reshape(DEPTH, bs, D_STATE),
                                     state_ssm_im.reshape(DEPTH, bs, D_STATE)], axis=2).astype(F32),
              'sc0': _pad_rows(cache_sconv), 'ff0': _pad_rows(cache_ffn_conv),
              'bias': _rel_bias_table(rel_bias, ss, ss), 'pk': pk_all, 'pv': pv_all}
    kv_p = tuple(jnp.zeros((DEPTH, bp, min(PAST, sp), D_ATT), F32) for _ in range(2))
    kv_s = tuple(jnp.zeros((DEPTH, bs, ss, D_ATT), F32) for _ in range(2))
    st_p = [[] for _ in range(4)]
    st_s = [[] for _ in range(4)]
    for l in range(DEPTH):
        ln_in = (l == 0, g_in, b_in)
        xp, kv_p, new_p = _trunk_layer(xp, ln_in, l, lp, prompt, kv_p, _tiles(bp, sp))
        xs, kv_s, new_s = _trunk_layer(xs, ln_in, l, lp, sample, kv_s, _tiles(bs, ss))
        for i in range(4):
            st_p[i].append(new_p[i])
            st_s[i].append(new_s[i])
    hre_p, him_p, sc_p, ff_p = [jnp.stack(a, axis=0) for a in st_p]
    hre_s, him_s, sc_s, ff_s = [jnp.stack(a, axis=0) for a in st_s]
    heads = lambda a: a.reshape(a.shape[:-1] + (N_HEADS, HEAD_DIM))
    return (xp, xs, heads(kv_p[0]), heads(kv_p[1]), heads(kv_s[0]), heads(kv_s[1]),
            hre_p, him_p, hre_s, him_s, sc_p, sc_s, ff_p, ff_s)
```

```python
import functools
import math

import jax
import jax.numpy as jnp
import numpy as np
from jax import lax
from jax.experimental import pallas as pl
from jax.experimental.pallas import tpu as pltpu

D_MODEL = 1024
DEPTH = 4
CHUNK = 64
N_HEADS = 8
HEAD_DIM = 64
D_ATT = N_HEADS * HEAD_DIM
N_PAST_CHUNKS = 8
PAST = N_PAST_CHUNKS * CHUNK
REL_MAX = 128
SSM_GROUP = 16
N_SSM_GROUPS = 16
D_SSM = SSM_GROUP * N_SSM_GROUPS
SSM_STATE = 64
D_STATE = N_SSM_GROUPS * SSM_STATE
D_SCONV = 256
CONV_K = 3
D_FF = 2048
D_QKV = 3 * D_ATT
D_REST = D_SSM + 3 * D_SCONV
ALPHA = (2 * DEPTH) ** 0.25
LN_EPS = 1e-5

SUBLANES = 8
VMEM_LIMIT = 56 * 1024 * 1024

F32 = jnp.float32
BF16 = jnp.bfloat16


def _cparams(sem):
    return pltpu.CompilerParams(dimension_semantics=sem, vmem_limit_bytes=VMEM_LIMIT)


def _layer_norm(x, g, b):
    mu = jnp.mean(x, axis=-1, keepdims=True)
    xc = x - mu
    var = jnp.mean(xc * xc, axis=-1, keepdims=True)
    return xc * lax.rsqrt(var + LN_EPS) * g + b


def _dot(a, b):
    return jnp.dot(a, b, preferred_element_type=F32)


def _layer_spec(shape, layer):
    return pl.BlockSpec((None,) + shape, lambda *_: (layer,) + (0,) * len(shape))


def _causal_conv3(cur, hist_ref, w_ref, b_ref, nb, tt):
    outs = []
    for s in range(nb):
        v = cur[s * tt:(s + 1) * tt]
        ext = jnp.concatenate([hist_ref[s], v], axis=0)
        m1 = pltpu.roll(ext, 1, 0)[SUBLANES:]
        m2 = pltpu.roll(ext, 2, 0)[SUBLANES:]
        hist_ref[s] = v[tt - SUBLANES:tt]
        outs.append(w_ref[0:1, :] * m2 + w_ref[1:2, :] * m1 + w_ref[2:3, :] * v + b_ref[...])
    return outs[0] if nb == 1 else jnp.concatenate(outs, axis=0)


def _gelu_tanh(x):
    c = math.sqrt(2.0 / math.pi)
    return 0.5 * x * (1.0 + jnp.tanh(c * (x + 0.044715 * (x * x * x))))


def _in_ssm_kernel(x_ref, gin_ref, bin_ref, w_ref, scw_ref, scb_ref, sc0_ref, h0_ref,
                   lbar_ref, bblk_ref, cblk_ref, d_ref, wglu_ref, kt_all_ref, vt_all_ref,
                   qkv_ref, kt_ref, vt_ref, ssm_ref, sc_ref, scn_ref, hn_ref,
                   hist_ref, h_ref, st_ref, *, tt, ln_in):
    del kt_all_ref, vt_all_ref
    ti = pl.program_id(1)
    nb = SUBLANES
    rows = nb * tt

    @pl.when(ti == 0)
    def _():
        hist_ref[...] = sc0_ref[...]
        h_ref[...] = h0_ref[...]

    x = x_ref[...].reshape(rows, D_MODEL)
    if ln_in:
        x = _layer_norm(x, gin_ref[...], bin_ref[...])
    xb = x.astype(BF16)
    u = _dot(xb, w_ref[:, D_QKV:D_QKV + D_SSM])
    ut = jnp.swapaxes(u.reshape(nb, tt, D_SSM), 0, 1).reshape(rows, D_SSM)
    bu = _dot(ut.astype(BF16), bblk_ref[...])

    q = _dot(xb, w_ref[:, 0:D_ATT]) * (HEAD_DIM ** -0.5)
    qkv_ref[:, :, 0:D_ATT] = q.astype(BF16).reshape(nb, tt, D_ATT)
    for tail_ref, lo in ((kt_ref, D_ATT), (vt_ref, 2 * D_ATT)):
        kv = _dot(xb, w_ref[:, lo:lo + D_ATT])
        qkv_ref[:, :, lo:lo + D_ATT] = kv.astype(BF16).reshape(nb, tt, D_ATT)
        tail_ref[...] = kv.reshape(nb, tt, D_ATT)

    g = _dot(xb, w_ref[:, D_QKV + D_SSM:D_QKV + D_REST])
    gate_b = g[:, 0:D_SCONV]
    gate_c = g[:, D_SCONV:2 * D_SCONV]
    xv = g[:, 2 * D_SCONV:3 * D_SCONV]
    conv = _causal_conv3(gate_c * xv, hist_ref, scw_ref, scb_ref, nb, tt)
    sc_ref[...] = (gate_b * conv).astype(BF16).reshape(nb, tt, D_SCONV)
    scn_ref[...] = hist_ref[...]

    lr = jnp.broadcast_to(lbar_ref[0:1, :], (SUBLANES, D_STATE))
    li = jnp.broadcast_to(lbar_ref[1:2, :], (SUBLANES, D_STATE))
    hr = h_ref[:, 0:D_STATE]
    hi = h_ref[:, D_STATE:2 * D_STATE]
    for t in range(tt):
        r = slice(t * SUBLANES, (t + 1) * SUBLANES)
        hr, hi = (lr * hr - li * hi + bu[r, 0:D_STATE],
                  lr * hi + li * hr + bu[r, D_STATE:2 * D_STATE])
        st_ref[r, 0:D_STATE] = hr
        st_ref[r, D_STATE:2 * D_STATE] = hi
    y = _dot(st_ref[...].astype(BF16), cblk_ref[...]) + d_ref[...] * ut
    z = _gelu_tanh(y)
    gate = jax.nn.sigmoid(_dot(z.astype(BF16), wglu_ref[...]))
    h_ref[:, 0:D_STATE] = hr
    h_ref[:, D_STATE:2 * D_STATE] = hi
    hn_ref[:, 0:D_STATE] = hr
    hn_ref[:, D_STATE:2 * D_STATE] = hi
    out = (z * gate).reshape(tt, nb, D_SSM)
    ssm_ref[...] = jnp.swapaxes(out, 0, 1).astype(BF16)


def _in_ssm_call(x, ln_in, layer, init_layer, w_in, scw, scb, sc0, h0, lbar, bblk, cblk, d, wglu,
                 kt_all, vt_all, tt):
    apply_ln, g_in, b_in = ln_in
    bsz, s, _ = x.shape
    keep = kt_all.shape[2]
    nb = SUBLANES
    nt = s // tt
    tail_first = nt - keep // tt
    blk = lambda b, t: (b, t, 0)
    tail = lambda b, t: (layer, b, jnp.maximum(t - tail_first, 0), 0)
    bonly3 = lambda b, t: (b, 0, 0)
    bonly2 = lambda b, t: (b, 0)
    const2 = lambda b, t: (0, 0)
    init3 = lambda b, t: (init_layer, b, 0, 0)
    init2 = lambda b, t: (init_layer, b, 0)
    operands = (x, g_in, b_in, w_in, scw, scb, sc0, h0, lbar, bblk, cblk, d, wglu, kt_all, vt_all)
    return pl.pallas_call(
        functools.partial(_in_ssm_kernel, tt=tt, ln_in=apply_ln),
        grid=(bsz // nb, nt),
        in_specs=[pl.BlockSpec((nb, tt, D_MODEL), blk),
                  pl.BlockSpec((1, D_MODEL), const2),
                  pl.BlockSpec((1, D_MODEL), const2),
                  _layer_spec((D_MODEL, D_QKV + D_REST), layer),
                  _layer_spec((CONV_K, D_SCONV), layer),
                  _layer_spec((1, D_SCONV), layer),
                  pl.BlockSpec((None, nb, SUBLANES, D_SCONV), init3),
                  pl.BlockSpec((None, nb, 2 * D_STATE), init2),
                  _layer_spec((2, D_STATE), layer),
                  _layer_spec((D_SSM, 2 * D_STATE), layer),
                  _layer_spec((2 * D_STATE, D_SSM), layer),
                  _layer_spec((1, D_SSM), layer),
                  _layer_spec((D_SSM, D_SSM), layer),
                  pl.BlockSpec(memory_space=pl.ANY),
                  pl.BlockSpec(memory_space=pl.ANY)],
        out_specs=[pl.BlockSpec((nb, tt, D_QKV), blk),
                   pl.BlockSpec((None, nb, tt, D_ATT), tail),
                   pl.BlockSpec((None, nb, tt, D_ATT), tail),
                   pl.BlockSpec((nb, tt, D_SSM), blk),
                   pl.BlockSpec((nb, tt, D_SCONV), blk),
                   pl.BlockSpec((nb, SUBLANES, D_SCONV), bonly3),
                   pl.BlockSpec((nb, 2 * D_STATE), bonly2)],
        out_shape=[jax.ShapeDtypeStruct((bsz, s, D_QKV), BF16),
                   jax.ShapeDtypeStruct(kt_all.shape, F32),
                   jax.ShapeDtypeStruct(vt_all.shape, F32),
                   jax.ShapeDtypeStruct((bsz, s, D_SSM), BF16),
                   jax.ShapeDtypeStruct((bsz, s, D_SCONV), BF16),
                   jax.ShapeDtypeStruct((bsz, SUBLANES, D_SCONV), F32),
                   jax.ShapeDtypeStruct((bsz, 2 * D_STATE), F32)],
        scratch_shapes=[pltpu.VMEM((nb, SUBLANES, D_SCONV), F32),
                        pltpu.VMEM((nb, 2 * D_STATE), F32),
                        pltpu.VMEM((nb * tt, 2 * D_STATE), F32)],
        input_output_aliases={len(operands) - 2: 1, len(operands) - 1: 2},
        compiler_params=_cparams(("parallel", "arbitrary")),
        name="in_ssm",
    )(*operands)


HEADS_PER_GROUP = 4
GROUP_W = HEADS_PER_GROUP * HEAD_DIM
N_GROUPS = N_HEADS // HEADS_PER_GROUP
Q_STEP = 4 * CHUNK
K_WIN = PAST + Q_STEP


def _softmax_unnormalised(sc):
    m = jnp.max(sc, axis=-1, keepdims=True)
    e = jnp.exp(sc - m)
    return e.astype(BF16), jnp.sum(e, axis=-1, keepdims=True)


def _group_head_of_lane(shape):
    lane = lax.broadcasted_iota(jnp.int32, shape, len(shape) - 1)
    return (lane // HEAD_DIM) % HEADS_PER_GROUP


def _softmax_banded(sc, bias):
    half, skip = Q_STEP // 2, 2 * CHUNK
    zeros = jnp.zeros((half, skip), BF16)
    e_top, l_top = _softmax_unnormalised(sc[0:half, 0:K_WIN - skip] + bias[0:half, 0:K_WIN - skip])
    e_bot, l_bot = _softmax_unnormalised(sc[half:Q_STEP, skip:K_WIN] + bias[half:Q_STEP, skip:K_WIN])
    e = jnp.concatenate([jnp.concatenate([e_top, zeros], axis=1),
                         jnp.concatenate([zeros, e_bot], axis=1)], axis=0)
    return e, jnp.concatenate([l_top, l_bot], axis=0)


def _attn_group(q4, k4, v4, bias_of_head):
    head = _group_head_of_lane(q4.shape)
    full_window = q4.shape[0] == Q_STEP and k4.shape[0] == K_WIN
    out = None
    scale = None
    for h in range(HEADS_PER_GROUP):
        qm = jnp.where(head == h, q4, jnp.zeros_like(q4))
        sc = lax.dot_general(qm, k4, (((1,), (1,)), ((), ())), preferred_element_type=F32)
        if full_window:
            e, l = _softmax_banded(sc, bias_of_head(h))
        else:
            e, l = _softmax_unnormalised(sc + bias_of_head(h))
        pv = _dot(e, v4)
        inv = 1.0 / l
        out = pv if out is None else jnp.where(head == h, pv, out)
        scale = inv if scale is None else jnp.where(head == h, inv, scale)
    return out * scale


def _attn_prompt_kernel(qkv_ref, bias_ref, o_ref, *, s):
    def step(r0, k0, nk):
        for g in range(N_GROUPS):
            lo = g * GROUP_W
            out = _attn_group(
                qkv_ref[pl.ds(r0, Q_STEP), lo:lo + GROUP_W],
                qkv_ref[pl.ds(k0, nk), D_ATT + lo:D_ATT + lo + GROUP_W],
                qkv_ref[pl.ds(k0, nk), 2 * D_ATT + lo:2 * D_ATT + lo + GROUP_W],
                lambda h: bias_ref[g * HEADS_PER_GROUP + h, :, K_WIN - nk:K_WIN])
            o_ref[pl.ds(r0, Q_STEP), lo:lo + GROUP_W] = out.astype(BF16)

    n_steps = s // Q_STEP
    n_head_steps = min(PAST // Q_STEP, n_steps)
    for j in range(n_head_steps):
        step(j * Q_STEP, 0, (j + 1) * Q_STEP)

    def body(j, carry):
        r0 = pl.multiple_of(j * Q_STEP, Q_STEP)
        step(r0, pl.multiple_of(r0 - PAST, Q_STEP), K_WIN)
        return carry

    if n_steps > n_head_steps:
        lax.fori_loop(n_head_steps, n_steps, body, 0)


def _attn_prompt_call(qkv, layer, bias):
    bsz, s, _ = qkv.shape
    assert s % Q_STEP == 0
    bsel = lambda b: (b, 0, 0)
    return pl.pallas_call(
        functools.partial(_attn_prompt_kernel, s=s),
        grid=(bsz,),
        in_specs=[pl.BlockSpec((None, s, D_QKV), bsel),
                  _layer_spec((N_HEADS, Q_STEP, K_WIN), layer)],
        out_specs=pl.BlockSpec((None, s, D_ATT), bsel),
        out_shape=jax.ShapeDtypeStruct((bsz, s, D_ATT), BF16),
        compiler_params=_cparams(("parallel",)),
        name="attention_prompt",
    )(qkv, bias)


def _attn_sample_kernel(qkv_ref, pkt_ref, pvt_ref, bias_ref, o_ref, *, s):
    contract_last = (((1,), (1,)), ((), ()))
    for g in range(N_GROUPS):
        cols = slice(g * GROUP_W, (g + 1) * GROUP_W)
        q4 = qkv_ref[:, cols]
        k_new = qkv_ref[:, D_ATT + g * GROUP_W:D_ATT + (g + 1) * GROUP_W]
        v_new = qkv_ref[:, 2 * D_ATT + g * GROUP_W:2 * D_ATT + (g + 1) * GROUP_W]
        kt_past = pkt_ref[cols, :].astype(BF16)
        vt_past = pvt_ref[cols, :].astype(BF16)
        head = _group_head_of_lane(q4.shape)
        out = None
        scale = None
        for h in range(HEADS_PER_GROUP):
            qm = jnp.where(head == h, q4, jnp.zeros_like(q4))
            bias = bias_ref[g * HEADS_PER_GROUP + h]
            sc_past = _dot(qm, kt_past) + bias[:, 0:PAST]
            sc_new = lax.dot_general(qm, k_new, contract_last, preferred_element_type=F32) + bias[:, PAST:]
            m = jnp.maximum(jnp.max(sc_past, axis=-1, keepdims=True), jnp.max(sc_new, axis=-1, keepdims=True))
            e_past = jnp.exp(sc_past - m)
            e_new = jnp.exp(sc_new - m)
            l = jnp.sum(e_past, axis=-1, keepdims=True) + jnp.sum(e_new, axis=-1, keepdims=True)
            pv = (lax.dot_general(e_past.astype(BF16), vt_past, contract_last, preferred_element_type=F32)
                  + _dot(e_new.astype(BF16), v_new))
            inv = 1.0 / l
            out = pv if out is None else jnp.where(head == h, pv, out)
            scale = inv if scale is None else jnp.where(head == h, inv, scale)
        o_ref[:, cols] = (out * scale).astype(BF16)


def _attn_sample_call(qkv, layer, pkt, pvt, bias):
    bsz, s, _ = qkv.shape
    band = PAST + s
    bsel = lambda b: (b, 0, 0)
    return pl.pallas_call(
        functools.partial(_attn_sample_kernel, s=s),
        grid=(bsz,),
        in_specs=[pl.BlockSpec((None, s, D_QKV), bsel),
                  pl.BlockSpec((None, None, D_ATT, PAST), lambda b: (layer, b, 0, 0)),
                  pl.BlockSpec((None, None, D_ATT, PAST), lambda b: (layer, b, 0, 0)),
                  _layer_spec((N_HEADS, s, band), layer)],
        out_specs=pl.BlockSpec((None, s, D_ATT), bsel),
        out_shape=jax.ShapeDtypeStruct((bsz, s, D_ATT), BF16),
        compiler_params=_cparams(("parallel",)),
        name="attention_sample",
    )(qkv, pkt, pvt, bias)


def _mix_ffn_kernel(att_ref, ssm_ref, sc_ref, x_ref, gin_ref, bin_ref, wo_ref, g1_ref, b1_ref,
                    w1_ref, cw_ref, cb_ref, c0_ref, w2_ref, g2_ref, b2_ref,
                    o_ref, cn_ref, hist_ref, *, nb, tt, ln_in):
    ti = pl.program_id(1)
    rows = nb * tt

    @pl.when(ti == 0)
    def _():
        hist_ref[...] = c0_ref[...]

    x = x_ref[...].reshape(rows, D_MODEL)
    if ln_in:
        x = _layer_norm(x, gin_ref[...], bin_ref[...])
    mix = _dot(att_ref[...].reshape(rows, D_ATT), wo_ref[0:D_ATT, :])
    mix = mix + _dot(ssm_ref[...].reshape(rows, D_SSM), wo_ref[D_ATT:D_ATT + D_SSM, :])
    mix = mix + _dot(sc_ref[...].reshape(rows, D_SCONV), wo_ref[D_ATT + D_SSM:D_MODEL, :])
    x1 = _layer_norm(ALPHA * x + mix, g1_ref[...], b1_ref[...])

    up = _dot(x1.astype(BF16), w1_ref[...])
    conv = _causal_conv3(up, hist_ref, cw_ref, cb_ref, nb, tt)
    cn_ref[...] = hist_ref[...]
    gate = conv[:, 0:D_FF]
    val = conv[:, D_FF:2 * D_FF]
    h = (gate * jax.nn.sigmoid(gate) * val).astype(BF16)
    y = _dot(h, w2_ref[...])
    o_ref[...] = _layer_norm(ALPHA * x1 + y, g2_ref[...], b2_ref[...]).reshape(nb, tt, D_MODEL)


def _mix_ffn_call(att, ssm, sc, x, ln_in, layer, init_layer, w_out, g1, b1, w1, cw, cb, c0, w2, g2, b2,
                  nb, tt):
    apply_ln, g_in, b_in = ln_in
    bsz, s, _ = x.shape
    blk = lambda b, t: (b, t, 0)
    bonly = lambda b, t: (b, 0, 0)
    const2 = lambda b, t: (0, 0)
    vec = pl.BlockSpec((1, D_MODEL), const2)
    lvec = _layer_spec((1, D_MODEL), layer)
    return pl.pallas_call(
        functools.partial(_mix_ffn_kernel, nb=nb, tt=tt, ln_in=apply_ln),
        grid=(bsz // nb, s // tt),
        in_specs=[pl.BlockSpec((nb, tt, D_ATT), blk),
                  pl.BlockSpec((nb, tt, D_SSM), blk),
                  pl.BlockSpec((nb, tt, D_SCONV), blk),
                  pl.BlockSpec((nb, tt, D_MODEL), blk), vec, vec,
                  _layer_spec((D_MODEL, D_MODEL), layer), lvec, lvec,
                  _layer_spec((D_MODEL, 2 * D_FF), layer),
                  _layer_spec((CONV_K, 2 * D_FF), layer),
                  _layer_spec((1, 2 * D_FF), layer),
                  pl.BlockSpec((None, nb, SUBLANES, 2 * D_FF), lambda b, t: (init_layer, b, 0, 0)),
                  _layer_spec((D_FF, D_MODEL), layer), lvec, lvec],
        out_specs=[pl.BlockSpec((nb, tt, D_MODEL), blk),
                   pl.BlockSpec((nb, SUBLANES, 2 * D_FF), bonly)],
        out_shape=[jax.ShapeDtypeStruct((bsz, s, D_MODEL), F32),
                   jax.ShapeDtypeStruct((bsz, SUBLANES, 2 * D_FF), F32)],
        scratch_shapes=[pltpu.VMEM((nb, SUBLANES, 2 * D_FF), F32)],
        compiler_params=_cparams(("parallel", "arbitrary")),
        name="mix_ffn",
    )(att, ssm, sc, x, g_in, b_in, w_out, g1, b1, w1, cw, cb, c0, w2, g2, b2)


def _rel_bias_table(rel_bias, tq, chunk):
    depth, n_heads, _ = rel_bias.shape
    nk = PAST + tq
    period = nk + tq
    dist = (nk - 1) - jnp.arange(period)
    by_lag = rel_bias[:, :, jnp.clip(dist, -(CHUNK - 1), REL_MAX) + (CHUNK - 1)].astype(F32)
    rolled = jnp.roll(by_lag, -(tq - 1), axis=-1)
    bias = jnp.tile(rolled, (1, 1, tq))[:, :, :tq * (period - 1)]
    bias = bias.reshape(depth, n_heads, tq, period - 1)[:, :, :, :nk]
    if tq == chunk:
        return bias
    t_idx = np.arange(tq)[:, None]
    s_idx = np.arange(nk)[None, :]
    key_chunk = s_idx // chunk - t_idx // chunk
    in_band = (key_chunk >= 0) & (key_chunk <= N_PAST_CHUNKS)
    return jnp.where(in_band[None, None], bias, -jnp.inf)


def _ssm_params(lam_re, lam_im, log_dt, b_re, b_im, c_re, c_im):
    depth = lam_re.shape[0]
    lam = lax.complex(lam_re.astype(F32), lam_im.astype(F32))
    dt = jnp.exp(log_dt.astype(F32))[:, :, None]
    lbar = jnp.exp(lam * dt)
    bbar = ((lbar - 1.0) / lam)[..., None] * lax.complex(b_re.astype(F32), b_im.astype(F32))
    eye = jnp.eye(N_SSM_GROUPS, dtype=F32)

    def in_map(m):
        return jnp.einsum('lgpc,gh->lgchp', m, eye).reshape(depth, D_SSM, D_STATE)

    def out_map(m):
        return jnp.einsum('lgcp,gh->lgphc', m, eye).reshape(depth, D_STATE, D_SSM)

    bblk = jnp.concatenate([in_map(bbar.real), in_map(bbar.imag)], axis=2).astype(BF16)
    cblk = jnp.concatenate([out_map(c_re.astype(F32)), out_map(-c_im.astype(F32))], axis=1).astype(BF16)
    lbar2 = jnp.stack([lbar.real.reshape(depth, D_STATE), lbar.imag.reshape(depth, D_STATE)], axis=1)
    return lbar2, bblk, cblk


def _pad_rows(buf):
    return jnp.pad(buf.astype(F32), ((0, 0), (0, 0), (SUBLANES - (CONV_K - 1), 0), (0, 0)))


def _trunk_layer(x, ln_in, layer, lp, group, kv_tails, tiles):
    bsz, s, _ = x.shape
    tt_in, nb_ffn, tt_ffn = tiles
    init_layer = group['init_layer'](layer)

    qkv, kt_all, vt_all, ssm_out, sconv_out, sc_new, h_new = _in_ssm_call(
        x, ln_in, layer, init_layer, lp['w_in'], lp['sconv_w'], lp['sconv_b'], group['sc0'], group['h0'],
        lp['lbar'], lp['bblk'], lp['cblk'], lp['d'], lp['w_glu'], *kv_tails, tt_in)
    if 'pk' in group:
        att = _attn_sample_call(qkv, layer, group['pk'], group['pv'], group['bias'])
    else:
        att = _attn_prompt_call(qkv, layer, group['bias'])
    x2, ff_new = _mix_ffn_call(att, ssm_out, sconv_out, x, ln_in, layer, init_layer,
                               lp['w_out'], lp['ln1_g'], lp['ln1_b'],
                               lp['w_ff_in'], lp['ffn_conv_w'], lp['ffn_conv_b'], group['ff0'],
                               lp['w_ff_out'], lp['ln2_g'], lp['ln2_b'], nb_ffn, tt_ffn)
    new = (h_new[:, 0:D_STATE].reshape(bsz, N_SSM_GROUPS, SSM_STATE),
           h_new[:, D_STATE:].reshape(bsz, N_SSM_GROUPS, SSM_STATE),
           sc_new[:, SUBLANES - (CONV_K - 1):], ff_new[:, SUBLANES - (CONV_K - 1):])
    return x2, (kt_all, vt_all), new


ROW_TILE = 512
IN_SSM_POSITIONS = 128


def _tiles(bsz, s):
    tt_in = min(IN_SSM_POSITIONS, s)
    tt_ffn = min(ROW_TILE, s)
    nb_ffn = min(bsz, ROW_TILE // tt_ffn)
    return tt_in, nb_ffn, tt_ffn


def kernel(x_prompt, x_sample, cache_k, cache_v, state_ssm_re, state_ssm_im, cache_sconv, cache_ffn_conv, ln_in_g, ln_in_b, w_in, rel_bias, ssm_lam_re, ssm_lam_im, ssm_log_dt, ssm_b_re, ssm_b_im, ssm_c_re, ssm_c_im, ssm_d, w_glu, sconv_w, sconv_b, w_out, ln1_g, ln1_b, w_ff_in, ffn_conv_w, ffn_conv_b, w_ff_out, ln2_g, ln2_b):
    bp, sp, _ = x_prompt.shape
    bs, ss, _ = x_sample.shape
    g_in = ln_in_g.reshape(1, D_MODEL)
    b_in = ln_in_b.reshape(1, D_MODEL)
    xp, xs = x_prompt, x_sample
    to_feature_major = lambda c: jnp.transpose(c, (0, 1, 3, 4, 2)).reshape(DEPTH, bs, D_ATT, PAST)
    pk_all = to_feature_major(cache_k)
    pv_all = to_feature_major(cache_v)
    lbar, bblk, cblk = _ssm_params(ssm_lam_re, ssm_lam_im, ssm_log_dt, ssm_b_re, ssm_b_im, ssm_c_re, ssm_c_im)
    row = lambda a: a.reshape(DEPTH, 1, a.shape[-1])
    lp = {'w_in': w_in.astype(BF16), 'lbar': lbar, 'bblk': bblk, 'cblk': cblk,
          'd': row(ssm_d), 'w_glu': w_glu.astype(BF16),
          'sconv_w': sconv_w, 'sconv_b': row(sconv_b),
          'w_out': w_out.astype(BF16), 'ln1_g': row(ln1_g), 'ln1_b': row(ln1_b),
          'w_ff_in': w_ff_in.astype(BF16), 'ffn_conv_w': ffn_conv_w, 'ffn_conv_b': row(ffn_conv_b),
          'w_ff_out': w_ff_out.astype(BF16), 'ln2_g': row(ln2_g), 'ln2_b': row(ln2_b)}
    prompt = {'init_layer': lambda l: 0,
              'h0': jnp.zeros((1, bp, 2 * D_STATE), F32),
              'sc0': jnp.zeros((1, bp, SUBLANES, D_SCONV), F32),
              'ff0': jnp.zeros((1, bp, SUBLANES, 2 * D_FF), F32),
              'bias': _rel_bias_table(rel_bias, Q_STEP, CHUNK)}
    sample = {'init_layer': lambda l: l,
              'h0': jnp.concatenate([state_ssm_re.reshape(DEPTH, bs, D_STATE),
                                     state_ssm_im.reshape(DEPTH, bs, D_STATE)], axis=2).astype(F32),
              'sc0': _pad_rows(cache_sconv), 'ff0': _pad_rows(cache_ffn_conv),
              'bias': _rel_bias_table(rel_bias, ss, ss), 'pk': pk_all, 'pv': pv_all}
    kv_p = tuple(jnp.zeros((DEPTH, bp, min(PAST, sp), D_ATT), F32) for _ in range(2))
    kv_s = tuple(jnp.zeros((DEPTH, bs, ss, D_ATT), F32) for _ in range(2))
    st_p = [[] for _ in range(4)]
    st_s = [[] for _ in range(4)]
    for l in range(DEPTH):
        ln_in = (l == 0, g_in, b_in)
        xp, kv_p, new_p = _trunk_layer(xp, ln_in, l, lp, prompt, kv_p, _tiles(bp, sp))
        xs, kv_s, new_s = _trunk_layer(xs, ln_in, l, lp, sample, kv_s, _tiles(bs, ss))
        for i in range(4):
            st_p[i].append(new_p[i])
            st_s[i].append(new_s[i])
    hre_p, him_p, sc_p, ff_p = [jnp.stack(a, axis=0) for a in st_p]
    hre_s, him_s, sc_s, ff_s = [jnp.stack(a, axis=0) for a in st_s]
    heads = lambda a: a.reshape(a.shape[:-1] + (N_HEADS, HEAD_DIM))
    return (xp, xs, heads(kv_p[0]), heads(kv_p[1]), heads(kv_s[0]), heads(kv_s[1]),
            hre_p, him_p, hre_s, him_s, sc_p, sc_s, ff_p, ff_s)
```

```python
import functools
import math

import jax
import jax.numpy as jnp
import numpy as np
from jax import lax
from jax.experimental import pallas as pl
from jax.experimental.pallas import tpu as pltpu

D_MODEL = 1024
DEPTH = 4
CHUNK = 64
N_HEADS = 8
HEAD_DIM = 64
D_ATT = N_HEADS * HEAD_DIM
N_PAST_CHUNKS = 8
PAST = N_PAST_CHUNKS * CHUNK
REL_MAX = 128
SSM_GROUP = 16
N_SSM_GROUPS = 16
D_SSM = SSM_GROUP * N_SSM_GROUPS
SSM_STATE = 64
D_STATE = N_SSM_GROUPS * SSM_STATE
D_SCONV = 256
CONV_K = 3
D_FF = 2048
D_QKV = 3 * D_ATT
D_REST = D_SSM + 3 * D_SCONV
ALPHA = (2 * DEPTH) ** 0.25
LN_EPS = 1e-5

SUBLANES = 8
VMEM_LIMIT = 56 * 1024 * 1024

F32 = jnp.float32
BF16 = jnp.bfloat16


def _cparams(sem):
    return pltpu.CompilerParams(dimension_semantics=sem, vmem_limit_bytes=VMEM_LIMIT)


def _layer_norm(x, g, b):
    mu = jnp.mean(x, axis=-1, keepdims=True)
    xc = x - mu
    var = jnp.mean(xc * xc, axis=-1, keepdims=True)
    return xc * lax.rsqrt(var + LN_EPS) * g + b


def _dot(a, b):
    return jnp.dot(a, b, preferred_element_type=F32)


def _layer_spec(shape, layer):
    return pl.BlockSpec((None,) + shape, lambda *_: (layer,) + (0,) * len(shape))


def _causal_conv3(cur, hist_ref, w_ref, b_ref, nb, tt):
    outs = []
    for s in range(nb):
        v = cur[s * tt:(s + 1) * tt]
        ext = jnp.concatenate([hist_ref[s], v], axis=0)
        m1 = pltpu.roll(ext, 1, 0)[SUBLANES:]
        m2 = pltpu.roll(ext, 2, 0)[SUBLANES:]
        hist_ref[s] = v[tt - SUBLANES:tt]
        outs.append(w_ref[0:1, :] * m2 + w_ref[1:2, :] * m1 + w_ref[2:3, :] * v + b_ref[...])
    return outs[0] if nb == 1 else jnp.concatenate(outs, axis=0)


def _gelu_tanh(x):
    c = math.sqrt(2.0 / math.pi)
    return 0.5 * x * (1.0 + jnp.tanh(c * (x + 0.044715 * (x * x * x))))


def _in_ssm_kernel(x_ref, gin_ref, bin_ref, w_ref, scw_ref, scb_ref, sc0_ref, h0_ref,
                   lbar_ref, bblk_ref, cblk_ref, d_ref, wglu_ref, kt_all_ref, vt_all_ref,
                   qkv_ref, kt_ref, vt_ref, ssm_ref, sc_ref, scn_ref, hn_ref,
                   hist_ref, h_ref, st_ref, *, tt, ln_in):
    del kt_all_ref, vt_all_ref
    ti = pl.program_id(1)
    nb = SUBLANES
    rows = nb * tt

    @pl.when(ti == 0)
    def _():
        hist_ref[...] = sc0_ref[...]
        h_ref[...] = h0_ref[...]

    x = x_ref[...].reshape(rows, D_MODEL)
    if ln_in:
        x = _layer_norm(x, gin_ref[...], bin_ref[...])
    xb = x.astype(BF16)
    u = _dot(xb, w_ref[:, D_QKV:D_QKV + D_SSM])
    ut = jnp.swapaxes(u.reshape(nb, tt, D_SSM), 0, 1).reshape(rows, D_SSM)
    bu = _dot(ut.astype(BF16), bblk_ref[...])

    q = _dot(xb, w_ref[:, 0:D_ATT]) * (HEAD_DIM ** -0.5)
    qkv_ref[:, :, 0:D_ATT] = q.astype(BF16).reshape(nb, tt, D_ATT)
    for tail_ref, lo in ((kt_ref, D_ATT), (vt_ref, 2 * D_ATT)):
        kv = _dot(xb, w_ref[:, lo:lo + D_ATT])
        qkv_ref[:, :, lo:lo + D_ATT] = kv.astype(BF16).reshape(nb, tt, D_ATT)
        tail_ref[...] = kv.reshape(nb, tt, D_ATT)

    g = _dot(xb, w_ref[:, D_QKV + D_SSM:D_QKV + D_REST])
    gate_b = g[:, 0:D_SCONV]
    gate_c = g[:, D_SCONV:2 * D_SCONV]
    xv = g[:, 2 * D_SCONV:3 * D_SCONV]
    conv = _causal_conv3(gate_c * xv, hist_ref, scw_ref, scb_ref, nb, tt)
    sc_ref[...] = (gate_b * conv).astype(BF16).reshape(nb, tt, D_SCONV)
    scn_ref[...] = hist_ref[...]

    lr = jnp.broadcast_to(lbar_ref[0:1, :], (SUBLANES, D_STATE))
    li = jnp.broadcast_to(lbar_ref[1:2, :], (SUBLANES, D_STATE))
    hr = h_ref[:, 0:D_STATE]
    hi = h_ref[:, D_STATE:2 * D_STATE]
    for t in range(tt):
        r = slice(t * SUBLANES, (t + 1) * SUBLANES)
        hr, hi = (lr * hr - li * hi + bu[r, 0:D_STATE],
                  lr * hi + li * hr + bu[r, D_STATE:2 * D_STATE])
        st_ref[r, 0:D_STATE] = hr
        st_ref[r, D_STATE:2 * D_STATE] = hi
    y = _dot(st_ref[...].astype(BF16), cblk_ref[...]) + d_ref[...] * ut
    z = _gelu_tanh(y)
    gate = jax.nn.sigmoid(_dot(z.astype(BF16), wglu_ref[...]))
    h_ref[:, 0:D_STATE] = hr
    h_ref[:, D_STATE:2 * D_STATE] = hi
    hn_ref[:, 0:D_STATE] = hr
    hn_ref[:, D_STATE:2 * D_STATE] = hi
    out = (z * gate).reshape(tt, nb, D_SSM)
    ssm_ref[...] = jnp.swapaxes(out, 0, 1).astype(BF16)


def _in_ssm_call(x, ln_in, layer, init_layer, w_in, scw, scb, sc0, h0, lbar, bblk, cblk, d, wglu,
                 kt_all, vt_all, tt):
    apply_ln, g_in, b_in = ln_in
    bsz, s, _ = x.shape
    keep = kt_all.shape[2]
    nb = SUBLANES
    nt = s // tt
    tail_first = nt - keep // tt
    blk = lambda b, t: (b, t, 0)
    tail = lambda b, t: (layer, b, jnp.maximum(t - tail_first, 0), 0)
    bonly3 = lambda b, t: (b, 0, 0)
    bonly2 = lambda b, t: (b, 0)
    const2 = lambda b, t: (0, 0)
    init3 = lambda b, t: (init_layer, b, 0, 0)
    init2 = lambda b, t: (init_layer, b, 0)
    operands = (x, g_in, b_in, w_in, scw, scb, sc0, h0, lbar, bblk, cblk, d, wglu, kt_all, vt_all)
    return pl.pallas_call(
        functools.partial(_in_ssm_kernel, tt=tt, ln_in=apply_ln),
        grid=(bsz // nb, nt),
        in_specs=[pl.BlockSpec((nb, tt, D_MODEL), blk),
                  pl.BlockSpec((1, D_MODEL), const2),
                  pl.BlockSpec((1, D_MODEL), const2),
                  _layer_spec((D_MODEL, D_QKV + D_REST), layer),
                  _layer_spec((CONV_K, D_SCONV), layer),
                  _layer_spec((1, D_SCONV), layer),
                  pl.BlockSpec((None, nb, SUBLANES, D_SCONV), init3),
                  pl.BlockSpec((None, nb, 2 * D_STATE), init2),
                  _layer_spec((2, D_STATE), layer),
                  _layer_spec((D_SSM, 2 * D_STATE), layer),
                  _layer_spec((2 * D_STATE, D_SSM), layer),
                  _layer_spec((1, D_SSM), layer),
                  _layer_spec((D_SSM, D_SSM), layer),
                  pl.BlockSpec(memory_space=pl.ANY),
                  pl.BlockSpec(memory_space=pl.ANY)],
        out_specs=[pl.BlockSpec((nb, tt, D_QKV), blk),
                   pl.BlockSpec((None, nb, tt, D_ATT), tail),
                   pl.BlockSpec((None, nb, tt, D_ATT), tail),
                   pl.BlockSpec((nb, tt, D_SSM), blk),
                   pl.BlockSpec((nb, tt, D_SCONV), blk),
                   pl.BlockSpec((nb, SUBLANES, D_SCONV), bonly3),
                   pl.BlockSpec((nb, 2 * D_STATE), bonly2)],
        out_shape=[jax.ShapeDtypeStruct((bsz, s, D_QKV), BF16),
                   jax.ShapeDtypeStruct(kt_all.shape, F32),
                   jax.ShapeDtypeStruct(vt_all.shape, F32),
                   jax.ShapeDtypeStruct((bsz, s, D_SSM), BF16),
                   jax.ShapeDtypeStruct((bsz, s, D_SCONV), BF16),
                   jax.ShapeDtypeStruct((bsz, SUBLANES, D_SCONV), F32),
                   jax.ShapeDtypeStruct((bsz, 2 * D_STATE), F32)],
        scratch_shapes=[pltpu.VMEM((nb, SUBLANES, D_SCONV), F32),
                        pltpu.VMEM((nb, 2 * D_STATE), F32),
                        pltpu.VMEM((nb * tt, 2 * D_STATE), F32)],
        input_output_aliases={len(operands) - 2: 1, len(operands) - 1: 2},
        compiler_params=_cparams(("parallel", "arbitrary")),
        name="in_ssm",
    )(*operands)


HEADS_PER_GROUP = 4
GROUP_W = HEADS_PER_GROUP * HEAD_DIM
N_GROUPS = N_HEADS // HEADS_PER_GROUP
Q_STEP = 4 * CHUNK
K_WIN = PAST + Q_STEP


def _softmax_unnormalised(sc):
    m = jnp.max(sc, axis=-1, keepdims=True)
    e = jnp.exp(sc - m)
    return e.astype(BF16), jnp.sum(e, axis=-1, keepdims=True)


def _group_head_of_lane(shape):
    lane = lax.broadcasted_iota(jnp.int32, shape, len(shape) - 1)
    return (lane // HEAD_DIM) % HEADS_PER_GROUP


def _softmax_banded(sc, bias):
    half, skip = Q_STEP // 2, 2 * CHUNK
    zeros = jnp.zeros((half, skip), BF16)
    e_top, l_top = _softmax_unnormalised(sc[0:half, 0:K_WIN - skip] + bias[0:half, 0:K_WIN - skip])
    e_bot, l_bot = _softmax_unnormalised(sc[half:Q_STEP, skip:K_WIN] + bias[half:Q_STEP, skip:K_WIN])
    e = jnp.concatenate([jnp.concatenate([e_top, zeros], axis=1),
                         jnp.concatenate([zeros, e_bot], axis=1)], axis=0)
    return e, jnp.concatenate([l_top, l_bot], axis=0)


def _attn_group(q4, k4, v4, bias_of_head):
    head = _group_head_of_lane(q4.shape)
    full_window = q4.shape[0] == Q_STEP and k4.shape[0] == K_WIN
    out = None
    scale = None
    for h in range(HEADS_PER_GROUP):
        qm = jnp.where(head == h, q4, jnp.zeros_like(q4))
        sc = lax.dot_general(qm, k4, (((1,), (1,)), ((), ())), preferred_element_type=F32)
        if full_window:
            e, l = _softmax_banded(sc, bias_of_head(h))
        else:
            e, l = _softmax_unnormalised(sc + bias_of_head(h))
        pv = _dot(e, v4)
        inv = 1.0 / l
        out = pv if out is None else jnp.where(head == h, pv, out)
        scale = inv if scale is None else jnp.where(head == h, inv, scale)
    return out * scale


def _attn_prompt_kernel(qkv_ref, bias_ref, o_ref, *, s):
    def step(r0, k0, nk):
        for g in range(N_GROUPS):
            lo = g * GROUP_W
            out = _attn_group(
                qkv_ref[pl.ds(r0, Q_STEP), lo:lo + GROUP_W],
                qkv_ref[pl.ds(k0, nk), D_ATT + lo:D_ATT + lo + GROUP_W],
                qkv_ref[pl.ds(k0, nk), 2 * D_ATT + lo:2 * D_ATT + lo + GROUP_W],
                lambda h: bias_ref[g * HEADS_PER_GROUP + h, :, K_WIN - nk:K_WIN])
            o_ref[pl.ds(r0, Q_STEP), lo:lo + GROUP_W] = out.astype(BF16)

    n_steps = s // Q_STEP
    n_head_steps = min(PAST // Q_STEP, n_steps)
    for j in range(n_head_steps):
        step(j * Q_STEP, 0, (j + 1) * Q_STEP)

    def body(j, carry):
        r0 = pl.multiple_of(j * Q_STEP, Q_STEP)
        step(r0, pl.multiple_of(r0 - PAST, Q_STEP), K_WIN)
        return carry

    if n_steps > n_head_steps:
        lax.fori_loop(n_head_steps, n_steps, body, 0)


def _attn_prompt_call(qkv, layer, bias):
    bsz, s, _ = qkv.shape
    assert s % Q_STEP == 0
    bsel = lambda b: (b, 0, 0)
    return pl.pallas_call(
        functools.partial(_attn_prompt_kernel, s=s),
        grid=(bsz,),
        in_specs=[pl.BlockSpec((None, s, D_QKV), bsel),
                  _layer_spec((N_HEADS, Q_STEP, K_WIN), layer)],
        out_specs=pl.BlockSpec((None, s, D_ATT), bsel),
        out_shape=jax.ShapeDtypeStruct((bsz, s, D_ATT), BF16),
        compiler_params=_cparams(("parallel",)),
        name="attention_prompt",
    )(qkv, bias)


def _attn_sample_kernel(qkv_ref, pkt_ref, pvt_ref, bias_ref, o_ref, *, s):
    contract_last = (((1,), (1,)), ((), ()))
    for g in range(N_GROUPS):
        cols = slice(g * GROUP_W, (g + 1) * GROUP_W)
        q4 = qkv_ref[:, cols]
        k_new = qkv_ref[:, D_ATT + g * GROUP_W:D_ATT + (g + 1) * GROUP_W]
        v_new = qkv_ref[:, 2 * D_ATT + g * GROUP_W:2 * D_ATT + (g + 1) * GROUP_W]
        kt_past = pkt_ref[cols, :].astype(BF16)
        vt_past = pvt_ref[cols, :].astype(BF16)
        head = _group_head_of_lane(q4.shape)
        out = None
        scale = None
        for h in range(HEADS_PER_GROUP):
            qm = jnp.where(head == h, q4, jnp.zeros_like(q4))
            bias = bias_ref[g * HEADS_PER_GROUP + h]
            sc_past = _dot(qm, kt_past) + bias[:, 0:PAST]
            sc_new = lax.dot_general(qm, k_new, contract_last, preferred_element_type=F32) + bias[:, PAST:]
            m = jnp.maximum(jnp.max(sc_past, axis=-1, keepdims=True), jnp.max(sc_new, axis=-1, keepdims=True))
            e_past = jnp.exp(sc_past - m)
            e_new = jnp.exp(sc_new - m)
            l = jnp.sum(e_past, axis=-1, keepdims=True) + jnp.sum(e_new, axis=-1, keepdims=True)
            pv = (lax.dot_general(e_past.astype(BF16), vt_past, contract_last, preferred_element_type=F32)
                  + _dot(e_new.astype(BF16), v_new))
            inv = 1.0 / l
            out = pv if out is None else jnp.where(head == h, pv, out)
            scale = inv if scale is None else jnp.where(head == h, inv, scale)
        o_ref[:, cols] = (out * scale).astype(BF16)


def _attn_sample_call(qkv, layer, pkt, pvt, bias):
    bsz, s, _ = qkv.shape
    band = PAST + s
    bsel = lambda b: (b, 0, 0)
    return pl.pallas_call(
        functools.partial(_attn_sample_kernel, s=s),
        grid=(bsz,),
        in_specs=[pl.BlockSpec((None, s, D_QKV), bsel),
                  pl.BlockSpec((None, None, D_ATT, PAST), lambda b: (layer, b, 0, 0)),
                  pl.BlockSpec((None, None, D_ATT, PAST), lambda b: (layer, b, 0, 0)),
                  _layer_spec((N_HEADS, s, band), layer)],
        out_specs=pl.BlockSpec((None, s, D_ATT), bsel),
        out_shape=jax.ShapeDtypeStruct((bsz, s, D_ATT), BF16),
        compiler_params=_cparams(("parallel",)),
        name="attention_sample",
    )(qkv, pkt, pvt, bias)


def _mix_ffn_kernel(att_ref, ssm_ref, sc_ref, x_ref, gin_ref, bin_ref, wo_ref, g1_ref, b1_ref,
                    w1_ref, cw_ref, cb_ref, c0_ref, w2_ref, g2_ref, b2_ref,
                    o_ref, cn_ref, hist_ref, *, nb, tt, ln_in):
    ti = pl.program_id(1)
    rows = nb * tt

    @pl.when(ti == 0)
    def _():
        hist_ref[...] = c0_ref[...]

    x = x_ref[...].reshape(rows, D_MODEL)
    if ln_in:
        x = _layer_norm(x, gin_ref[...], bin_ref[...])
    mix = _dot(att_ref[...].reshape(rows, D_ATT), wo_ref[0:D_ATT, :])
    mix = mix + _dot(ssm_ref[...].reshape(rows, D_SSM), wo_ref[D_ATT:D_ATT + D_SSM, :])
    mix = mix + _dot(sc_ref[...].reshape(rows, D_SCONV), wo_ref[D_ATT + D_SSM:D_MODEL, :])
    x1 = _layer_norm(ALPHA * x + mix, g1_ref[...], b1_ref[...])

    x1b = x1.astype(BF16)
    halves = []
    for cols in (slice(0, D_FF), slice(D_FF, 2 * D_FF)):
        up = _dot(x1b, w1_ref[:, cols])
        halves.append(_causal_conv3(up, hist_ref.at[:, :, cols], cw_ref.at[:, cols], cb_ref.at[:, cols], nb, tt))
    gate, val = halves
    cn_ref[...] = hist_ref[...]
    h = (gate * jax.nn.sigmoid(gate) * val).astype(BF16)
    y = _dot(h, w2_ref[...])
    o_ref[...] = _layer_norm(ALPHA * x1 + y, g2_ref[...], b2_ref[...]).reshape(nb, tt, D_MODEL)


def _mix_ffn_call(att, ssm, sc, x, ln_in, layer, init_layer, w_out, g1, b1, w1, cw, cb, c0, w2, g2, b2,
                  nb, tt):
    apply_ln, g_in, b_in = ln_in
    bsz, s, _ = x.shape
    blk = lambda b, t: (b, t, 0)
    bonly = lambda b, t: (b, 0, 0)
    const2 = lambda b, t: (0, 0)
    vec = pl.BlockSpec((1, D_MODEL), const2)
    lvec = _layer_spec((1, D_MODEL), layer)
    return pl.pallas_call(
        functools.partial(_mix_ffn_kernel, nb=nb, tt=tt, ln_in=apply_ln),
        grid=(bsz // nb, s // tt),
        in_specs=[pl.BlockSpec((nb, tt, D_ATT), blk),
                  pl.BlockSpec((nb, tt, D_SSM), blk),
                  pl.BlockSpec((nb, tt, D_SCONV), blk),
                  pl.BlockSpec((nb, tt, D_MODEL), blk), vec, vec,
                  _layer_spec((D_MODEL, D_MODEL), layer), lvec, lvec,
                  _layer_spec((D_MODEL, 2 * D_FF), layer),
                  _layer_spec((CONV_K, 2 * D_FF), layer),
                  _layer_spec((1, 2 * D_FF), layer),
                  pl.BlockSpec((None, nb, SUBLANES, 2 * D_FF), lambda b, t: (init_layer, b, 0, 0)),
                  _layer_spec((D_FF, D_MODEL), layer), lvec, lvec],
        out_specs=[pl.BlockSpec((nb, tt, D_MODEL), blk),
                   pl.BlockSpec((nb, SUBLANES, 2 * D_FF), bonly)],
        out_shape=[jax.ShapeDtypeStruct((bsz, s, D_MODEL), F32),
                   jax.ShapeDtypeStruct((bsz, SUBLANES, 2 * D_FF), F32)],
        scratch_shapes=[pltpu.VMEM((nb, SUBLANES, 2 * D_FF), F32)],
        compiler_params=_cparams(("parallel", "arbitrary")),
        name="mix_ffn",
    )(att, ssm, sc, x, g_in, b_in, w_out, g1, b1, w1, cw, cb, c0, w2, g2, b2)


def _rel_bias_table(rel_bias, tq, chunk):
    depth, n_heads, _ = rel_bias.shape
    nk = PAST + tq
    period = nk + tq
    dist = (nk - 1) - jnp.arange(period)
    by_lag = rel_bias[:, :, jnp.clip(dist, -(CHUNK - 1), REL_MAX) + (CHUNK - 1)].astype(F32)
    rolled = jnp.roll(by_lag, -(tq - 1), axis=-1)
    bias = jnp.tile(rolled, (1, 1, tq))[:, :, :tq * (period - 1)]
    bias = bias.reshape(depth, n_heads, tq, period - 1)[:, :, :, :nk]
    if tq == chunk:
        return bias
    t_idx = np.arange(tq)[:, None]
    s_idx = np.arange(nk)[None, :]
    key_chunk = s_idx // chunk - t_idx // chunk
    in_band = (key_chunk >= 0) & (key_chunk <= N_PAST_CHUNKS)
    return jnp.where(in_band[None, None], bias, -jnp.inf)


def _ssm_params(lam_re, lam_im, log_dt, b_re, b_im, c_re, c_im):
    depth = lam_re.shape[0]
    lam = lax.complex(lam_re.astype(F32), lam_im.astype(F32))
    dt = jnp.exp(log_dt.astype(F32))[:, :, None]
    lbar = jnp.exp(lam * dt)
    bbar = ((lbar - 1.0) / lam)[..., None] * lax.complex(b_re.astype(F32), b_im.astype(F32))
    eye = jnp.eye(N_SSM_GROUPS, dtype=F32)

    def in_map(m):
        return jnp.einsum('lgpc,gh->lgchp', m, eye).reshape(depth, D_SSM, D_STATE)

    def out_map(m):
        return jnp.einsum('lgcp,gh->lgphc', m, eye).reshape(depth, D_STATE, D_SSM)

    bblk = jnp.concatenate([in_map(bbar.real), in_map(bbar.imag)], axis=2).astype(BF16)
    cblk = jnp.concatenate([out_map(c_re.astype(F32)), out_map(-c_im.astype(F32))], axis=1).astype(BF16)
    lbar2 = jnp.stack([lbar.real.reshape(depth, D_STATE), lbar.imag.reshape(depth, D_STATE)], axis=1)
    return lbar2, bblk, cblk


def _pad_rows(buf):
    return jnp.pad(buf.astype(F32), ((0, 0), (0, 0), (SUBLANES - (CONV_K - 1), 0), (0, 0)))


def _trunk_layer(x, ln_in, layer, lp, group, kv_tails, tiles):
    bsz, s, _ = x.shape
    tt_in, nb_ffn, tt_ffn = tiles
    init_layer = group['init_layer'](layer)

    qkv, kt_all, vt_all, ssm_out, sconv_out, sc_new, h_new = _in_ssm_call(
        x, ln_in, layer, init_layer, lp['w_in'], lp['sconv_w'], lp['sconv_b'], group['sc0'], group['h0'],
        lp['lbar'], lp['bblk'], lp['cblk'], lp['d'], lp['w_glu'], *kv_tails, tt_in)
    if 'pk' in group:
        att = _attn_sample_call(qkv, layer, group['pk'], group['pv'], group['bias'])
    else:
        att = _attn_prompt_call(qkv, layer, group['bias'])
    x2, ff_new = _mix_ffn_call(att, ssm_out, sconv_out, x, ln_in, layer, init_layer,
                               lp['w_out'], lp['ln1_g'], lp['ln1_b'],
                               lp['w_ff_in'], lp['ffn_conv_w'], lp['ffn_conv_b'], group['ff0'],
                               lp['w_ff_out'], lp['ln2_g'], lp['ln2_b'], nb_ffn, tt_ffn)
    new = (h_new[:, 0:D_STATE].reshape(bsz, N_SSM_GROUPS, SSM_STATE),
           h_new[:, D_STATE:].reshape(bsz, N_SSM_GROUPS, SSM_STATE),
           sc_new[:, SUBLANES - (CONV_K - 1):], ff_new[:, SUBLANES - (CONV_K - 1):])
    return x2, (kt_all, vt_all), new


ROW_TILE = 512
IN_SSM_POSITIONS = 128


def _tiles(bsz, s):
    tt_in = min(IN_SSM_POSITIONS, s)
    tt_ffn = min(ROW_TILE, s)
    nb_ffn = min(bsz, ROW_TILE // tt_ffn)
    return tt_in, nb_ffn, tt_ffn


def kernel(x_prompt, x_sample, cache_k, cache_v, state_ssm_re, state_ssm_im, cache_sconv, cache_ffn_conv, ln_in_g, ln_in_b, w_in, rel_bias, ssm_lam_re, ssm_lam_im, ssm_log_dt, ssm_b_re, ssm_b_im, ssm_c_re, ssm_c_im, ssm_d, w_glu, sconv_w, sconv_b, w_out, ln1_g, ln1_b, w_ff_in, ffn_conv_w, ffn_conv_b, w_ff_out, ln2_g, ln2_b):
    bp, sp, _ = x_prompt.shape
    bs, ss, _ = x_sample.shape
    g_in = ln_in_g.reshape(1, D_MODEL)
    b_in = ln_in_b.reshape(1, D_MODEL)
    xp, xs = x_prompt, x_sample
    to_feature_major = lambda c: jnp.transpose(c, (0, 1, 3, 4, 2)).reshape(DEPTH, bs, D_ATT, PAST)
    pk_all = to_feature_major(cache_k)
    pv_all = to_feature_major(cache_v)
    lbar, bblk, cblk = _ssm_params(ssm_lam_re, ssm_lam_im, ssm_log_dt, ssm_b_re, ssm_b_im, ssm_c_re, ssm_c_im)
    row = lambda a: a.reshape(DEPTH, 1, a.shape[-1])
    lp = {'w_in': w_in.astype(BF16), 'lbar': lbar, 'bblk': bblk, 'cblk': cblk,
          'd': row(ssm_d), 'w_glu': w_glu.astype(BF16),
          'sconv_w': sconv_w, 'sconv_b': row(sconv_b),
          'w_out': w_out.astype(BF16), 'ln1_g': row(ln1_g), 'ln1_b': row(ln1_b),
          'w_ff_in': w_ff_in.astype(BF16), 'ffn_conv_w': ffn_conv_w, 'ffn_conv_b': row(ffn_conv_b),
          'w_ff_out': w_ff_out.astype(BF16), 'ln2_g': row(ln2_g), 'ln2_b': row(ln2_b)}
    prompt = {'init_layer': lambda l: 0,
              'h0': jnp.zeros((1, bp, 2 * D_STATE), F32),
              'sc0': jnp.zeros((1, bp, SUBLANES, D_SCONV), F32),
              'ff0': jnp.zeros((1, bp, SUBLANES, 2 * D_FF), F32),
              'bias': _rel_bias_table(rel_bias, Q_STEP, CHUNK)}
    sample = {'init_layer': lambda l: l,
              'h0': jnp.concatenate([state_ssm_re.reshape(DEPTH, bs, D_STATE),
                                     state_ssm_im.reshape(DEPTH, bs, D_STATE)], axis=2).astype(F32),
              'sc0': _pad_rows(cache_sconv), 'ff0': _pad_rows(cache_ffn_conv),
              'bias': _rel_bias_table(rel_bias, ss, ss), 'pk': pk_all, 'pv': pv_all}
    kv_p = tuple(jnp.zeros((DEPTH, bp, min(PAST, sp), D_ATT), F32) for _ in range(2))
    kv_s = tuple(jnp.zeros((DEPTH, bs, ss, D_ATT), F32) for _ in range(2))
    st_p = [[] for _ in range(4)]
    st_s = [[] for _ in range(4)]
    for l in range(DEPTH):
        ln_in = (l == 0, g_in, b_in)
        xp, kv_p, new_p = _trunk_layer(xp, ln_in, l, lp, prompt, kv_p, _tiles(bp, sp))
        xs, kv_s, new_s = _trunk_layer(xs, ln_in, l, lp, sample, kv_s, _tiles(bs, ss))
        for i in range(4):
            st_p[i].append(new_p[i])
            st_s[i].append(new_s[i])
    hre_p, him_p, sc_p, ff_p = [jnp.stack(a, axis=0) for a in st_p]
    hre_s, him_s, sc_s, ff_s = [jnp.stack(a, axis=0) for a in st_s]
    heads = lambda a: a.reshape(a.shape[:-1] + (N_HEADS, HEAD_DIM))
    return (xp, xs, heads(kv_p[0]), heads(kv_p[1]), heads(kv_s[0]), heads(kv_s[1]),
            hre_p, him_p, hre_s, him_s, sc_p, sc_s, ff_p, ff_s)
```

```python
import functools
import math

import jax
import jax.numpy as jnp
import numpy as np
from jax import lax
from jax.experimental import pallas as pl
from jax.experimental.pallas import tpu as pltpu

D_MODEL = 1024
DEPTH = 4
CHUNK = 64
N_HEADS = 8
HEAD_DIM = 64
D_ATT = N_HEADS * HEAD_DIM
N_PAST_CHUNKS = 8
PAST = N_PAST_CHUNKS * CHUNK
REL_MAX = 128
SSM_GROUP = 16
N_SSM_GROUPS = 16
D_SSM = SSM_GROUP * N_SSM_GROUPS
SSM_STATE = 64
D_STATE = N_SSM_GROUPS * SSM_STATE
D_SCONV = 256
CONV_K = 3
D_FF = 2048
D_QKV = 3 * D_ATT
D_REST = D_SSM + 3 * D_SCONV
ALPHA = (2 * DEPTH) ** 0.25
LN_EPS = 1e-5

SUBLANES = 8
VMEM_LIMIT = 56 * 1024 * 1024

F32 = jnp.float32
BF16 = jnp.bfloat16


def _cparams(sem):
    return pltpu.CompilerParams(dimension_semantics=sem, vmem_limit_bytes=VMEM_LIMIT)


def _layer_norm(x, g, b):
    mu = jnp.mean(x, axis=-1, keepdims=True)
    xc = x - mu
    var = jnp.mean(xc * xc, axis=-1, keepdims=True)
    return xc * lax.rsqrt(var + LN_EPS) * g + b


def _dot(a, b):
    return jnp.dot(a, b, preferred_element_type=F32)


def _layer_spec(shape, layer):
    return pl.BlockSpec((None,) + shape, lambda *_: (layer,) + (0,) * len(shape))


def _causal_conv3(cur, hist_ref, w_ref, b_ref, nb, tt):
    outs = []
    for s in range(nb):
        v = cur[s * tt:(s + 1) * tt]
        ext = jnp.concatenate([hist_ref[s], v], axis=0)
        m1 = pltpu.roll(ext, 1, 0)[SUBLANES:]
        m2 = pltpu.roll(ext, 2, 0)[SUBLANES:]
        hist_ref[s] = v[tt - SUBLANES:tt]
        outs.append(w_ref[0:1, :] * m2 + w_ref[1:2, :] * m1 + w_ref[2:3, :] * v + b_ref[...])
    return outs[0] if nb == 1 else jnp.concatenate(outs, axis=0)


def _gelu_tanh(x):
    c = math.sqrt(2.0 / math.pi)
    return 0.5 * x * (1.0 + jnp.tanh(c * (x + 0.044715 * (x * x * x))))


def _in_ssm_kernel(x_ref, gin_ref, bin_ref, w_ref, scw_ref, scb_ref, sc0_ref, h0_ref,
                   lbar_ref, bblk_ref, cblk_ref, d_ref, wglu_ref, kt_all_ref, vt_all_ref,
                   qkv_ref, kt_ref, vt_ref, ssm_ref, sc_ref, scn_ref, hn_ref,
                   hist_ref, h_ref, st_ref, *, tt, ln_in):
    del kt_all_ref, vt_all_ref
    ti = pl.program_id(1)
    nb = SUBLANES
    rows = nb * tt

    @pl.when(ti == 0)
    def _():
        hist_ref[...] = sc0_ref[...]
        h_ref[...] = h0_ref[...]

    x = x_ref[...].reshape(rows, D_MODEL)
    if ln_in:
        x = _layer_norm(x, gin_ref[...], bin_ref[...])
    xb = x.astype(BF16)
    u = _dot(xb, w_ref[:, D_QKV:D_QKV + D_SSM])
    ut = jnp.swapaxes(u.reshape(nb, tt, D_SSM), 0, 1).reshape(rows, D_SSM)
    bu = _dot(ut.astype(BF16), bblk_ref[...])

    q = _dot(xb, w_ref[:, 0:D_ATT]) * (HEAD_DIM ** -0.5)
    qkv_ref[:, :, 0:D_ATT] = q.astype(BF16).reshape(nb, tt, D_ATT)
    for tail_ref, lo in ((kt_ref, D_ATT), (vt_ref, 2 * D_ATT)):
        kv = _dot(xb, w_ref[:, lo:lo + D_ATT])
        qkv_ref[:, :, lo:lo + D_ATT] = kv.astype(BF16).reshape(nb, tt, D_ATT)
        tail_ref[...] = kv.reshape(nb, tt, D_ATT)

    g = _dot(xb, w_ref[:, D_QKV + D_SSM:D_QKV + D_REST])
    gate_b = g[:, 0:D_SCONV]
    gate_c = g[:, D_SCONV:2 * D_SCONV]
    xv = g[:, 2 * D_SCONV:3 * D_SCONV]
    conv = _causal_conv3(gate_c * xv, hist_ref, scw_ref, scb_ref, nb, tt)
    sc_ref[...] = (gate_b * conv).astype(BF16).reshape(nb, tt, D_SCONV)
    scn_ref[...] = hist_ref[...]

    lr = jnp.broadcast_to(lbar_ref[0:1, :], (SUBLANES, D_STATE))
    li = jnp.broadcast_to(lbar_ref[1:2, :], (SUBLANES, D_STATE))
    hr = h_ref[:, 0:D_STATE]
    hi = h_ref[:, D_STATE:2 * D_STATE]
    for t in range(tt):
        r = slice(t * SUBLANES, (t + 1) * SUBLANES)
        hr, hi = (lr * hr - li * hi + bu[r, 0:D_STATE],
                  lr * hi + li * hr + bu[r, D_STATE:2 * D_STATE])
        st_ref[r, 0:D_STATE] = hr
        st_ref[r, D_STATE:2 * D_STATE] = hi
    y = _dot(st_ref[...].astype(BF16), cblk_ref[...]) + d_ref[...] * ut
    z = _gelu_tanh(y)
    gate = jax.nn.sigmoid(_dot(z.astype(BF16), wglu_ref[...]))
    h_ref[:, 0:D_STATE] = hr
    h_ref[:, D_STATE:2 * D_STATE] = hi
    hn_ref[:, 0:D_STATE] = hr
    hn_ref[:, D_STATE:2 * D_STATE] = hi
    out = (z * gate).reshape(tt, nb, D_SSM)
    ssm_ref[...] = jnp.swapaxes(out, 0, 1).astype(BF16)


def _in_ssm_call(x, ln_in, layer, init_layer, w_in, scw, scb, sc0, h0, lbar, bblk, cblk, d, wglu,
                 kt_all, vt_all, tt):
    apply_ln, g_in, b_in = ln_in
    bsz, s, _ = x.shape
    keep = kt_all.shape[2]
    nb = SUBLANES
    nt = s // tt
    tail_first = nt - keep // tt
    blk = lambda b, t: (b, t, 0)
    tail = lambda b, t: (layer, b, jnp.maximum(t - tail_first, 0), 0)
    bonly3 = lambda b, t: (b, 0, 0)
    bonly2 = lambda b, t: (b, 0)
    const2 = lambda b, t: (0, 0)
    init3 = lambda b, t: (init_layer, b, 0, 0)
    init2 = lambda b, t: (init_layer, b, 0)
    operands = (x, g_in, b_in, w_in, scw, scb, sc0, h0, lbar, bblk, cblk, d, wglu, kt_all, vt_all)
    return pl.pallas_call(
        functools.partial(_in_ssm_kernel, tt=tt, ln_in=apply_ln),
        grid=(bsz // nb, nt),
        in_specs=[pl.BlockSpec((nb, tt, D_MODEL), blk),
                  pl.BlockSpec((1, D_MODEL), const2),
                  pl.BlockSpec((1, D_MODEL), const2),
                  _layer_spec((D_MODEL, D_QKV + D_REST), layer),
                  _layer_spec((CONV_K, D_SCONV), layer),
                  _layer_spec((1, D_SCONV), layer),
                  pl.BlockSpec((None, nb, SUBLANES, D_SCONV), init3),
                  pl.BlockSpec((None, nb, 2 * D_STATE), init2),
                  _layer_spec((2, D_STATE), layer),
                  _layer_spec((D_SSM, 2 * D_STATE), layer),
                  _layer_spec((2 * D_STATE, D_SSM), layer),
                  _layer_spec((1, D_SSM), layer),
                  _layer_spec((D_SSM, D_SSM), layer),
                  pl.BlockSpec(memory_space=pl.ANY),
                  pl.BlockSpec(memory_space=pl.ANY)],
        out_specs=[pl.BlockSpec((nb, tt, D_QKV), blk),
                   pl.BlockSpec((None, nb, tt, D_ATT), tail),
                   pl.BlockSpec((None, nb, tt, D_ATT), tail),
                   pl.BlockSpec((nb, tt, D_SSM), blk),
                   pl.BlockSpec((nb, tt, D_SCONV), blk),
                   pl.BlockSpec((nb, SUBLANES, D_SCONV), bonly3),
                   pl.BlockSpec((nb, 2 * D_STATE), bonly2)],
        out_shape=[jax.ShapeDtypeStruct((bsz, s, D_QKV), BF16),
                   jax.ShapeDtypeStruct(kt_all.shape, F32),
                   jax.ShapeDtypeStruct(vt_all.shape, F32),
                   jax.ShapeDtypeStruct((bsz, s, D_SSM), BF16),
                   jax.ShapeDtypeStruct((bsz, s, D_SCONV), BF16),
                   jax.ShapeDtypeStruct((bsz, SUBLANES, D_SCONV), F32),
                   jax.ShapeDtypeStruct((bsz, 2 * D_STATE), F32)],
        scratch_shapes=[pltpu.VMEM((nb, SUBLANES, D_SCONV), F32),
                        pltpu.VMEM((nb, 2 * D_STATE), F32),
                        pltpu.VMEM((nb * tt, 2 * D_STATE), F32)],
        input_output_aliases={len(operands) - 2: 1, len(operands) - 1: 2},
        compiler_params=_cparams(("parallel", "arbitrary")),
        name="in_ssm",
    )(*operands)


HEADS_PER_GROUP = 4
GROUP_W = HEADS_PER_GROUP * HEAD_DIM
N_GROUPS = N_HEADS // HEADS_PER_GROUP
Q_STEP = 4 * CHUNK
K_WIN = PAST + Q_STEP


def _softmax_unnormalised(sc):
    m = jnp.max(sc, axis=-1, keepdims=True)
    e = jnp.exp(sc - m)
    return e.astype(BF16), jnp.sum(e, axis=-1, keepdims=True)


def _group_head_of_lane(shape):
    lane = lax.broadcasted_iota(jnp.int32, shape, len(shape) - 1)
    return (lane // HEAD_DIM) % HEADS_PER_GROUP


def _softmax_banded(sc, bias):
    half, skip = Q_STEP // 2, 2 * CHUNK
    zeros = jnp.zeros((half, skip), BF16)
    e_top, l_top = _softmax_unnormalised(sc[0:half, 0:K_WIN - skip] + bias[0:half, 0:K_WIN - skip])
    e_bot, l_bot = _softmax_unnormalised(sc[half:Q_STEP, skip:K_WIN] + bias[half:Q_STEP, skip:K_WIN])
    e = jnp.concatenate([jnp.concatenate([e_top, zeros], axis=1),
                         jnp.concatenate([zeros, e_bot], axis=1)], axis=0)
    return e, jnp.concatenate([l_top, l_bot], axis=0)


def _attn_group(q4, k4, v4, bias_of_head):
    head = _group_head_of_lane(q4.shape)
    full_window = q4.shape[0] == Q_STEP and k4.shape[0] == K_WIN
    out = None
    scale = None
    for h in range(HEADS_PER_GROUP):
        qm = jnp.where(head == h, q4, jnp.zeros_like(q4))
        sc = lax.dot_general(qm, k4, (((1,), (1,)), ((), ())), preferred_element_type=F32)
        if full_window:
            e, l = _softmax_banded(sc, bias_of_head(h))
        else:
            e, l = _softmax_unnormalised(sc + bias_of_head(h))
        pv = _dot(e, v4)
        inv = 1.0 / l
        out = pv if out is None else jnp.where(head == h, pv, out)
        scale = inv if scale is None else jnp.where(head == h, inv, scale)
    return out * scale


def _attn_prompt_kernel(qkv_ref, bias_ref, o_ref, *, s):
    def step(r0, k0, nk):
        for g in range(N_GROUPS):
            lo = g * GROUP_W
            out = _attn_group(
                qkv_ref[pl.ds(r0, Q_STEP), lo:lo + GROUP_W],
                qkv_ref[pl.ds(k0, nk), D_ATT + lo:D_ATT + lo + GROUP_W],
                qkv_ref[pl.ds(k0, nk), 2 * D_ATT + lo:2 * D_ATT + lo + GROUP_W],
                lambda h: bias_ref[g * HEADS_PER_GROUP + h, :, K_WIN - nk:K_WIN])
            o_ref[pl.ds(r0, Q_STEP), lo:lo + GROUP_W] = out.astype(BF16)

    n_steps = s // Q_STEP
    n_head_steps = min(PAST // Q_STEP, n_steps)
    for j in range(n_head_steps):
        step(j * Q_STEP, 0, (j + 1) * Q_STEP)

    def body(j, carry):
        r0 = pl.multiple_of(j * Q_STEP, Q_STEP)
        step(r0, pl.multiple_of(r0 - PAST, Q_STEP), K_WIN)
        return carry

    if n_steps > n_head_steps:
        lax.fori_loop(n_head_steps, n_steps, body, 0, unroll=2)


def _attn_prompt_call(qkv, layer, bias):
    bsz, s, _ = qkv.shape
    assert s % Q_STEP == 0
    bsel = lambda b: (b, 0, 0)
    return pl.pallas_call(
        functools.partial(_attn_prompt_kernel, s=s),
        grid=(bsz,),
        in_specs=[pl.BlockSpec((None, s, D_QKV), bsel),
                  _layer_spec((N_HEADS, Q_STEP, K_WIN), layer)],
        out_specs=pl.BlockSpec((None, s, D_ATT), bsel),
        out_shape=jax.ShapeDtypeStruct((bsz, s, D_ATT), BF16),
        compiler_params=_cparams(("parallel",)),
        name="attention_prompt",
    )(qkv, bias)


def _attn_sample_kernel(qkv_ref, pkt_ref, pvt_ref, bias_ref, o_ref, *, s):
    contract_last = (((1,), (1,)), ((), ()))
    for g in range(N_GROUPS):
        cols = slice(g * GROUP_W, (g + 1) * GROUP_W)
        q4 = qkv_ref[:, cols]
        k_new = qkv_ref[:, D_ATT + g * GROUP_W:D_ATT + (g + 1) * GROUP_W]
        v_new = qkv_ref[:, 2 * D_ATT + g * GROUP_W:2 * D_ATT + (g + 1) * GROUP_W]
        kt_past = pkt_ref[cols, :].astype(BF16)
        vt_past = pvt_ref[cols, :].astype(BF16)
        head = _group_head_of_lane(q4.shape)
        out = None
        scale = None
        for h in range(HEADS_PER_GROUP):
            qm = jnp.where(head == h, q4, jnp.zeros_like(q4))
            bias = bias_ref[g * HEADS_PER_GROUP + h]
            sc_past = _dot(qm, kt_past) + bias[:, 0:PAST]
            sc_new = lax.dot_general(qm, k_new, contract_last, preferred_element_type=F32) + bias[:, PAST:]
            m = jnp.maximum(jnp.max(sc_past, axis=-1, keepdims=True), jnp.max(sc_new, axis=-1, keepdims=True))
            e_past = jnp.exp(sc_past - m)
            e_new = jnp.exp(sc_new - m)
            l = jnp.sum(e_past, axis=-1, keepdims=True) + jnp.sum(e_new, axis=-1, keepdims=True)
            pv = (lax.dot_general(e_past.astype(BF16), vt_past, contract_last, preferred_element_type=F32)
                  + _dot(e_new.astype(BF16), v_new))
            inv = 1.0 / l
            out = pv if out is None else jnp.where(head == h, pv, out)
            scale = inv if scale is None else jnp.where(head == h, inv, scale)
        o_ref[:, cols] = (out * scale).astype(BF16)


def _attn_sample_call(qkv, layer, pkt, pvt, bias):
    bsz, s, _ = qkv.shape
    band = PAST + s
    bsel = lambda b: (b, 0, 0)
    return pl.pallas_call(
        functools.partial(_attn_sample_kernel, s=s),
        grid=(bsz,),
        in_specs=[pl.BlockSpec((None, s, D_QKV), bsel),
                  pl.BlockSpec((None, None, D_ATT, PAST), lambda b: (layer, b, 0, 0)),
                  pl.BlockSpec((None, None, D_ATT, PAST), lambda b: (layer, b, 0, 0)),
                  _layer_spec((N_HEADS, s, band), layer)],
        out_specs=pl.BlockSpec((None, s, D_ATT), bsel),
        out_shape=jax.ShapeDtypeStruct((bsz, s, D_ATT), BF16),
        compiler_params=_cparams(("parallel",)),
        name="attention_sample",
    )(qkv, pkt, pvt, bias)


def _mix_ffn_kernel(att_ref, ssm_ref, sc_ref, x_ref, gin_ref, bin_ref, wo_ref, g1_ref, b1_ref,
                    w1_ref, cw_ref, cb_ref, c0_ref, w2_ref, g2_ref, b2_ref,
                    o_ref, cn_ref, hist_ref, *, nb, tt, ln_in):
    ti = pl.program_id(1)
    rows = nb * tt

    @pl.when(ti == 0)
    def _():
        hist_ref[...] = c0_ref[...]

    x = x_ref[...].reshape(rows, D_MODEL)
    if ln_in:
        x = _layer_norm(x, gin_ref[...], bin_ref[...])
    mix = _dot(att_ref[...].reshape(rows, D_ATT), wo_ref[0:D_ATT, :])
    mix = mix + _dot(ssm_ref[...].reshape(rows, D_SSM), wo_ref[D_ATT:D_ATT + D_SSM, :])
    mix = mix + _dot(sc_ref[...].reshape(rows, D_SCONV), wo_ref[D_ATT + D_SSM:D_MODEL, :])
    x1 = _layer_norm(ALPHA * x + mix, g1_ref[...], b1_ref[...])

    x1b = x1.astype(BF16)
    halves = []
    for cols in (slice(0, D_FF), slice(D_FF, 2 * D_FF)):
        up = _dot(x1b, w1_ref[:, cols])
        halves.append(_causal_conv3(up, hist_ref.at[:, :, cols], cw_ref.at[:, cols], cb_ref.at[:, cols], nb, tt))
    gate, val = halves
    cn_ref[...] = hist_ref[...]
    h = (gate * jax.nn.sigmoid(gate) * val).astype(BF16)
    y = _dot(h, w2_ref[...])
    o_ref[...] = _layer_norm(ALPHA * x1 + y, g2_ref[...], b2_ref[...]).reshape(nb, tt, D_MODEL)


def _mix_ffn_call(att, ssm, sc, x, ln_in, layer, init_layer, w_out, g1, b1, w1, cw, cb, c0, w2, g2, b2,
                  nb, tt):
    apply_ln, g_in, b_in = ln_in
    bsz, s, _ = x.shape
    blk = lambda b, t: (b, t, 0)
    bonly = lambda b, t: (b, 0, 0)
    const2 = lambda b, t: (0, 0)
    vec = pl.BlockSpec((1, D_MODEL), const2)
    lvec = _layer_spec((1, D_MODEL), layer)
    return pl.pallas_call(
        functools.partial(_mix_ffn_kernel, nb=nb, tt=tt, ln_in=apply_ln),
        grid=(bsz // nb, s // tt),
        in_specs=[pl.BlockSpec((nb, tt, D_ATT), blk),
                  pl.BlockSpec((nb, tt, D_SSM), blk),
                  pl.BlockSpec((nb, tt, D_SCONV), blk),
                  pl.BlockSpec((nb, tt, D_MODEL), blk), vec, vec,
                  _layer_spec((D_MODEL, D_MODEL), layer), lvec, lvec,
                  _layer_spec((D_MODEL, 2 * D_FF), layer),
                  _layer_spec((CONV_K, 2 * D_FF), layer),
                  _layer_spec((1, 2 * D_FF), layer),
                  pl.BlockSpec((None, nb, SUBLANES, 2 * D_FF), lambda b, t: (init_layer, b, 0, 0)),
                  _layer_spec((D_FF, D_MODEL), layer), lvec, lvec],
        out_specs=[pl.BlockSpec((nb, tt, D_MODEL), blk),
                   pl.BlockSpec((nb, SUBLANES, 2 * D_FF), bonly)],
        out_shape=[jax.ShapeDtypeStruct((bsz, s, D_MODEL), F32),
                   jax.ShapeDtypeStruct((bsz, SUBLANES, 2 * D_FF), F32)],
        scratch_shapes=[pltpu.VMEM((nb, SUBLANES, 2 * D_FF), F32)],
        compiler_params=_cparams(("parallel", "arbitrary")),
        name="mix_ffn",
    )(att, ssm, sc, x, g_in, b_in, w_out, g1, b1, w1, cw, cb, c0, w2, g2, b2)


def _rel_bias_table(rel_bias, tq, chunk):
    depth, n_heads, _ = rel_bias.shape
    nk = PAST + tq
    period = nk + tq
    dist = (nk - 1) - jnp.arange(period)
    by_lag = rel_bias[:, :, jnp.clip(dist, -(CHUNK - 1), REL_MAX) + (CHUNK - 1)].astype(F32)
    rolled = jnp.roll(by_lag, -(tq - 1), axis=-1)
    bias = jnp.tile(rolled, (1, 1, tq))[:, :, :tq * (period - 1)]
    bias = bias.reshape(depth, n_heads, tq, period - 1)[:, :, :, :nk]
    if tq == chunk:
        return bias
    t_idx = np.arange(tq)[:, None]
    s_idx = np.arange(nk)[None, :]
    key_chunk = s_idx // chunk - t_idx // chunk
    in_band = (key_chunk >= 0) & (key_chunk <= N_PAST_CHUNKS)
    return jnp.where(in_band[None, None], bias, -jnp.inf)


def _ssm_params(lam_re, lam_im, log_dt, b_re, b_im, c_re, c_im):
    depth = lam_re.shape[0]
    lam = lax.complex(lam_re.astype(F32), lam_im.astype(F32))
    dt = jnp.exp(log_dt.astype(F32))[:, :, None]
    lbar = jnp.exp(lam * dt)
    bbar = ((lbar - 1.0) / lam)[..., None] * lax.complex(b_re.astype(F32), b_im.astype(F32))
    eye = jnp.eye(N_SSM_GROUPS, dtype=F32)

    def in_map(m):
        return jnp.einsum('lgpc,gh->lgchp', m, eye).reshape(depth, D_SSM, D_STATE)

    def out_map(m):
        return jnp.einsum('lgcp,gh->lgphc', m, eye).reshape(depth, D_STATE, D_SSM)

    bblk = jnp.concatenate([in_map(bbar.real), in_map(bbar.imag)], axis=2).astype(BF16)
    cblk = jnp.concatenate([out_map(c_re.astype(F32)), out_map(-c_im.astype(F32))], axis=1).astype(BF16)
    lbar2 = jnp.stack([lbar.real.reshape(depth, D_STATE), lbar.imag.reshape(depth, D_STATE)], axis=1)
    return lbar2, bblk, cblk


def _pad_rows(buf):
    return jnp.pad(buf.astype(F32), ((0, 0), (0, 0), (SUBLANES - (CONV_K - 1), 0), (0, 0)))


def _trunk_layer(x, ln_in, layer, lp, group, kv_tails, tiles):
    bsz, s, _ = x.shape
    tt_in, nb_ffn, tt_ffn = tiles
    init_layer = group['init_layer'](layer)

    qkv, kt_all, vt_all, ssm_out, sconv_out, sc_new, h_new = _in_ssm_call(
        x, ln_in, layer, init_layer, lp['w_in'], lp['sconv_w'], lp['sconv_b'], group['sc0'], group['h0'],
        lp['lbar'], lp['bblk'], lp['cblk'], lp['d'], lp['w_glu'], *kv_tails, tt_in)
    if 'pk' in group:
        att = _attn_sample_call(qkv, layer, group['pk'], group['pv'], group['bias'])
    else:
        att = _attn_prompt_call(qkv, layer, group['bias'])
    x2, ff_new = _mix_ffn_call(att, ssm_out, sconv_out, x, ln_in, layer, init_layer,
                               lp['w_out'], lp['ln1_g'], lp['ln1_b'],
                               lp['w_ff_in'], lp['ffn_conv_w'], lp['ffn_conv_b'], group['ff0'],
                               lp['w_ff_out'], lp['ln2_g'], lp['ln2_b'], nb_ffn, tt_ffn)
    new = (h_new[:, 0:D_STATE].reshape(bsz, N_SSM_GROUPS, SSM_STATE),
           h_new[:, D_STATE:].reshape(bsz, N_SSM_GROUPS, SSM_STATE),
           sc_new[:, SUBLANES - (CONV_K - 1):], ff_new[:, SUBLANES - (CONV_K - 1):])
    return x2, (kt_all, vt_all), new


ROW_TILE = 512
IN_SSM_POSITIONS = 128


def _tiles(bsz, s):
    tt_in = min(IN_SSM_POSITIONS, s)
    tt_ffn = min(ROW_TILE, s)
    nb_ffn = min(bsz, ROW_TILE // tt_ffn)
    return tt_in, nb_ffn, tt_ffn


def kernel(x_prompt, x_sample, cache_k, cache_v, state_ssm_re, state_ssm_im, cache_sconv, cache_ffn_conv, ln_in_g, ln_in_b, w_in, rel_bias, ssm_lam_re, ssm_lam_im, ssm_log_dt, ssm_b_re, ssm_b_im, ssm_c_re, ssm_c_im, ssm_d, w_glu, sconv_w, sconv_b, w_out, ln1_g, ln1_b, w_ff_in, ffn_conv_w, ffn_conv_b, w_ff_out, ln2_g, ln2_b):
    bp, sp, _ = x_prompt.shape
    bs, ss, _ = x_sample.shape
    g_in = ln_in_g.reshape(1, D_MODEL)
    b_in = ln_in_b.reshape(1, D_MODEL)
    xp, xs = x_prompt, x_sample
    to_feature_major = lambda c: jnp.transpose(c, (0, 1, 3, 4, 2)).reshape(DEPTH, bs, D_ATT, PAST)
    pk_all = to_feature_major(cache_k)
    pv_all = to_feature_major(cache_v)
    lbar, bblk, cblk = _ssm_params(ssm_lam_re, ssm_lam_im, ssm_log_dt, ssm_b_re, ssm_b_im, ssm_c_re, ssm_c_im)
    row = lambda a: a.reshape(DEPTH, 1, a.shape[-1])
    lp = {'w_in': w_in.astype(BF16), 'lbar': lbar, 'bblk': bblk, 'cblk': cblk,
          'd': row(ssm_d), 'w_glu': w_glu.astype(BF16),
          'sconv_w': sconv_w, 'sconv_b': row(sconv_b),
          'w_out': w_out.astype(BF16), 'ln1_g': row(ln1_g), 'ln1_b': row(ln1_b),
          'w_ff_in': w_ff_in.astype(BF16), 'ffn_conv_w': ffn_conv_w, 'ffn_conv_b': row(ffn_conv_b),
          'w_ff_out': w_ff_out.astype(BF16), 'ln2_g': row(ln2_g), 'ln2_b': row(ln2_b)}
    prompt = {'init_layer': lambda l: 0,
              'h0': jnp.zeros((1, bp, 2 * D_STATE), F32),
              'sc0': jnp.zeros((1, bp, SUBLANES, D_SCONV), F32),
              'ff0': jnp.zeros((1, bp, SUBLANES, 2 * D_FF), F32),
              'bias': _rel_bias_table(rel_bias, Q_STEP, CHUNK)}
    sample = {'init_layer': lambda l: l,
              'h0': jnp.concatenate([state_ssm_re.reshape(DEPTH, bs, D_STATE),
                                     state_ssm_im.reshape(DEPTH, bs, D_STATE)], axis=2).astype(F32),
              'sc0': _pad_rows(cache_sconv), 'ff0': _pad_rows(cache_ffn_conv),
              'bias': _rel_bias_table(rel_bias, ss, ss), 'pk': pk_all, 'pv': pv_all}
    kv_p = tuple(jnp.zeros((DEPTH, bp, min(PAST, sp), D_ATT), F32) for _ in range(2))
    kv_s = tuple(jnp.zeros((DEPTH, bs, ss, D_ATT), F32) for _ in range(2))
    st_p = [[] for _ in range(4)]
    st_s = [[] for _ in range(4)]
    for l in range(DEPTH):
        ln_in = (l == 0, g_in, b_in)
        xp, kv_p, new_p = _trunk_layer(xp, ln_in, l, lp, prompt, kv_p, _tiles(bp, sp))
        xs, kv_s, new_s = _trunk_layer(xs, ln_in, l, lp, sample, kv_s, _tiles(bs, ss))
        for i in range(4):
            st_p[i].append(new_p[i])
            st_s[i].append(new_s[i])
    hre_p, him_p, sc_p, ff_p = [jnp.stack(a, axis=0) for a in st_p]
    hre_s, him_s, sc_s, ff_s = [jnp.stack(a, axis=0) for a in st_s]
    heads = lambda a: a.reshape(a.shape[:-1] + (N_HEADS, HEAD_DIM))
    return (xp, xs, heads(kv_p[0]), heads(kv_p[1]), heads(kv_s[0]), heads(kv_s[1]),
            hre_p, him_p, hre_s, him_s, sc_p, sc_s, ff_p, ff_s)
```
